```python
import math
import numpy as np
import jax
import jax.numpy as jnp
from jax import lax

D_MODEL = 1024
BATCH = 16
SEQ = 2048
DEPTH = 4

GRID_W = 64
CTX_LEN = 256
DN_HEADS = 4
DN_HEAD_DIM = 128
DN_WIDTH = DN_HEADS * DN_HEAD_DIM
DN_CONV = 3
DN_CHUNK = 64
SC_WIDTH = 512
SC_CONV = 3
NA_HEADS = 8
NA_HEAD_DIM = 64
NA_WIDTH = NA_HEADS * NA_HEAD_DIM
NA_WIN_R = 8
NA_WIN_C = 16
NA_QBLK_C = 16
NA_KBLK_C = 32
D_FF = 2816
FFN_CONV = 3
N_BRANCH = 3
ROPE_BASE = 10000.0
EPS = 1e-6
IN_SPLITS = (3 * DN_WIDTH, DN_WIDTH, 2 * DN_HEADS, 2 * DN_HEADS, 3 * SC_WIDTH, 3 * NA_WIDTH, N_BRANCH * D_MODEL)
N_IN = sum(IN_SPLITS)

kernel_name = 'hybrid_dit_gdn_shortconv_natten_convffn'


def rms_norm(x, g):
    x32 = x.astype(jnp.float32)
    y = x32 * lax.rsqrt(jnp.mean(x32 * x32, axis=-1, keepdims=True) + EPS)
    return (y * g.astype(jnp.float32)).astype(x.dtype)


def l2norm(x):
    x32 = x.astype(jnp.float32)
    return x32 * lax.rsqrt(jnp.sum(x32 * x32, axis=-1, keepdims=True) + EPS)


def dwconv(x, w):
    K = w.shape[0]
    T = x.shape[1]
    lo = (K - 1) // 2
    xp = jnp.pad(x, ((0, 0), (lo, K - 1 - lo), (0, 0)))
    out = xp[:, 0:T] * w[0]
    for i in range(1, K):
        out = out + xp[:, i:i + T] * w[i]
    return out


def split_cols(p):
    pts = np.cumsum(IN_SPLITS)[:-1].tolist()
    return jnp.split(p, pts, axis=-1)


def axial_rope(T, dim):
    t = jnp.arange(T, dtype=jnp.int32)
    rows = (t // GRID_W).astype(jnp.float32)
    cols = (t % GRID_W).astype(jnp.float32)
    nf = dim // 4
    inv = ROPE_BASE ** (-jnp.arange(nf, dtype=jnp.float32) / nf)
    ang = jnp.concatenate([rows[:, None] * inv, cols[:, None] * inv], axis=-1)
    return jnp.cos(ang), jnp.sin(ang)


def apply_rope(x, cos, sin):
    x1 = x[..., 0::2]
    x2 = x[..., 1::2]
    return jnp.stack([x1 * cos - x2 * sin, x1 * sin + x2 * cos], axis=-1).reshape(x.shape)


def chunk_gated_delta(q, k, v, g, beta, S0):
    B, H, T, dk = q.shape
    dv = v.shape[-1]
    C = DN_CHUNK
    N = T // C
    q = (q * dk ** -0.5).reshape(B, H, N, C, dk)
    k = k.reshape(B, H, N, C, dk)
    v = v.reshape(B, H, N, C, dv)
    beta = beta.reshape(B, H, N, C)
    gc = jnp.cumsum(g.reshape(B, H, N, C), axis=-1)
    tri_incl = jnp.tril(jnp.ones((C, C), dtype=bool))
    tri_strict = jnp.tril(jnp.ones((C, C), dtype=bool), -1)
    diff = gc[..., :, None] - gc[..., None, :]
    decay = jnp.where(tri_incl, jnp.exp(jnp.where(tri_incl, diff, 0.0)), 0.0)
    kk = jnp.einsum('bhncd,bhnsd->bhncs', k, k)
    lmat = jnp.where(tri_strict, beta[..., :, None] * kk * decay, 0.0) + jnp.eye(C, dtype=jnp.float32)
    u = lax.linalg.triangular_solve(lmat, v * beta[..., None], left_side=True, lower=True, unit_diagonal=True)
    w = lax.linalg.triangular_solve(lmat, k * (beta * jnp.exp(gc))[..., None], left_side=True, lower=True, unit_diagonal=True)
    qk = jnp.einsum('bhncd,bhnsd->bhncs', q, k) * decay
    q_dec = q * jnp.exp(gc)[..., None]
    k_dec = k * jnp.exp(gc[..., -1:] - gc)[..., None]
    g_last = jnp.exp(gc[..., -1])

    def step(S, inp):
        u_i, w_i, qk_i, qd_i, kd_i, gl_i = inp
        v_new = u_i - jnp.einsum('bhck,bhkv->bhcv', w_i, S)
        o_i = jnp.einsum('bhck,bhkv->bhcv', qd_i, S) + jnp.einsum('bhcs,bhsv->bhcv', qk_i, v_new)
        S = S * gl_i[..., None, None] + jnp.einsum('bhck,bhcv->bhkv', kd_i, v_new)
        return S, o_i

    to_scan = lambda t: jnp.moveaxis(t, 2, 0)
    S_fin, o = lax.scan(step, S0, (to_scan(u), to_scan(w), to_scan(qk), to_scan(q_dec), to_scan(k_dec), to_scan(g_last)))
    o = jnp.moveaxis(o, 0, 2).reshape(B, H, T, dv)
    return o, S_fin


def gdn_prepare(qkv, a, b, conv_w, a_log, dt_bias, rope):
    B, T, _ = qkv.shape
    qkv = jax.nn.silu(dwconv(qkv, conv_w)).astype(jnp.float32)
    q, k, v = jnp.split(qkv, 3, axis=-1)
    heads = lambda t: t.reshape(B, T, DN_HEADS, DN_HEAD_DIM).transpose(0, 2, 1, 3)
    q = l2norm(heads(q))
    k = l2norm(heads(k))
    v = heads(v)
    if rope is not None:
        q = apply_rope(q, *rope)
        k = apply_rope(k, *rope)
    a = a.astype(jnp.float32).reshape(B, T, 2, DN_HEADS).transpose(2, 0, 3, 1)
    b = b.astype(jnp.float32).reshape(B, T, 2, DN_HEADS).transpose(2, 0, 3, 1)
    g = -jnp.exp(a_log.astype(jnp.float32))[:, None, :, None] * jax.nn.softplus(a + dt_bias.astype(jnp.float32)[:, None, :, None])
    beta = jax.nn.sigmoid(b)
    return q, k, v, g, beta


def gdn_bidir(q, k, v, g, beta, S0f, S0b):
    o_f, S_f = chunk_gated_delta(q, k, v, g[0], beta[0], S0f)
    flip = lambda t: jnp.flip(t, axis=2)
    o_b, S_b = chunk_gated_delta(flip(q), flip(k), flip(v), flip(g[1]), flip(beta[1]), S0b)
    return o_f + flip(o_b), S_f, S_b


def gdn_output(o, z, norm_g):
    B, H, T, dv = o.shape
    o = o.transpose(0, 2, 1, 3)
    o = o * lax.rsqrt(jnp.mean(o * o, axis=-1, keepdims=True) + EPS) * norm_g.astype(jnp.float32)
    o = o * jax.nn.silu(z.astype(jnp.float32)).reshape(B, T, H, dv)
    return o.reshape(B, T, H * dv).astype(z.dtype)


def short_conv(sc_in, conv_w):
    bg, cg, hh = jnp.split(sc_in, 3, axis=-1)
    return bg * dwconv(cg * hh, conv_w)


def neighbourhood_attention(q, k, v, kc, vc, rpb):
    B, T, H, dh = q.shape
    rows = T // GRID_W
    wr = min(NA_WIN_R, rows)
    to_grid = lambda t: t.reshape(B, rows, GRID_W, H, dh).transpose(0, 3, 1, 2, 4)
    qg, kg, vg = to_grid(q), to_grid(k), to_grid(v)
    ncb = GRID_W // NA_QBLK_C
    qcol = np.arange(GRID_W).reshape(ncb, NA_QBLK_C)
    kstart = np.clip(np.arange(ncb) * NA_QBLK_C - NA_WIN_C // 2, 0, GRID_W - NA_KBLK_C)
    kcol = kstart[:, None] + np.arange(NA_KBLK_C)
    cstart = np.clip(qcol - NA_WIN_C // 2, 0, GRID_W - NA_WIN_C)
    col_ok = (kcol[:, None, :] >= cstart[..., None]) & (kcol[:, None, :] < cstart[..., None] + NA_WIN_C)
    col_idx = np.clip(kcol[:, None, :] - qcol[:, :, None] + NA_WIN_C - 1, 0, 2 * NA_WIN_C - 2)
    nl = wr * NA_KBLK_C
    mask = jnp.asarray(np.broadcast_to(col_ok[:, :, None, :], (ncb, NA_QBLK_C, wr, NA_KBLK_C)).reshape(ncb, NA_QBLK_C, nl))
    rpb_c = rpb[:, :, col_idx]
    scale = dh ** -0.5

    def row_block(r):
        rs = jnp.clip(r - wr // 2, 0, rows - wr)
        kr = lax.dynamic_slice_in_dim(kg, rs, wr, axis=2)[:, :, :, kcol]
        vr = lax.dynamic_slice_in_dim(vg, rs, wr, axis=2)[:, :, :, kcol]
        kr = kr.transpose(0, 1, 3, 2, 4, 5).reshape(B, H, ncb, nl, dh)
        vr = vr.transpose(0, 1, 3, 2, 4, 5).reshape(B, H, ncb, nl, dh)
        qr = lax.dynamic_index_in_dim(qg, r, axis=2, keepdims=False).reshape(B, H, ncb, NA_QBLK_C, dh)
        ridx = rs + jnp.arange(wr) - r + NA_WIN_R - 1
        bias = rpb_c[:, ridx].transpose(0, 2, 3, 1, 4).reshape(H, ncb, NA_QBLK_C, nl).astype(jnp.float32)
        s_loc = jnp.einsum('bhnqd,bhnkd->bhnqk', qr, kr).astype(jnp.float32) * scale + bias
        s_loc = jnp.where(mask, s_loc, -1e30)
        s_ctx = jnp.einsum('bhnqd,bhcd->bhnqc', qr, kc).astype(jnp.float32) * scale
        p = jax.nn.softmax(jnp.concatenate([s_loc, s_ctx], axis=-1), axis=-1).astype(v.dtype)
        o = jnp.einsum('bhnqk,bhnkd->bhnqd', p[..., :nl], vr) + jnp.einsum('bhnqc,bhcd->bhnqd', p[..., nl:], vc)
        return o.reshape(B, H, GRID_W, dh)

    o = lax.map(row_block, jnp.arange(rows))
    return o.transpose(1, 0, 3, 2, 4).reshape(B, T, H * dh)


def context_attention(qc, kc, vc):
    B, H, L, dh = qc.shape
    s = jnp.einsum('bhqd,bhkd->bhqk', qc, kc).astype(jnp.float32) * dh ** -0.5
    p = jax.nn.softmax(s, axis=-1).astype(vc.dtype)
    return jnp.einsum('bhqk,bhkd->bhqd', p, vc).transpose(0, 2, 1, 3).reshape(B, L, H * dh)


def merge(y_a, y_b, y_c, gates, w_pa, w_pb, w_pc, w_o):
    g_a, g_b, g_c = jnp.split(jax.nn.sigmoid(gates), N_BRANCH, axis=-1)
    return (g_a * (y_a @ w_pa) + g_b * (y_b @ w_pb) + g_c * (y_c @ w_pc)) @ w_o


def token_mixing(h, hc, w_in, dn_conv_w, dn_a_log, dn_dt_bias, dn_norm_g, sc_conv_w, na_rpb, w_pa, w_pb, w_pc, w_o, rope, need_ctx):
    B, T, _ = h.shape
    L = hc.shape[1]
    qkv, z, a, b, sc_in, na_in, gates = split_cols(h @ w_in)
    qkv_c, z_c, a_c, b_c, sc_in_c, na_in_c, gates_c = split_cols(hc @ w_in)
    cq, ck, cv, cg, cb = gdn_prepare(qkv_c, a_c, b_c, dn_conv_w, dn_a_log, dn_dt_bias, None)
    S0 = jnp.zeros((B, DN_HEADS, DN_HEAD_DIM, DN_HEAD_DIM), jnp.float32)
    o_c, S_f, S_b = gdn_bidir(cq, ck, cv, cg, cb, S0, S0)
    lq, lk, lv, lg, lb = gdn_prepare(qkv, a, b, dn_conv_w, dn_a_log, dn_dt_bias, rope)
    o_l, _, _ = gdn_bidir(lq, lk, lv, lg, lb, S_f, S_b)
    y_a = gdn_output(o_l, z, dn_norm_g)
    y_b = short_conv(sc_in, sc_conv_w)
    nq, nk, nv = [t.reshape(B, T, NA_HEADS, NA_HEAD_DIM) for t in jnp.split(na_in, 3, axis=-1)]
    nqc, nkc, nvc = [t.reshape(B, L, NA_HEADS, NA_HEAD_DIM).transpose(0, 2, 1, 3) for t in jnp.split(na_in_c, 3, axis=-1)]
    y_c = neighbourhood_attention(nq, nk, nv, nkc, nvc, na_rpb)
    y = merge(y_a, y_b, y_c, gates, w_pa, w_pb, w_pc, w_o)
    if not need_ctx:
        return y, None
    yc = merge(gdn_output(o_c, z_c, dn_norm_g), short_conv(sc_in_c, sc_conv_w), context_attention(nqc, nkc, nvc), gates_c, w_pa, w_pb, w_pc, w_o)
    return y, yc


def conv_ffn(h, w_up, conv_w, conv_b, w_down):
    u = dwconv(h @ w_up, conv_w) + conv_b
    a, b = jnp.split(u, 2, axis=-1)
    return (jax.nn.silu(a) * b) @ w_down


def setup_inputs(seed: int = 0) -> dict:
    key = jax.random.key(seed)
    ks = jax.random.split(key, 24)
    f32 = jnp.float32
    nrm = lambda k, shape, s: jax.random.normal(k, shape, f32) * s
    dt = jnp.exp(jax.random.uniform(ks[11], (DEPTH, 2, DN_HEADS), f32, math.log(1e-3), math.log(1e-1)))
    return {
        'x': nrm(ks[0], (BATCH, SEQ, D_MODEL), 1.0),
        'c': nrm(ks[1], (BATCH, D_MODEL), 1.0),
        'ctx': nrm(ks[2], (BATCH, CTX_LEN, D_MODEL), 1.0),
        'c_ctx': nrm(ks[3], (D_MODEL,), 1.0),
        'norm1_g': 1.0 + nrm(ks[4], (DEPTH, D_MODEL), 0.02),
        'norm2_g': 1.0 + nrm(ks[5], (DEPTH, D_MODEL), 0.02),
        'w_ada': nrm(ks[6], (DEPTH, D_MODEL, 6 * D_MODEL), 0.5 * D_MODEL ** -0.5),
        'b_ada': nrm(ks[7], (DEPTH, 6 * D_MODEL), 0.01),
        'w_in': nrm(ks[8], (DEPTH, D_MODEL, N_IN), D_MODEL ** -0.5),
        'dn_conv_w': nrm(ks[9], (DEPTH, DN_CONV, 3 * DN_WIDTH), DN_CONV ** -0.5),
        'dn_a_log': jnp.log(jax.random.uniform(ks[10], (DEPTH, 2, DN_HEADS), f32, 1.0, 16.0)),
        'dn_dt_bias': dt + jnp.log(-jnp.expm1(-dt)),
        'dn_norm_g': 1.0 + nrm(ks[12], (DEPTH, DN_HEAD_DIM), 0.02),
        'sc_conv_w': nrm(ks[13], (DEPTH, SC_CONV, SC_WIDTH), SC_CONV ** -0.5),
        'na_rpb': nrm(ks[14], (DEPTH, NA_HEADS, 2 * NA_WIN_R - 1, 2 * NA_WIN_C - 1), 0.1),
        'w_pa': nrm(ks[15], (DEPTH, DN_WIDTH, D_MODEL), DN_WIDTH ** -0.5),
        'w_pb': nrm(ks[16], (DEPTH, SC_WIDTH, D_MODEL), SC_WIDTH ** -0.5),
        'w_pc': nrm(ks[17], (DEPTH, NA_WIDTH, D_MODEL), NA_WIDTH ** -0.5),
        'w_o': nrm(ks[18], (DEPTH, D_MODEL, D_MODEL), D_MODEL ** -0.5),
        'w_up': nrm(ks[19], (DEPTH, D_MODEL, 2 * D_FF), D_MODEL ** -0.5),
        'ffn_conv_w': nrm(ks[20], (DEPTH, FFN_CONV, 2 * D_FF), FFN_CONV ** -0.5),
        'ffn_conv_b': nrm(ks[21], (DEPTH, 2 * D_FF), 0.01),
        'w_down': nrm(ks[22], (DEPTH, D_FF, D_MODEL), D_FF ** -0.5),
        'final_norm_g': 1.0 + nrm(ks[23], (D_MODEL,), 0.02),
    }


def reference(x, c, ctx, c_ctx, norm1_g, norm2_g, w_ada, b_ada, w_in, dn_conv_w, dn_a_log, dn_dt_bias, dn_norm_g, sc_conv_w, na_rpb, w_pa, w_pb, w_pc, w_o, w_up, ffn_conv_w, ffn_conv_b, w_down, final_norm_g):
    T = x.shape[1]
    rope = axial_rope(T, DN_HEAD_DIM)
    xc = ctx
    for l in range(DEPTH):
        need_ctx = l < DEPTH - 1
        mod = (jax.nn.silu(c) @ w_ada[l] + b_ada[l])[:, None, :]
        mod_c = jax.nn.silu(c_ctx) @ w_ada[l] + b_ada[l]
        sh1, sc1, gt1, sh2, sc2, gt2 = jnp.split(mod, 6, axis=-1)
        sh1c, sc1c, gt1c, sh2c, sc2c, gt2c = jnp.split(mod_c, 6, axis=-1)
        h = rms_norm(x, norm1_g[l]) * (1.0 + sc1) + sh1
        hc = rms_norm(xc, norm1_g[l]) * (1.0 + sc1c) + sh1c
        y, yc = token_mixing(h, hc, w_in[l], dn_conv_w[l], dn_a_log[l], dn_dt_bias[l], dn_norm_g[l], sc_conv_w[l], na_rpb[l], w_pa[l], w_pb[l], w_pc[l], w_o[l], rope, need_ctx)
        x = x + gt1 * y
        h2 = rms_norm(x, norm2_g[l]) * (1.0 + sc2) + sh2
        x = x + gt2 * conv_ffn(h2, w_up[l], ffn_conv_w[l], ffn_conv_b[l], w_down[l])
        if need_ctx:
            xc = xc + gt1c * yc
            h2c = rms_norm(xc, norm2_g[l]) * (1.0 + sc2c) + sh2c
            xc = xc + gt2c * conv_ffn(h2c, w_up[l], ffn_conv_w[l], ffn_conv_b[l], w_down[l])
    return rms_norm(x, final_norm_g)
```

```python
import functools

import numpy as np
import jax
import jax.numpy as jnp
from jax import lax
from jax.experimental import pallas as pl
from jax.experimental.pallas import tpu as pltpu

F32 = jnp.float32
BF16 = jnp.bfloat16

GRID_W = 64
DN_HEADS = 4
DN_HEAD_DIM = 128
DN_WIDTH = DN_HEADS * DN_HEAD_DIM
DN_CHUNK = 64
SC_WIDTH = 512
NA_HEADS = 8
NA_HEAD_DIM = 64
NA_WIDTH = NA_HEADS * NA_HEAD_DIM
NA_WIN_R = 8
NA_WIN_C = 16
ROPE_BASE = 10000.0
EPS = 1e-6
NEG = -1e30

LANE = 128
SUBLANE = 8
HALO = SUBLANE
VMEM_LIMIT = 48 * 1024 * 1024

P_COLS = 3 * DN_WIDTH + DN_WIDTH + 3 * SC_WIDTH + 3 * NA_WIDTH + 3 * 1024 + LANE
AB_BLK = (P_COLS - LANE) // LANE
NA_BLK0 = (3 * DN_WIDTH + DN_WIDTH + 3 * SC_WIDTH) // LANE


def _dot(a, b):
    return jnp.dot(a.astype(BF16), b.astype(BF16), preferred_element_type=F32)


def _dot_nt(a, b):
    return lax.dot_general(a.astype(BF16), b.astype(BF16), (((1,), (1,)), ((), ())), preferred_element_type=F32)


def _dot_tn(a, b):
    return lax.dot_general(a.astype(BF16), b.astype(BF16), (((0,), (0,)), ((), ())), preferred_element_type=F32)


def _sigmoid(x):
    return 1.0 / (1.0 + jnp.exp(-x))


def _silu(x):
    return x * _sigmoid(x)


def _softplus(x):
    return jnp.maximum(x, 0.0) + jnp.log1p(jnp.exp(-jnp.abs(x)))


def _rms(x, eps=EPS):
    return x * lax.rsqrt(jnp.mean(x * x, axis=-1, keepdims=True) + eps)


def _params(*sem):
    return pltpu.CompilerParams(dimension_semantics=sem, vmem_limit_bytes=VMEM_LIMIT)


def _dwconv3_rows(full, w_ref, n):
    rows = full.shape[0]
    dn = pltpu.roll(full, 1, axis=0)
    up = pltpu.roll(full, rows - 1, axis=0)
    out = dn * w_ref[0:1, :] + full * w_ref[1:2, :] + up * w_ref[2:3, :]
    return out[HALO:HALO + n]


def _ada_kernel(c_ref, w_ref, b_ref, o_ref):
    o_ref[...] = _dot(_silu(c_ref[...]), w_ref[...]) + b_ref[...]


def _ada_call(cc, w_ada, b_ada):
    depth, d, n6 = w_ada.shape
    r = cc.shape[0]
    tn = 1536
    return pl.pallas_call(
        _ada_kernel,
        grid=(depth, n6 // tn),
        in_specs=[
            pl.BlockSpec((r, d), lambda l, j: (0, 0)),
            pl.BlockSpec((None, d, tn), lambda l, j: (l, 0, j)),
            pl.BlockSpec((None, 1, tn), lambda l, j: (l, 0, j)),
        ],
        out_specs=pl.BlockSpec((None, r, tn), lambda l, j: (l, 0, j)),
        out_shape=jax.ShapeDtypeStruct((depth, r, n6), F32),
        compiler_params=_params("parallel", "parallel"),
        name="ada_mod",
    )(cc, w_ada, b_ada.reshape(depth, 1, n6))


def _inproj_kernel(x_ref, g_ref, sh_ref, sc_ref, w_ref, o_ref, h_ref):
    @pl.when(pl.program_id(2) == 0)
    def _():
        h = (_rms(x_ref[...]) * g_ref[...]) * (1.0 + sc_ref[...]) + sh_ref[...]
        h_ref[...] = h.astype(BF16)

    o_ref[...] = jnp.dot(h_ref[...], w_ref[...], preferred_element_type=F32)


def _inproj_call(x, g, mod, mrow, w, tm):
    b, t, d = x.shape
    n = w.shape[1]
    tn = 640
    return pl.pallas_call(
        _inproj_kernel,
        grid=(b, t // tm, n // tn),
        in_specs=[
            pl.BlockSpec((None, tm, d), lambda bi, i, j: (bi, i, 0)),
            pl.BlockSpec((1, d), lambda bi, i, j: (0, 0)),
            pl.BlockSpec((None, 1, d), lambda bi, i, j: (mrow(bi), 0, 0)),
            pl.BlockSpec((None, 1, d), lambda bi, i, j: (mrow(bi), 0, 1)),
            pl.BlockSpec((d, tn), lambda bi, i, j: (0, j)),
        ],
        out_specs=pl.BlockSpec((None, tm, tn), lambda bi, i, j: (bi, i, j)),
        out_shape=jax.ShapeDtypeStruct((b, t, n), F32),
        scratch_shapes=[pltpu.VMEM((tm, d), BF16)],
        compiler_params=_params("parallel", "parallel", "arbitrary"),
        name="inproj",
    )(x, g, mod, mod, w)


def _swap_pairs(x, even):
    n = x.shape[-1]
    return jnp.where(even, pltpu.roll(x, n - 1, axis=1), pltpu.roll(x, 1, axis=1))


def _gdn_kernel(*refs, nc, use_rope, emit_state):
    it = iter(refs)
    xs = [[next(it) for _ in range(4)] for _ in range(2)]
    cw_ref = next(it)
    par_ref = next(it)
    ropes = [[next(it), next(it)] for _ in range(2)] if use_rope else None
    s0_ref = next(it)
    o_refs = [next(it), next(it)]
    sout_ref = next(it) if emit_state else None
    s_scr = next(it)

    c = DN_CHUNK
    i = pl.program_id(1)

    @pl.when(i == 0)
    def _():
        s_scr[...] = s0_ref[...]

    ri = lax.broadcasted_iota(jnp.int32, (c, c), 0)
    ci = lax.broadcasted_iota(jnp.int32, (c, c), 1)
    eye = ri == ci
    lower = ri >= ci
    upper = ri <= ci
    even = (lax.broadcasted_iota(jnp.int32, (c, DN_HEAD_DIM), 1) % 2) == 0
    scale = DN_HEAD_DIM ** -0.5
    neg_a = -jnp.exp(par_ref[0:1, :])
    dt_b = par_ref[1:2, :]

    for d in range(2):
        x_ref, xp_ref, xn_ref, ab_ref = xs[d]
        chunk = i if d == 0 else nc - 1 - i
        has_prev = (chunk > 0).astype(F32)
        has_next = (chunk < nc - 1).astype(F32)
        full = jnp.concatenate([xp_ref[...] * has_prev, x_ref[...], xn_ref[...] * has_next], axis=0)
        act = _silu(_dwconv3_rows(full, cw_ref, c))
        ab = ab_ref[...]
        g_all = neg_a * _softplus(ab + dt_b)
        beta_all = _sigmoid(ab)
        incl = lower if d == 0 else upper
        incl_t = upper if d == 0 else lower
        strict = incl & (~eye)
        for h in range(DN_HEADS):
            sl = slice(h * DN_HEAD_DIM, (h + 1) * DN_HEAD_DIM)
            qh = act[:, sl]
            kh = act[:, DN_WIDTH + h * DN_HEAD_DIM:DN_WIDTH + (h + 1) * DN_HEAD_DIM]
            vh = act[:, 2 * DN_WIDTH + h * DN_HEAD_DIM:2 * DN_WIDTH + (h + 1) * DN_HEAD_DIM]
            qh = qh * lax.rsqrt(jnp.sum(qh * qh, axis=-1, keepdims=True) + EPS)
            kh = kh * lax.rsqrt(jnp.sum(kh * kh, axis=-1, keepdims=True) + EPS)
            if use_rope:
                cos = ropes[d][0][...]
                sin = ropes[d][1][...]
                qh = qh * cos + _swap_pairs(qh, even) * sin
                kh = kh * cos + _swap_pairs(kh, even) * sin
            qh = qh * scale
            col = d * DN_HEADS + h
            g_col = g_all[:, col:col + 1]
            beta = beta_all[:, 2 * DN_HEADS + col:2 * DN_HEADS + col + 1]
            g_cols = jnp.broadcast_to(g_col, (c, c))
            g_row = jnp.sum(jnp.where(eye, g_cols, 0.0), axis=0, keepdims=True)
            g_rows = jnp.broadcast_to(g_row, (c, c))
            gc_col = jnp.sum(jnp.where(incl, g_rows, 0.0), axis=1, keepdims=True)
            gc_row = jnp.sum(jnp.where(incl_t, g_cols, 0.0), axis=0, keepdims=True)
            g_tot = jnp.sum(g_col, axis=0, keepdims=True)
            decay = jnp.where(incl, jnp.exp(jnp.where(incl, gc_col - gc_row, 0.0)), 0.0)
            kk = _dot_nt(kh, kh)
            amat = jnp.where(strict, beta * kk * decay, 0.0)
            tinv = jnp.where(eye, 1.0, 0.0)
            s = 1
            while s < c:
                same2 = (ri // (2 * s)) == (ci // (2 * s))
                diff1 = (ri // s) != (ci // s)
                a_off = jnp.where(same2 & diff1, amat, 0.0)
                tinv = tinv - _dot(_dot(tinv, a_off), tinv)
                s *= 2
            e_gc = jnp.exp(gc_col)
            u = _dot(tinv, vh * beta)
            w = _dot(tinv, kh * (beta * e_gc))
            qk = _dot_nt(qh, kh) * decay
            q_dec = qh * e_gc
            k_dec = kh * jnp.exp(g_tot - gc_col)
            state = s_scr[d, h]
            v_new = u - _dot(w, state)
            o_refs[d][:, sl] = _dot(q_dec, state) + _dot(qk, v_new)
            s_scr[d, h] = state * jnp.exp(g_tot) + _dot_tn(k_dec, v_new)

    if emit_state:
        @pl.when(i == nc - 1)
        def _():
            sout_ref[...] = s_scr[...]


def _gdn_call(p, conv_w, par, s0, rope, emit_state):
    b, t, _ = p.shape
    c = DN_CHUNK
    nc = t // c
    hb = c // HALO
    nhb = t // HALO
    qkv_w = 3 * DN_WIDTH

    def chunk_of(d, i):
        return i if d == 0 else nc - 1 - i

    in_specs = []
    args = []
    for d in range(2):
        in_specs += [
            pl.BlockSpec((None, c, qkv_w), lambda bi, i, d=d: (bi, chunk_of(d, i), 0)),
            pl.BlockSpec((None, HALO, qkv_w), lambda bi, i, d=d: (bi, jnp.maximum(chunk_of(d, i) * hb - 1, 0), 0)),
            pl.BlockSpec((None, HALO, qkv_w), lambda bi, i, d=d: (bi, jnp.minimum(chunk_of(d, i) * hb + hb, nhb - 1), 0)),
            pl.BlockSpec((None, c, LANE), lambda bi, i, d=d: (bi, chunk_of(d, i), AB_BLK)),
        ]
        args += [p, p, p, p]
    in_specs += [pl.BlockSpec(conv_w.shape, lambda bi, i: (0, 0)), pl.BlockSpec(par.shape, lambda bi, i: (0, 0))]
    args += [conv_w, par]
    if rope is not None:
        for d in range(2):
            in_specs += [pl.BlockSpec((c, DN_HEAD_DIM), lambda bi, i, d=d: (chunk_of(d, i), 0))] * 2
            args += [rope[0], rope[1]]
    sshape = (2, DN_HEADS, DN_HEAD_DIM, DN_HEAD_DIM)
    in_specs += [pl.BlockSpec((None,) + sshape, lambda bi, i: (bi, 0, 0, 0, 0))]
    args += [s0]
    out_specs = [
        pl.BlockSpec((None, c, DN_WIDTH), lambda bi, i: (bi, i, 0)),
        pl.BlockSpec((None, c, DN_WIDTH), lambda bi, i: (bi, nc - 1 - i, 0)),
    ]
    out_shape = [jax.ShapeDtypeStruct((b, t, DN_WIDTH), F32)] * 2
    if emit_state:
        out_specs += [pl.BlockSpec((None,) + sshape, lambda bi, i: (bi, 0, 0, 0, 0))]
        out_shape += [jax.ShapeDtypeStruct((b,) + sshape, F32)]
    return pl.pallas_call(
        functools.partial(_gdn_kernel, nc=nc, use_rope=rope is not None, emit_state=emit_state),
        grid=(b, nc),
        in_specs=in_specs,
        out_specs=out_specs,
        out_shape=out_shape,
        scratch_shapes=[pltpu.VMEM(sshape, F32)],
        compiler_params=_params("parallel", "arbitrary"),
        name="gdn_ctx" if emit_state else "gdn_lat",
    )(*args)


def _na_kernel(q_ref, k_ref, v_ref, kc_ref, vc_ref, bias_ref, o_ref, *, rows):
    w = GRID_W
    nl = NA_WIN_R * w
    scale = NA_HEAD_DIM ** -0.5
    head0 = lax.broadcasted_iota(jnp.int32, (w, LANE), 1) < NA_HEAD_DIM
    kc = kc_ref[...]
    vc = vc_ref[...]

    def row_block(r, carry):
        rs = jnp.clip(r - NA_WIN_R // 2, 0, rows - NA_WIN_R)
        d0 = rs - r + NA_WIN_R - 1
        q2 = q_ref[pl.ds(pl.multiple_of(r * w, w), w), :] * scale
        kw = k_ref[pl.ds(pl.multiple_of(rs * w, w), nl), :]
        vw = v_ref[pl.ds(pl.multiple_of(rs * w, w), nl), :]
        out = None
        for hh in range(2):
            sel = head0 if hh == 0 else ~head0
            qm = jnp.where(sel, q2, 0.0)
            s_loc = _dot_nt(qm, kw) + bias_ref[hh, d0]
            s_ctx = _dot_nt(qm, kc)
            m = jnp.maximum(jnp.max(s_loc, axis=-1, keepdims=True), jnp.max(s_ctx, axis=-1, keepdims=True))
            p_loc = jnp.exp(s_loc - m)
            p_ctx = jnp.exp(s_ctx - m)
            den = jnp.sum(p_loc, axis=-1, keepdims=True) + jnp.sum(p_ctx, axis=-1, keepdims=True)
            o = (_dot(p_loc, vw) + _dot(p_ctx, vc)) / den
            out = o if out is None else jnp.where(sel, o, out)
        o_ref[pl.ds(pl.multiple_of(r * w, w), w), :] = out
        return carry

    lax.fori_loop(0, rows, row_block, 0)


def _na_call(p, pc, bias):
    b, t, _ = p.shape
    l = pc.shape[1]
    rows = t // GRID_W
    npair = NA_HEADS // 2
    return pl.pallas_call(
        functools.partial(_na_kernel, rows=rows),
        grid=(npair, b),
        in_specs=[
            pl.BlockSpec((None, t, LANE), lambda j, bi: (bi, 0, NA_BLK0 + j)),
            pl.BlockSpec((None, t, LANE), lambda j, bi: (bi, 0, NA_BLK0 + npair + j)),
            pl.BlockSpec((None, t, LANE), lambda j, bi: (bi, 0, NA_BLK0 + 2 * npair + j)),
            pl.BlockSpec((None, l, LANE), lambda j, bi: (bi, 0, NA_BLK0 + npair + j)),
            pl.BlockSpec((None, l, LANE), lambda j, bi: (bi, 0, NA_BLK0 + 2 * npair + j)),
            pl.BlockSpec((None,) + bias.shape[1:], lambda j, bi: (j, 0, 0, 0, 0)),
        ],
        out_specs=pl.BlockSpec((None, t, LANE), lambda j, bi: (bi, 0, j)),
        out_shape=jax.ShapeDtypeStruct((b, t, NA_WIDTH), F32),
        compiler_params=_params("parallel", "parallel"),
        name="na_lat",
    )(p, p, p, pc, pc, bias)


def _ctx_attn_kernel(q_ref, k_ref, v_ref, o_ref):
    scale = NA_HEAD_DIM ** -0.5
    q2 = q_ref[...] * scale
    k2 = k_ref[...]
    v2 = v_ref[...]
    head0 = lax.broadcasted_iota(jnp.int32, q2.shape, 1) < NA_HEAD_DIM
    out = None
    for hh in range(2):
        sel = head0 if hh == 0 else ~head0
        s = _dot_nt(jnp.where(sel, q2, 0.0), k2)
        p = jnp.exp(s - jnp.max(s, axis=-1, keepdims=True))
        o = _dot(p, v2) / jnp.sum(p, axis=-1, keepdims=True)
        out = o if out is None else jnp.where(sel, o, out)
    o_ref[...] = out


def _ctx_attn_call(pc):
    b, l, _ = pc.shape
    npair = NA_HEADS // 2
    return pl.pallas_call(
        _ctx_attn_kernel,
        grid=(b, npair),
        in_specs=[
            pl.BlockSpec((None, l, LANE), lambda bi, j: (bi, 0, NA_BLK0 + j)),
            pl.BlockSpec((None, l, LANE), lambda bi, j: (bi, 0, NA_BLK0 + npair + j)),
            pl.BlockSpec((None, l, LANE), lambda bi, j: (bi, 0, NA_BLK0 + 2 * npair + j)),
        ],
        out_specs=pl.BlockSpec((None, l, LANE), lambda bi, j: (bi, 0, j)),
        out_shape=jax.ShapeDtypeStruct((b, l, NA_WIDTH), F32),
        compiler_params=_params("parallel", "parallel"),
        name="ctx_attn",
    )(pc, pc, pc)


def _merge_kernel(of_ref, ob_ref, z_ref, bg_ref, cg_ref, hh_ref, cgp_ref, cgn_ref, hhp_ref, hhn_ref, yc_ref,
                  ga_ref, gb_ref, gcg_ref, x_ref, gt_ref, wpa_ref, wpb_ref, wpc_ref, wo_ref, ng_ref, scw_ref,
                  o_ref, *, nt):
    i = pl.program_id(1)
    tm = x_ref.shape[0]
    o = of_ref[...] + ob_ref[...]
    z = z_ref[...]
    ya = []
    for h in range(DN_HEADS):
        sl = slice(h * DN_HEAD_DIM, (h + 1) * DN_HEAD_DIM)
        ya.append(_rms(o[:, sl]) * ng_ref[...] * _silu(z[:, sl]))
    y_a = jnp.concatenate(ya, axis=1)
    has_prev = (i > 0).astype(F32)
    has_next = (i < nt - 1).astype(F32)
    full = jnp.concatenate([cgp_ref[...] * hhp_ref[...] * has_prev, cg_ref[...] * hh_ref[...],
                            cgn_ref[...] * hhn_ref[...] * has_next], axis=0)
    y_b = bg_ref[...] * _dwconv3_rows(full, scw_ref, tm)
    y = (_sigmoid(ga_ref[...]) * _dot(y_a, wpa_ref[...]) + _sigmoid(gb_ref[...]) * _dot(y_b, wpb_ref[...])
         + _sigmoid(gcg_ref[...]) * _dot(yc_ref[...], wpc_ref[...]))
    o_ref[...] = x_ref[...] + gt_ref[...] * _dot(y, wo_ref[...])


def _merge_call(of, ob, p, yc, x, mod, mrow, wpa, wpb, wpc, wo, ng, scw, tm):
    b, t, d = x.shape
    nt = t // tm
    hb = tm // HALO
    nhb = t // HALO
    w5 = SC_WIDTH

    def tile(width, col):
        return pl.BlockSpec((None, tm, width), lambda bi, i: (bi, i, col))

    def prev(col):
        return pl.BlockSpec((None, HALO, w5), lambda bi, i: (bi, jnp.maximum(i * hb - 1, 0), col))

    def nxt(col):
        return pl.BlockSpec((None, HALO, w5), lambda bi, i: (bi, jnp.minimum(i * hb + hb, nhb - 1), col))

    def full(a):
        return pl.BlockSpec(a.shape, lambda bi, i: (0,) * a.ndim)

    in_specs = [
        tile(DN_WIDTH, 0), tile(DN_WIDTH, 0),
        tile(w5, 3), tile(w5, 4), tile(w5, 5), tile(w5, 6),
        prev(5), nxt(5), prev(6), nxt(6),
        tile(NA_WIDTH, 0),
        tile(d, 5), tile(d, 6), tile(d, 7),
        tile(d, 0),
        pl.BlockSpec((None, 1, d), lambda bi, i: (mrow(bi), 0, 2)),
        full(wpa), full(wpb), full(wpc), full(wo), full(ng), full(scw),
    ]
    return pl.pallas_call(
        functools.partial(_merge_kernel, nt=nt),
        grid=(b, nt),
        in_specs=in_specs,
        out_specs=pl.BlockSpec((None, tm, d), lambda bi, i: (bi, i, 0)),
        out_shape=jax.ShapeDtypeStruct((b, t, d), F32),
        compiler_params=_params("parallel", "parallel"),
        name="merge",
    )(of, ob, p, p, p, p, p, p, p, p, yc, p, p, p, x, mod, wpa, wpb, wpc, wo, ng, scw)


def _ffn_kernel(*refs, nt, nk, final):
    it = iter(refs)
    x_ref, xp_ref, xn_ref, g_ref, sh_ref, sc_ref, gt_ref = (next(it) for _ in range(7))
    wa_ref, wb_ref, cwa_ref, cwb_ref, cba_ref, cbb_ref, wd_ref = (next(it) for _ in range(7))
    fg_ref = next(it) if final else None
    o_ref, h_scr, acc_scr = next(it), next(it), next(it)
    i = pl.program_id(1)
    k = pl.program_id(2)
    tm = x_ref.shape[0]

    @pl.when(k == 0)
    def _():
        xfull = jnp.concatenate([xp_ref[...], x_ref[...], xn_ref[...]], axis=0)
        h = (_rms(xfull) * g_ref[...]) * (1.0 + sc_ref[...]) + sh_ref[...]
        h_scr[...] = h.astype(BF16)
        acc_scr[...] = jnp.zeros_like(acc_scr)

    row = lax.broadcasted_iota(jnp.int32, (tm + 2 * HALO, 1), 0)
    keep = ((row >= HALO) | (i > 0)) & ((row < tm + HALO) | (i < nt - 1))
    hfull = h_scr[...]
    ua = jnp.where(keep, jnp.dot(hfull, wa_ref[...], preferred_element_type=F32), 0.0)
    ub = jnp.where(keep, jnp.dot(hfull, wb_ref[...], preferred_element_type=F32), 0.0)
    a = _dwconv3_rows(ua, cwa_ref, tm) + cba_ref[...]
    bb = _dwconv3_rows(ub, cwb_ref, tm) + cbb_ref[...]
    acc_scr[...] += _dot(_silu(a) * bb, wd_ref[...])

    @pl.when(k == nk - 1)
    def _():
        out = x_ref[...] + gt_ref[...] * acc_scr[...]
        if final:
            out = _rms(out) * fg_ref[...]
        o_ref[...] = out


def _ffn_call(x, g, mod, mrow, w_up, cw, cb, w_down, tm, final_g=None):
    b, t, d = x.shape
    dff = w_down.shape[0]
    fc = 256
    nk = dff // fc
    nt = t // tm
    hb = tm // HALO
    nhb = t // HALO
    final = final_g is not None
    in_specs = [
        pl.BlockSpec((None, tm, d), lambda bi, i, k: (bi, i, 0)),
        pl.BlockSpec((None, HALO, d), lambda bi, i, k: (bi, jnp.maximum(i * hb - 1, 0), 0)),
        pl.BlockSpec((None, HALO, d), lambda bi, i, k: (bi, jnp.minimum(i * hb + hb, nhb - 1), 0)),
        pl.BlockSpec((1, d), lambda bi, i, k: (0, 0)),
        pl.BlockSpec((None, 1, d), lambda bi, i, k: (mrow(bi), 0, 3)),
        pl.BlockSpec((None, 1, d), lambda bi, i, k: (mrow(bi), 0, 4)),
        pl.BlockSpec((None, 1, d), lambda bi, i, k: (mrow(bi), 0, 5)),
        pl.BlockSpec((d, fc), lambda bi, i, k: (0, k)),
        pl.BlockSpec((d, fc), lambda bi, i, k: (0, nk + k)),
        pl.BlockSpec((3, fc), lambda bi, i, k: (0, k)),
        pl.BlockSpec((3, fc), lambda bi, i, k: (0, nk + k)),
        pl.BlockSpec((1, fc), lambda bi, i, k: (0, k)),
        pl.BlockSpec((1, fc), lambda bi, i, k: (0, nk + k)),
        pl.BlockSpec((fc, d), lambda bi, i, k: (k, 0)),
    ]
    args = [x, x, x, g, mod, mod, mod, w_up, w_up, cw, cw, cb, cb, w_down]
    if final:
        in_specs.append(pl.BlockSpec((1, d), lambda bi, i, k: (0, 0)))
        args.append(final_g)
    return pl.pallas_call(
        functools.partial(_ffn_kernel, nt=nt, nk=nk, final=final),
        grid=(b, nt, nk),
        in_specs=in_specs,
        out_specs=pl.BlockSpec((None, tm, d), lambda bi, i, k: (bi, i, 0)),
        out_shape=jax.ShapeDtypeStruct((b, t, d), F32),
        scratch_shapes=[pltpu.VMEM((tm + 2 * HALO, d), BF16), pltpu.VMEM((tm, d), F32)],
        compiler_params=_params("parallel", "parallel", "arbitrary"),
        name="ffn_final" if final else "ffn",
    )(*args)


def _rope_tables(t):
    tok = jnp.arange(t, dtype=jnp.int32)
    rows = (tok // GRID_W).astype(F32)
    cols = (tok % GRID_W).astype(F32)
    nf = DN_HEAD_DIM // 4
    inv = ROPE_BASE ** (-jnp.arange(nf, dtype=F32) / nf)
    ang = jnp.concatenate([rows[:, None] * inv, cols[:, None] * inv], axis=-1)
    cos = jnp.repeat(jnp.cos(ang), 2, axis=-1)
    sin = jnp.repeat(jnp.sin(ang), 2, axis=-1)
    sign = jnp.where(jnp.arange(DN_HEAD_DIM) % 2 == 0, -1.0, 1.0).astype(F32)
    return cos, sin * sign


def _na_bias_table(rpb):
    w = GRID_W
    qc = np.arange(w)[:, None]
    kc = np.arange(w)[None, :]
    cstart = np.clip(qc - NA_WIN_C // 2, 0, w - NA_WIN_C)
    valid = (kc >= cstart) & (kc < cstart + NA_WIN_C)
    cidx = np.clip(kc - qc + NA_WIN_C - 1, 0, 2 * NA_WIN_C - 2)
    tbl = jnp.where(jnp.asarray(valid), rpb[:, :, cidx], NEG)
    didx = np.arange(NA_WIN_R)[:, None] + np.arange(NA_WIN_R)[None, :]
    tbl = tbl[:, didx]
    tbl = tbl.transpose(0, 1, 3, 2, 4).reshape(NA_HEADS // 2, 2, NA_WIN_R, w, NA_WIN_R * w)
    return tbl


def _permute_w_in(w_in):
    s = np.cumsum([0, 3 * DN_WIDTH, DN_WIDTH, 2 * DN_HEADS, 2 * DN_HEADS, 3 * SC_WIDTH, 3 * NA_WIDTH, 3 * 1024])
    qkv, z, a, b, sc, na, gates = (w_in[..., s[j]:s[j + 1]] for j in range(7))
    pad = jnp.zeros(w_in.shape[:-1] + (LANE - 4 * DN_HEADS,), w_in.dtype)
    return jnp.concatenate([qkv, z, sc, na, gates, a, b, pad], axis=-1)


def kernel(x, c, ctx, c_ctx, norm1_g, norm2_g, w_ada, b_ada, w_in, dn_conv_w, dn_a_log, dn_dt_bias, dn_norm_g, sc_conv_w, na_rpb, w_pa, w_pb, w_pc, w_o, w_up, ffn_conv_w, ffn_conv_b, w_down, final_norm_g):
    bsz, t, d = x.shape
    depth = w_in.shape[0]
    assert d == 1024 and t % (NA_WIN_R * GRID_W) == 0 and ctx.shape[1] % DN_CHUNK == 0

    mod_rows = -(-(bsz + 1) // SUBLANE) * SUBLANE
    cc = jnp.zeros((mod_rows, d), F32).at[:bsz].set(c).at[bsz].set(c_ctx)
    mod = _ada_call(cc, w_ada, b_ada).reshape(depth, mod_rows, 1, 6 * d)
    lat_row = lambda bi: bi
    ctx_row = lambda bi: bsz

    w_in_p = _permute_w_in(w_in).astype(BF16)
    rope = _rope_tables(t)
    par = jnp.zeros((depth, SUBLANE, LANE), F32)
    par = par.at[:, 0, :2 * DN_HEADS].set(dn_a_log.reshape(depth, -1)).at[:, 1, :2 * DN_HEADS].set(dn_dt_bias.reshape(depth, -1))
    s_zero = jnp.zeros((bsz, 2, DN_HEADS, DN_HEAD_DIM, DN_HEAD_DIM), F32)
    tm_lat = 1024 if t % 1024 == 0 else 512
    tm_ctx = ctx.shape[1]

    xc = ctx
    for l in range(depth):
        need_ctx = l < depth - 1
        g1 = norm1_g[l][None]
        g2 = norm2_g[l][None]
        wpa, wpb, wpc, wo = (a[l].astype(BF16) for a in (w_pa, w_pb, w_pc, w_o))
        wup, wdn = w_up[l].astype(BF16), w_down[l].astype(BF16)
        ng = dn_norm_g[l][None]
        p = _inproj_call(x, g1, mod[l], lat_row, w_in_p[l], tm_lat)
        pc = _inproj_call(xc, g1, mod[l], ctx_row, w_in_p[l], tm_ctx)
        ocf, ocb, s_ctx = _gdn_call(pc, dn_conv_w[l], par[l], s_zero, None, True)
        olf, olb = _gdn_call(p, dn_conv_w[l], par[l], s_ctx, rope, False)
        y_c = _na_call(p, pc, _na_bias_table(na_rpb[l]))
        x = _merge_call(olf, olb, p, y_c, x, mod[l], lat_row, wpa, wpb, wpc, wo, ng, sc_conv_w[l], 256)
        fg = final_norm_g[None] if l == depth - 1 else None
        x = _ffn_call(x, g2, mod[l], lat_row, wup, ffn_conv_w[l], ffn_conv_b[l][None], wdn, tm_lat, fg)
        if need_ctx:
            yc_c = _ctx_attn_call(pc)
            xc = _merge_call(ocf, ocb, pc, yc_c, xc, mod[l], ctx_row, wpa, wpb, wpc, wo, ng, sc_conv_w[l], tm_ctx)
            xc = _ffn_call(xc, g2, mod[l], ctx_row, wup, ffn_conv_w[l], ffn_conv_b[l][None], wdn, tm_ctx)
    return x
```

```python
import functools

import numpy as np
import jax
import jax.numpy as jnp
from jax import lax
from jax.experimental import pallas as pl
from jax.experimental.pallas import tpu as pltpu

F32 = jnp.float32
BF16 = jnp.bfloat16

GRID_W = 64
DN_HEADS = 4
DN_HEAD_DIM = 128
DN_WIDTH = DN_HEADS * DN_HEAD_DIM
DN_CHUNK = 64
SC_WIDTH = 512
NA_HEADS = 8
NA_HEAD_DIM = 64
NA_WIDTH = NA_HEADS * NA_HEAD_DIM
NA_WIN_R = 8
NA_WIN_C = 16
ROPE_BASE = 10000.0
EPS = 1e-6
NEG = -1e30

LANE = 128
SUBLANE = 8
HALO = SUBLANE
VMEM_LIMIT = 48 * 1024 * 1024

P_COLS = 3 * DN_WIDTH + DN_WIDTH + 3 * SC_WIDTH + 3 * NA_WIDTH + 3 * 1024 + LANE
AB_BLK = (P_COLS - LANE) // LANE
NA_BLK0 = (3 * DN_WIDTH + DN_WIDTH + 3 * SC_WIDTH) // LANE


def _dot(a, b):
    return jnp.dot(a.astype(BF16), b.astype(BF16), preferred_element_type=F32)


def _dot_nt(a, b):
    return lax.dot_general(a.astype(BF16), b.astype(BF16), (((1,), (1,)), ((), ())), preferred_element_type=F32)


def _dot_tn(a, b):
    return lax.dot_general(a.astype(BF16), b.astype(BF16), (((0,), (0,)), ((), ())), preferred_element_type=F32)


def _sigmoid(x):
    return 1.0 / (1.0 + jnp.exp(-x))


def _silu(x):
    return x * _sigmoid(x)


def _softplus(x):
    return jnp.maximum(x, 0.0) + jnp.log1p(jnp.exp(-jnp.abs(x)))


def _rms(x, eps=EPS):
    return x * lax.rsqrt(jnp.mean(x * x, axis=-1, keepdims=True) + eps)


def _params(*sem):
    return pltpu.CompilerParams(dimension_semantics=sem, vmem_limit_bytes=VMEM_LIMIT)


def _dwconv3_rows(full, w_ref, n):
    rows = full.shape[0]
    dn = pltpu.roll(full, 1, axis=0)
    up = pltpu.roll(full, rows - 1, axis=0)
    out = dn * w_ref[0:1, :] + full * w_ref[1:2, :] + up * w_ref[2:3, :]
    return out[HALO:HALO + n]


def _ada_kernel(c_ref, w_ref, b_ref, o_ref):
    o_ref[...] = _dot(_silu(c_ref[...]), w_ref[...]) + b_ref[...]


def _ada_call(cc, w_ada, b_ada):
    depth, d, n6 = w_ada.shape
    r = cc.shape[0]
    tn = 1536
    return pl.pallas_call(
        _ada_kernel,
        grid=(depth, n6 // tn),
        in_specs=[
            pl.BlockSpec((r, d), lambda l, j: (0, 0)),
            pl.BlockSpec((None, d, tn), lambda l, j: (l, 0, j)),
            pl.BlockSpec((None, 1, tn), lambda l, j: (l, 0, j)),
        ],
        out_specs=pl.BlockSpec((None, r, tn), lambda l, j: (l, 0, j)),
        out_shape=jax.ShapeDtypeStruct((depth, r, n6), F32),
        compiler_params=_params("parallel", "parallel"),
        name="ada_mod",
    )(cc, w_ada, b_ada.reshape(depth, 1, n6))


def _inproj_kernel(x_ref, g_ref, sh_ref, sc_ref, w_ref, o_ref, h_ref):
    @pl.when(pl.program_id(2) == 0)
    def _():
        h = (_rms(x_ref[...]) * g_ref[...]) * (1.0 + sc_ref[...]) + sh_ref[...]
        h_ref[...] = h.astype(BF16)

    o_ref[...] = jnp.dot(h_ref[...], w_ref[...], preferred_element_type=F32)


def _inproj_call(x, g, mod, mrow, w, tm):
    b, t, d = x.shape
    n = w.shape[1]
    tn = 640
    return pl.pallas_call(
        _inproj_kernel,
        grid=(b, t // tm, n // tn),
        in_specs=[
            pl.BlockSpec((None, tm, d), lambda bi, i, j: (bi, i, 0)),
            pl.BlockSpec((1, d), lambda bi, i, j: (0, 0)),
            pl.BlockSpec((None, 1, d), lambda bi, i, j: (mrow(bi), 0, 0)),
            pl.BlockSpec((None, 1, d), lambda bi, i, j: (mrow(bi), 0, 1)),
            pl.BlockSpec((d, tn), lambda bi, i, j: (0, j)),
        ],
        out_specs=pl.BlockSpec((None, tm, tn), lambda bi, i, j: (bi, i, j)),
        out_shape=jax.ShapeDtypeStruct((b, t, n), F32),
        scratch_shapes=[pltpu.VMEM((tm, d), BF16)],
        compiler_params=_params("parallel", "parallel", "arbitrary"),
        name="inproj",
    )(x, g, mod, mod, w)


def _swap_pairs(x, even):
    n = x.shape[-1]
    return jnp.where(even, pltpu.roll(x, n - 1, axis=1), pltpu.roll(x, 1, axis=1))


def _gdn_kernel(*refs, nc, use_rope, emit_state):
    it = iter(refs)
    xs = [[next(it) for _ in range(4)] for _ in range(2)]
    cw_ref = next(it)
    par_ref = next(it)
    ropes = [[next(it), next(it)] for _ in range(2)] if use_rope else None
    s0_ref = next(it)
    o_refs = [next(it), next(it)]
    sout_ref = next(it) if emit_state else None
    s_scr = next(it)

    c = DN_CHUNK
    i = pl.program_id(1)

    @pl.when(i == 0)
    def _():
        s_scr[...] = s0_ref[...]

    ri = lax.broadcasted_iota(jnp.int32, (c, c), 0)
    ci = lax.broadcasted_iota(jnp.int32, (c, c), 1)
    eye = ri == ci
    lower = ri >= ci
    upper = ri <= ci
    even = (lax.broadcasted_iota(jnp.int32, (c, DN_HEAD_DIM), 1) % 2) == 0
    scale = DN_HEAD_DIM ** -0.5
    neg_a = -jnp.exp(par_ref[0:1, :])
    dt_b = par_ref[1:2, :]

    chains = []
    for d in range(2):
        x_ref, xp_ref, xn_ref, ab_ref = xs[d]
        chunk = i if d == 0 else nc - 1 - i
        has_prev = (chunk > 0).astype(F32)
        has_next = (chunk < nc - 1).astype(F32)
        full = jnp.concatenate([xp_ref[...] * has_prev, x_ref[...], xn_ref[...] * has_next], axis=0)
        act = _silu(_dwconv3_rows(full, cw_ref, c))
        ab = ab_ref[...]
        g_all = neg_a * _softplus(ab + dt_b)
        beta_all = _sigmoid(ab)
        incl = lower if d == 0 else upper
        incl_t = upper if d == 0 else lower
        strict = incl & (~eye)
        for h in range(DN_HEADS):
            sl = slice(h * DN_HEAD_DIM, (h + 1) * DN_HEAD_DIM)
            qh = act[:, sl]
            kh = act[:, DN_WIDTH + h * DN_HEAD_DIM:DN_WIDTH + (h + 1) * DN_HEAD_DIM]
            vh = act[:, 2 * DN_WIDTH + h * DN_HEAD_DIM:2 * DN_WIDTH + (h + 1) * DN_HEAD_DIM]
            qh = qh * lax.rsqrt(jnp.sum(qh * qh, axis=-1, keepdims=True) + EPS)
            kh = kh * lax.rsqrt(jnp.sum(kh * kh, axis=-1, keepdims=True) + EPS)
            if use_rope:
                cos = ropes[d][0][...]
                sin = ropes[d][1][...]
                qh = qh * cos + _swap_pairs(qh, even) * sin
                kh = kh * cos + _swap_pairs(kh, even) * sin
            qh = qh * scale
            col = d * DN_HEADS + h
            g_col = g_all[:, col:col + 1]
            beta = beta_all[:, 2 * DN_HEADS + col:2 * DN_HEADS + col + 1]
            g_cols = jnp.broadcast_to(g_col, (c, c))
            g_row = jnp.sum(jnp.where(eye, g_cols, 0.0), axis=0, keepdims=True)
            g_rows = jnp.broadcast_to(g_row, (c, c))
            gc_col = jnp.sum(jnp.where(incl, g_rows, 0.0), axis=1, keepdims=True)
            gc_row = jnp.sum(jnp.where(incl_t, g_cols, 0.0), axis=0, keepdims=True)
            g_tot = jnp.sum(g_col, axis=0, keepdims=True)
            decay = jnp.where(incl, jnp.exp(jnp.where(incl, gc_col - gc_row, 0.0)), 0.0)
            chains.append(dict(d=d, h=h, sl=sl, qh=qh, kh=kh, vh=vh, beta=beta, gc_col=gc_col, g_tot=g_tot,
                               decay=decay, strict=strict))
    for ch in chains:
        ch["amat"] = jnp.where(ch["strict"], ch["beta"] * _dot_nt(ch["kh"], ch["kh"]) * ch["decay"], 0.0)
        ch["tinv"] = jnp.where(eye, 1.0, 0.0)
    s = 1
    while s < c:
        same2 = (ri // (2 * s)) == (ci // (2 * s))
        diff1 = (ri // s) != (ci // s)
        for ch in chains:
            ch["x"] = _dot(ch["tinv"], jnp.where(same2 & diff1, ch["amat"], 0.0))
        for ch in chains:
            ch["tinv"] = ch["tinv"] - _dot(ch["x"], ch["tinv"])
        s *= 2
    for ch in chains:
        e_gc = jnp.exp(ch["gc_col"])
        ch["u"] = _dot(ch["tinv"], ch["vh"] * ch["beta"])
        ch["w"] = _dot(ch["tinv"], ch["kh"] * (ch["beta"] * e_gc))
        ch["qk"] = _dot_nt(ch["qh"], ch["kh"]) * ch["decay"]
        ch["q_dec"] = ch["qh"] * e_gc
        ch["k_dec"] = ch["kh"] * jnp.exp(ch["g_tot"] - ch["gc_col"])
        ch["state"] = s_scr[ch["d"], ch["h"]]
    for ch in chains:
        ch["v_new"] = ch["u"] - _dot(ch["w"], ch["state"])
        ch["qs"] = _dot(ch["q_dec"], ch["state"])
    for ch in chains:
        o_refs[ch["d"]][:, ch["sl"]] = ch["qs"] + _dot(ch["qk"], ch["v_new"])
        s_scr[ch["d"], ch["h"]] = ch["state"] * jnp.exp(ch["g_tot"]) + _dot_tn(ch["k_dec"], ch["v_new"])

    if emit_state:
        @pl.when(i == nc - 1)
        def _():
            sout_ref[...] = s_scr[...]


def _gdn_call(p, conv_w, par, s0, rope, emit_state):
    b, t, _ = p.shape
    c = DN_CHUNK
    nc = t // c
    hb = c // HALO
    nhb = t // HALO
    qkv_w = 3 * DN_WIDTH

    def chunk_of(d, i):
        return i if d == 0 else nc - 1 - i

    in_specs = []
    args = []
    for d in range(2):
        in_specs += [
            pl.BlockSpec((None, c, qkv_w), lambda bi, i, d=d: (bi, chunk_of(d, i), 0)),
            pl.BlockSpec((None, HALO, qkv_w), lambda bi, i, d=d: (bi, jnp.maximum(chunk_of(d, i) * hb - 1, 0), 0)),
            pl.BlockSpec((None, HALO, qkv_w), lambda bi, i, d=d: (bi, jnp.minimum(chunk_of(d, i) * hb + hb, nhb - 1), 0)),
            pl.BlockSpec((None, c, LANE), lambda bi, i, d=d: (bi, chunk_of(d, i), AB_BLK)),
        ]
        args += [p, p, p, p]
    in_specs += [pl.BlockSpec(conv_w.shape, lambda bi, i: (0, 0)), pl.BlockSpec(par.shape, lambda bi, i: (0, 0))]
    args += [conv_w, par]
    if rope is not None:
        for d in range(2):
            in_specs += [pl.BlockSpec((c, DN_HEAD_DIM), lambda bi, i, d=d: (chunk_of(d, i), 0))] * 2
            args += [rope[0], rope[1]]
    sshape = (2, DN_HEADS, DN_HEAD_DIM, DN_HEAD_DIM)
    in_specs += [pl.BlockSpec((None,) + sshape, lambda bi, i: (bi, 0, 0, 0, 0))]
    args += [s0]
    out_specs = [
        pl.BlockSpec((None, c, DN_WIDTH), lambda bi, i: (bi, i, 0)),
        pl.BlockSpec((None, c, DN_WIDTH), lambda bi, i: (bi, nc - 1 - i, 0)),
    ]
    out_shape = [jax.ShapeDtypeStruct((b, t, DN_WIDTH), F32)] * 2
    if emit_state:
        out_specs += [pl.BlockSpec((None,) + sshape, lambda bi, i: (bi, 0, 0, 0, 0))]
        out_shape += [jax.ShapeDtypeStruct((b,) + sshape, F32)]
    return pl.pallas_call(
        functools.partial(_gdn_kernel, nc=nc, use_rope=rope is not None, emit_state=emit_state),
        grid=(b, nc),
        in_specs=in_specs,
        out_specs=out_specs,
        out_shape=out_shape,
        scratch_shapes=[pltpu.VMEM(sshape, F32)],
        compiler_params=_params("parallel", "arbitrary"),
        name="gdn_ctx" if emit_state else "gdn_lat",
    )(*args)


def _na_kernel(q_ref, k_ref, v_ref, kc_ref, vc_ref, bias_ref, o_ref, *, rows):
    w = GRID_W
    nl = NA_WIN_R * w
    scale = NA_HEAD_DIM ** -0.5
    head0 = lax.broadcasted_iota(jnp.int32, (w, LANE), 1) < NA_HEAD_DIM
    kc = kc_ref[...]
    vc = vc_ref[...]

    def row_block(r, carry):
        rs = jnp.clip(r - NA_WIN_R // 2, 0, rows - NA_WIN_R)
        d0 = rs - r + NA_WIN_R - 1
        q2 = q_ref[pl.ds(pl.multiple_of(r * w, w), w), :] * scale
        kw = k_ref[pl.ds(pl.multiple_of(rs * w, w), nl), :]
        vw = v_ref[pl.ds(pl.multiple_of(rs * w, w), nl), :]
        out = None
        for hh in range(2):
            sel = head0 if hh == 0 else ~head0
            qm = jnp.where(sel, q2, 0.0)
            s_loc = _dot_nt(qm, kw) + bias_ref[hh, d0]
            s_ctx = _dot_nt(qm, kc)
            m = jnp.maximum(jnp.max(s_loc, axis=-1, keepdims=True), jnp.max(s_ctx, axis=-1, keepdims=True))
            p_loc = jnp.exp(s_loc - m)
            p_ctx = jnp.exp(s_ctx - m)
            den = jnp.sum(p_loc, axis=-1, keepdims=True) + jnp.sum(p_ctx, axis=-1, keepdims=True)
            o = (_dot(p_loc, vw) + _dot(p_ctx, vc)) / den
            out = o if out is None else jnp.where(sel, o, out)
        o_ref[pl.ds(pl.multiple_of(r * w, w), w), :] = out
        return carry

    lax.fori_loop(0, rows, row_block, 0)


def _na_call(p, pc, bias):
    b, t, _ = p.shape
    l = pc.shape[1]
    rows = t // GRID_W
    npair = NA_HEADS // 2
    return pl.pallas_call(
        functools.partial(_na_kernel, rows=rows),
        grid=(npair, b),
        in_specs=[
            pl.BlockSpec((None, t, LANE), lambda j, bi: (bi, 0, NA_BLK0 + j)),
            pl.BlockSpec((None, t, LANE), lambda j, bi: (bi, 0, NA_BLK0 + npair + j)),
            pl.BlockSpec((None, t, LANE), lambda j, bi: (bi, 0, NA_BLK0 + 2 * npair + j)),
            pl.BlockSpec((None, l, LANE), lambda j, bi: (bi, 0, NA_BLK0 + npair + j)),
            pl.BlockSpec((None, l, LANE), lambda j, bi: (bi, 0, NA_BLK0 + 2 * npair + j)),
            pl.BlockSpec((None,) + bias.shape[1:], lambda j, bi: (j, 0, 0, 0, 0)),
        ],
        out_specs=pl.BlockSpec((None, t, LANE), lambda j, bi: (bi, 0, j)),
        out_shape=jax.ShapeDtypeStruct((b, t, NA_WIDTH), F32),
        compiler_params=_params("parallel", "parallel"),
        name="na_lat",
    )(p, p, p, pc, pc, bias)


def _ctx_attn_kernel(q_ref, k_ref, v_ref, o_ref):
    scale = NA_HEAD_DIM ** -0.5
    q2 = q_ref[...] * scale
    k2 = k_ref[...]
    v2 = v_ref[...]
    head0 = lax.broadcasted_iota(jnp.int32, q2.shape, 1) < NA_HEAD_DIM
    out = None
    for hh in range(2):
        sel = head0 if hh == 0 else ~head0
        s = _dot_nt(jnp.where(sel, q2, 0.0), k2)
        p = jnp.exp(s - jnp.max(s, axis=-1, keepdims=True))
        o = _dot(p, v2) / jnp.sum(p, axis=-1, keepdims=True)
        out = o if out is None else jnp.where(sel, o, out)
    o_ref[...] = out


def _ctx_attn_call(pc):
    b, l, _ = pc.shape
    npair = NA_HEADS // 2
    return pl.pallas_call(
        _ctx_attn_kernel,
        grid=(b, npair),
        in_specs=[
            pl.BlockSpec((None, l, LANE), lambda bi, j: (bi, 0, NA_BLK0 + j)),
            pl.BlockSpec((None, l, LANE), lambda bi, j: (bi, 0, NA_BLK0 + npair + j)),
            pl.BlockSpec((None, l, LANE), lambda bi, j: (bi, 0, NA_BLK0 + 2 * npair + j)),
        ],
        out_specs=pl.BlockSpec((None, l, LANE), lambda bi, j: (bi, 0, j)),
        out_shape=jax.ShapeDtypeStruct((b, l, NA_WIDTH), F32),
        compiler_params=_params("parallel", "parallel"),
        name="ctx_attn",
    )(pc, pc, pc)


def _merge_kernel(of_ref, ob_ref, z_ref, bg_ref, cg_ref, hh_ref, cgp_ref, cgn_ref, hhp_ref, hhn_ref, yc_ref,
                  ga_ref, gb_ref, gcg_ref, x_ref, gt_ref, wpa_ref, wpb_ref, wpc_ref, wo_ref, ng_ref, scw_ref,
                  o_ref, *, nt):
    i = pl.program_id(1)
    tm = x_ref.shape[0]
    o = of_ref[...] + ob_ref[...]
    z = z_ref[...]
    ya = []
    for h in range(DN_HEADS):
        sl = slice(h * DN_HEAD_DIM, (h + 1) * DN_HEAD_DIM)
        ya.append(_rms(o[:, sl]) * ng_ref[...] * _silu(z[:, sl]))
    y_a = jnp.concatenate(ya, axis=1)
    has_prev = (i > 0).astype(F32)
    has_next = (i < nt - 1).astype(F32)
    full = jnp.concatenate([cgp_ref[...] * hhp_ref[...] * has_prev, cg_ref[...] * hh_ref[...],
                            cgn_ref[...] * hhn_ref[...] * has_next], axis=0)
    y_b = bg_ref[...] * _dwconv3_rows(full, scw_ref, tm)
    y = (_sigmoid(ga_ref[...]) * _dot(y_a, wpa_ref[...]) + _sigmoid(gb_ref[...]) * _dot(y_b, wpb_ref[...])
         + _sigmoid(gcg_ref[...]) * _dot(yc_ref[...], wpc_ref[...]))
    o_ref[...] = x_ref[...] + gt_ref[...] * _dot(y, wo_ref[...])


def _merge_call(of, ob, p, yc, x, mod, mrow, wpa, wpb, wpc, wo, ng, scw, tm):
    b, t, d = x.shape
    nt = t // tm
    hb = tm // HALO
    nhb = t // HALO
    w5 = SC_WIDTH

    def tile(width, col):
        return pl.BlockSpec((None, tm, width), lambda bi, i: (bi, i, col))

    def prev(col):
        return pl.BlockSpec((None, HALO, w5), lambda bi, i: (bi, jnp.maximum(i * hb - 1, 0), col))

    def nxt(col):
        return pl.BlockSpec((None, HALO, w5), lambda bi, i: (bi, jnp.minimum(i * hb + hb, nhb - 1), col))

    def full(a):
        return pl.BlockSpec(a.shape, lambda bi, i: (0,) * a.ndim)

    in_specs = [
        tile(DN_WIDTH, 0), tile(DN_WIDTH, 0),
        tile(w5, 3), tile(w5, 4), tile(w5, 5), tile(w5, 6),
        prev(5), nxt(5), prev(6), nxt(6),
        tile(NA_WIDTH, 0),
        tile(d, 5), tile(d, 6), tile(d, 7),
        tile(d, 0),
        pl.BlockSpec((None, 1, d), lambda bi, i: (mrow(bi), 0, 2)),
        full(wpa), full(wpb), full(wpc), full(wo), full(ng), full(scw),
    ]
    return pl.pallas_call(
        functools.partial(_merge_kernel, nt=nt),
        grid=(b, nt),
        in_specs=in_specs,
        out_specs=pl.BlockSpec((None, tm, d), lambda bi, i: (bi, i, 0)),
        out_shape=jax.ShapeDtypeStruct((b, t, d), F32),
        compiler_params=_params("parallel", "parallel"),
        name="merge",
    )(of, ob, p, p, p, p, p, p, p, p, yc, p, p, p, x, mod, wpa, wpb, wpc, wo, ng, scw)


def _ffn_kernel(*refs, nt, nk, final):
    it = iter(refs)
    x_ref, xp_ref, xn_ref, g_ref, sh_ref, sc_ref, gt_ref = (next(it) for _ in range(7))
    wa_ref, wb_ref, cwa_ref, cwb_ref, cba_ref, cbb_ref, wd_ref = (next(it) for _ in range(7))
    fg_ref = next(it) if final else None
    o_ref, h_scr, acc_scr = next(it), next(it), next(it)
    i = pl.program_id(1)
    k = pl.program_id(2)
    tm = x_ref.shape[0]

    @pl.when(k == 0)
    def _():
        xfull = jnp.concatenate([xp_ref[...], x_ref[...], xn_ref[...]], axis=0)
        h = (_rms(xfull) * g_ref[...]) * (1.0 + sc_ref[...]) + sh_ref[...]
        h_scr[...] = h.astype(BF16)
        acc_scr[...] = jnp.zeros_like(acc_scr)

    row = lax.broadcasted_iota(jnp.int32, (tm + 2 * HALO, 1), 0)
    keep = ((row >= HALO) | (i > 0)) & ((row < tm + HALO) | (i < nt - 1))
    hfull = h_scr[...]
    ua = jnp.where(keep, jnp.dot(hfull, wa_ref[...], preferred_element_type=F32), 0.0)
    ub = jnp.where(keep, jnp.dot(hfull, wb_ref[...], preferred_element_type=F32), 0.0)
    a = _dwconv3_rows(ua, cwa_ref, tm) + cba_ref[...]
    bb = _dwconv3_rows(ub, cwb_ref, tm) + cbb_ref[...]
    acc_scr[...] += _dot(_silu(a) * bb, wd_ref[...])

    @pl.when(k == nk - 1)
    def _():
        out = x_ref[...] + gt_ref[...] * acc_scr[...]
        if final:
            out = _rms(out) * fg_ref[...]
        o_ref[...] = out


def _ffn_call(x, g, mod, mrow, w_up, cw, cb, w_down, tm, final_g=None):
    b, t, d = x.shape
    dff = w_down.shape[0]
    fc = 256
    nk = dff // fc
    nt = t // tm
    hb = tm // HALO
    nhb = t // HALO
    final = final_g is not None
    in_specs = [
        pl.BlockSpec((None, tm, d), lambda bi, i, k: (bi, i, 0)),
        pl.BlockSpec((None, HALO, d), lambda bi, i, k: (bi, jnp.maximum(i * hb - 1, 0), 0)),
        pl.BlockSpec((None, HALO, d), lambda bi, i, k: (bi, jnp.minimum(i * hb + hb, nhb - 1), 0)),
        pl.BlockSpec((1, d), lambda bi, i, k: (0, 0)),
        pl.BlockSpec((None, 1, d), lambda bi, i, k: (mrow(bi), 0, 3)),
        pl.BlockSpec((None, 1, d), lambda bi, i, k: (mrow(bi), 0, 4)),
        pl.BlockSpec((None, 1, d), lambda bi, i, k: (mrow(bi), 0, 5)),
        pl.BlockSpec((d, fc), lambda bi, i, k: (0, k)),
        pl.BlockSpec((d, fc), lambda bi, i, k: (0, nk + k)),
        pl.BlockSpec((3, fc), lambda bi, i, k: (0, k)),
        pl.BlockSpec((3, fc), lambda bi, i, k: (0, nk + k)),
        pl.BlockSpec((1, fc), lambda bi, i, k: (0, k)),
        pl.BlockSpec((1, fc), lambda bi, i, k: (0, nk + k)),
        pl.BlockSpec((fc, d), lambda bi, i, k: (k, 0)),
    ]
    args = [x, x, x, g, mod, mod, mod, w_up, w_up, cw, cw, cb, cb, w_down]
    if final:
        in_specs.append(pl.BlockSpec((1, d), lambda bi, i, k: (0, 0)))
        args.append(final_g)
    return pl.pallas_call(
        functools.partial(_ffn_kernel, nt=nt, nk=nk, final=final),
        grid=(b, nt, nk),
        in_specs=in_specs,
        out_specs=pl.BlockSpec((None, tm, d), lambda bi, i, k: (bi, i, 0)),
        out_shape=jax.ShapeDtypeStruct((b, t, d), F32),
        scratch_shapes=[pltpu.VMEM((tm + 2 * HALO, d), BF16), pltpu.VMEM((tm, d), F32)],
        compiler_params=_params("parallel", "parallel", "arbitrary"),
        name="ffn_final" if final else "ffn",
    )(*args)


def _rope_tables(t):
    tok = jnp.arange(t, dtype=jnp.int32)
    rows = (tok // GRID_W).astype(F32)
    cols = (tok % GRID_W).astype(F32)
    nf = DN_HEAD_DIM // 4
    inv = ROPE_BASE ** (-jnp.arange(nf, dtype=F32) / nf)
    ang = jnp.concatenate([rows[:, None] * inv, cols[:, None] * inv], axis=-1)
    cos = jnp.repeat(jnp.cos(ang), 2, axis=-1)
    sin = jnp.repeat(jnp.sin(ang), 2, axis=-1)
    sign = jnp.where(jnp.arange(DN_HEAD_DIM) % 2 == 0, -1.0, 1.0).astype(F32)
    return cos, sin * sign


def _na_bias_table(rpb):
    w = GRID_W
    qc = np.arange(w)[:, None]
    kc = np.arange(w)[None, :]
    cstart = np.clip(qc - NA_WIN_C // 2, 0, w - NA_WIN_C)
    valid = (kc >= cstart) & (kc < cstart + NA_WIN_C)
    cidx = np.clip(kc - qc + NA_WIN_C - 1, 0, 2 * NA_WIN_C - 2)
    tbl = jnp.where(jnp.asarray(valid), rpb[:, :, cidx], NEG)
    didx = np.arange(NA_WIN_R)[:, None] + np.arange(NA_WIN_R)[None, :]
    tbl = tbl[:, didx]
    tbl = tbl.transpose(0, 1, 3, 2, 4).reshape(NA_HEADS // 2, 2, NA_WIN_R, w, NA_WIN_R * w)
    return tbl


def _permute_w_in(w_in):
    s = np.cumsum([0, 3 * DN_WIDTH, DN_WIDTH, 2 * DN_HEADS, 2 * DN_HEADS, 3 * SC_WIDTH, 3 * NA_WIDTH, 3 * 1024])
    qkv, z, a, b, sc, na, gates = (w_in[..., s[j]:s[j + 1]] for j in range(7))
    pad = jnp.zeros(w_in.shape[:-1] + (LANE - 4 * DN_HEADS,), w_in.dtype)
    return jnp.concatenate([qkv, z, sc, na, gates, a, b, pad], axis=-1)


def kernel(x, c, ctx, c_ctx, norm1_g, norm2_g, w_ada, b_ada, w_in, dn_conv_w, dn_a_log, dn_dt_bias, dn_norm_g, sc_conv_w, na_rpb, w_pa, w_pb, w_pc, w_o, w_up, ffn_conv_w, ffn_conv_b, w_down, final_norm_g):
    bsz, t, d = x.shape
    depth = w_in.shape[0]
    assert d == 1024 and t % (NA_WIN_R * GRID_W) == 0 and ctx.shape[1] % DN_CHUNK == 0

    mod_rows = -(-(bsz + 1) // SUBLANE) * SUBLANE
    cc = jnp.zeros((mod_rows, d), F32).at[:bsz].set(c).at[bsz].set(c_ctx)
    mod = _ada_call(cc, w_ada, b_ada).reshape(depth, mod_rows, 1, 6 * d)
    lat_row = lambda bi: bi
    ctx_row = lambda bi: bsz

    w_in_p = _permute_w_in(w_in).astype(BF16)
    rope = _rope_tables(t)
    par = jnp.zeros((depth, SUBLANE, LANE), F32)
    par = par.at[:, 0, :2 * DN_HEADS].set(dn_a_log.reshape(depth, -1)).at[:, 1, :2 * DN_HEADS].set(dn_dt_bias.reshape(depth, -1))
    s_zero = jnp.zeros((bsz, 2, DN_HEADS, DN_HEAD_DIM, DN_HEAD_DIM), F32)
    tm_lat = 1024 if t % 1024 == 0 else 512
    tm_ctx = ctx.shape[1]

    xc = ctx
    for l in range(depth):
        need_ctx = l < depth - 1
        g1 = norm1_g[l][None]
        g2 = norm2_g[l][None]
        wpa, wpb, wpc, wo = (a[l].astype(BF16) for a in (w_pa, w_pb, w_pc, w_o))
        wup, wdn = w_up[l].astype(BF16), w_down[l].astype(BF16)
        ng = dn_norm_g[l][None]
        p = _inproj_call(x, g1, mod[l], lat_row, w_in_p[l], tm_lat)
        pc = _inproj_call(xc, g1, mod[l], ctx_row, w_in_p[l], tm_ctx)
        ocf, ocb, s_ctx = _gdn_call(pc, dn_conv_w[l], par[l], s_zero, None, True)
        olf, olb = _gdn_call(p, dn_conv_w[l], par[l], s_ctx, rope, False)
        y_c = _na_call(p, pc, _na_bias_table(na_rpb[l]))
        x = _merge_call(olf, olb, p, y_c, x, mod[l], lat_row, wpa, wpb, wpc, wo, ng, sc_conv_w[l], 256)
        fg = final_norm_g[None] if l == depth - 1 else None
        x = _ffn_call(x, g2, mod[l], lat_row, wup, ffn_conv_w[l], ffn_conv_b[l][None], wdn, tm_lat, fg)
        if need_ctx:
            yc_c = _ctx_attn_call(pc)
            xc = _merge_call(ocf, ocb, pc, yc_c, xc, mod[l], ctx_row, wpa, wpb, wpc, wo, ng, sc_conv_w[l], tm_ctx)
            xc = _ffn_call(xc, g2, mod[l], ctx_row, wup, ffn_conv_w[l], ffn_conv_b[l][None], wdn, tm_ctx)
    return x
```

```python
import functools

import numpy as np
import jax
import jax.numpy as jnp
from jax import lax
from jax.experimental import pallas as pl
from jax.experimental.pallas import tpu as pltpu

F32 = jnp.float32
BF16 = jnp.bfloat16

GRID_W = 64
DN_HEADS = 4
DN_HEAD_DIM = 128
DN_WIDTH = DN_HEADS * DN_HEAD_DIM
DN_CHUNK = 64
GDN_TILE_CHUNKS = 4
SC_WIDTH = 512
NA_HEADS = 8
NA_HEAD_DIM = 64
NA_WIDTH = NA_HEADS * NA_HEAD_DIM
NA_WIN_R = 8
NA_WIN_C = 16
NA_ROWS_PER_ITER = 4
ROPE_BASE = 10000.0
EPS = 1e-6
NEG = -1e30

LANE = 128
SUBLANE = 8
HALO = SUBLANE
VMEM_LIMIT = 48 * 1024 * 1024

P_COLS = 3 * DN_WIDTH + DN_WIDTH + 3 * SC_WIDTH + 3 * NA_WIDTH + 3 * 1024 + LANE
AB_BLK = (P_COLS - LANE) // LANE
NA_BLK0 = (3 * DN_WIDTH + DN_WIDTH + 3 * SC_WIDTH) // LANE


def _dot(a, b):
    return jnp.dot(a.astype(BF16), b.astype(BF16), preferred_element_type=F32)


def _dot_nt(a, b):
    return lax.dot_general(a.astype(BF16), b.astype(BF16), (((1,), (1,)), ((), ())), preferred_element_type=F32)


def _dot_tn(a, b):
    return lax.dot_general(a.astype(BF16), b.astype(BF16), (((0,), (0,)), ((), ())), preferred_element_type=F32)


def _sigmoid(x):
    return 1.0 / (1.0 + jnp.exp(-x))


def _silu(x):
    return x * _sigmoid(x)


def _softplus(x):
    return jnp.maximum(x, 0.0) + jnp.log1p(jnp.exp(-jnp.abs(x)))


def _rms(x, eps=EPS):
    return x * lax.rsqrt(jnp.mean(x * x, axis=-1, keepdims=True) + eps)


def _params(*sem):
    return pltpu.CompilerParams(dimension_semantics=sem, vmem_limit_bytes=VMEM_LIMIT)


def _dwconv3_rows(full, w_ref, n):
    rows = full.shape[0]
    dn = pltpu.roll(full, 1, axis=0)
    up = pltpu.roll(full, rows - 1, axis=0)
    out = dn * w_ref[0:1, :] + full * w_ref[1:2, :] + up * w_ref[2:3, :]
    return out[HALO:HALO + n]


def _ada_kernel(c_ref, w_ref, b_ref, o_ref):
    o_ref[...] = _dot(_silu(c_ref[...]), w_ref[...]) + b_ref[...]


def _ada_call(cc, w_ada, b_ada):
    depth, d, n6 = w_ada.shape
    r = cc.shape[0]
    tn = 1536
    return pl.pallas_call(
        _ada_kernel,
        grid=(depth, n6 // tn),
        in_specs=[
            pl.BlockSpec((r, d), lambda l, j: (0, 0)),
            pl.BlockSpec((None, d, tn), lambda l, j: (l, 0, j)),
            pl.BlockSpec((None, 1, tn), lambda l, j: (l, 0, j)),
        ],
        out_specs=pl.BlockSpec((None, r, tn), lambda l, j: (l, 0, j)),
        out_shape=jax.ShapeDtypeStruct((depth, r, n6), F32),
        compiler_params=_params("parallel", "parallel"),
        name="ada_mod",
    )(cc, w_ada, b_ada.reshape(depth, 1, n6))


def _inproj_kernel(x_ref, g_ref, sh_ref, sc_ref, w_ref, o_ref, h_ref):
    @pl.when(pl.program_id(2) == 0)
    def _():
        h = (_rms(x_ref[...]) * g_ref[...]) * (1.0 + sc_ref[...]) + sh_ref[...]
        h_ref[...] = h.astype(BF16)

    o_ref[...] = jnp.dot(h_ref[...], w_ref[...], preferred_element_type=F32)


def _inproj_call(x, g, mod, mrow, w, tm):
    b, t, d = x.shape
    n = w.shape[1]
    tn = 640
    return pl.pallas_call(
        _inproj_kernel,
        grid=(b, t // tm, n // tn),
        in_specs=[
            pl.BlockSpec((None, tm, d), lambda bi, i, j: (bi, i, 0)),
            pl.BlockSpec((1, d), lambda bi, i, j: (0, 0)),
            pl.BlockSpec((None, 1, d), lambda bi, i, j: (mrow(bi), 0, 0)),
            pl.BlockSpec((None, 1, d), lambda bi, i, j: (mrow(bi), 0, 1)),
            pl.BlockSpec((d, tn), lambda bi, i, j: (0, j)),
        ],
        out_specs=pl.BlockSpec((None, tm, tn), lambda bi, i, j: (bi, i, j)),
        out_shape=jax.ShapeDtypeStruct((b, t, n), F32),
        scratch_shapes=[pltpu.VMEM((tm, d), BF16)],
        compiler_params=_params("parallel", "parallel", "arbitrary"),
        name="inproj",
    )(x, g, mod, mod, w)


def _swap_pairs(x, even):
    n = x.shape[-1]
    return jnp.where(even, pltpu.roll(x, n - 1, axis=1), pltpu.roll(x, 1, axis=1))


def _gdn_kernel(*refs, nt, use_rope, emit_state):
    it = iter(refs)
    xs = [[next(it) for _ in range(4)] for _ in range(2)]
    cw_ref = next(it)
    par_ref = next(it)
    ropes = [[next(it), next(it)] for _ in range(2)] if use_rope else None
    s0_ref = next(it)
    o_refs = [next(it), next(it)]
    sout_ref = next(it) if emit_state else None
    s_scr, q_scr, k_scr, v_scr, uw_scr, qkd_scr = (next(it) for _ in range(6))

    c = DN_CHUNK
    nh = DN_HEADS
    hd = DN_HEAD_DIM
    nch = GDN_TILE_CHUNKS
    tr = nch * c
    cw = nh * c
    lc = c.bit_length() - 1
    i = pl.program_id(1)

    @pl.when(i == 0)
    def _():
        s_scr[...] = s0_ref[...]

    ri = lax.broadcasted_iota(jnp.int32, (c, cw), 0)
    li = lax.broadcasted_iota(jnp.int32, (c, cw), 1)
    jj = li & (c - 1)
    seg = li >> lc
    eye_cat = ri == jj
    bd_sq = jnp.where((lax.broadcasted_iota(jnp.int32, (cw, cw), 0) >> lc)
                      == (lax.broadcasted_iota(jnp.int32, (cw, cw), 1) >> lc), 1.0, 0.0).astype(BF16)
    bd_k = jnp.where((lax.broadcasted_iota(jnp.int32, (cw, nh * hd), 0) >> lc)
                     == (lax.broadcasted_iota(jnp.int32, (cw, nh * hd), 1) // hd), 1.0, 0.0).astype(BF16)
    even = (lax.broadcasted_iota(jnp.int32, (tr, hd), 1) % 2) == 0
    rowi = lax.broadcasted_iota(jnp.int32, (tr, LANE), 0) & (c - 1)
    scale = hd ** -0.5
    neg_a = -jnp.exp(par_ref[0:1, :])
    dt_b = par_ref[1:2, :]

    def col_bcast(x, j, width):
        return jnp.broadcast_to(x[:, j:j + 1], (x.shape[0], width))

    def colcat(x, base):
        out = col_bcast(x, base + nh - 1, cw)
        for h in range(nh - 2, -1, -1):
            out = jnp.where(seg == h, col_bcast(x, base + h, cw), out)
        return out

    def block_diag(m, mask):
        return jnp.concatenate([m.astype(BF16)] * nh, axis=0) * mask

    nar = []
    for d in range(2):
        x_ref, xp_ref, xn_ref, ab_ref = xs[d]
        tile = i if d == 0 else nt - 1 - i
        has_prev = (tile > 0).astype(F32)
        has_next = (tile < nt - 1).astype(F32)
        full = jnp.concatenate([xp_ref[...] * has_prev, x_ref[...], xn_ref[...] * has_next], axis=0)
        act = _silu(_dwconv3_rows(full, cw_ref, tr))
        for h in range(nh):
            sl = slice(h * hd, (h + 1) * hd)
            qh = act[:, h * hd:(h + 1) * hd]
            kh = act[:, DN_WIDTH + h * hd:DN_WIDTH + (h + 1) * hd]
            qh = qh * lax.rsqrt(jnp.sum(qh * qh, axis=-1, keepdims=True) + EPS)
            kh = kh * lax.rsqrt(jnp.sum(kh * kh, axis=-1, keepdims=True) + EPS)
            if use_rope:
                cos = ropes[d][0][...]
                sin = ropes[d][1][...]
                qh = qh * cos + _swap_pairs(qh, even) * sin
                kh = kh * cos + _swap_pairs(kh, even) * sin
            q_scr[d, :, sl] = qh * scale
            k_scr[d, :, sl] = kh
        v_scr[d] = act[:, 2 * DN_WIDTH:]
        ab = ab_ref[...]
        gc = neg_a * _softplus(ab + dt_b)
        beta = pltpu.roll(_sigmoid(ab), LANE - 2 * nh, axis=1)
        sft = 1
        while sft < c:
            if d == 0:
                gc = gc + jnp.where(rowi >= sft, pltpu.roll(gc, sft, axis=0), 0.0)
            else:
                gc = gc + jnp.where(rowi < c - sft, pltpu.roll(gc, tr - sft, axis=0), 0.0)
            sft *= 2
        e1 = jnp.exp(gc)
        nar.append(dict(gc=gc, e1=e1, beta=beta, be=beta * e1))

    probs = [(d, ch) for ch in range(nch) for d in range(2)]
    st = {}
    for d, ch in probs:
        rs = slice(ch * c, (ch + 1) * c)
        base = d * nh
        incl = (ri >= jj) if d == 0 else (ri <= jj)
        gc_c = nar[d]["gc"][rs]
        gcc = colcat(gc_c, base)
        gcr = jnp.broadcast_to(jnp.sum(jnp.where(eye_cat, gcc, 0.0), axis=0, keepdims=True), (c, cw))
        decay = jnp.where(incl, jnp.exp(jnp.where(incl, gcc - gcr, 0.0)), 0.0)
        kf = k_scr[d, rs, :]
        prod = _dot_nt(jnp.concatenate([q_scr[d, rs, :], kf], axis=0), block_diag(kf, bd_k))
        qkd_scr[d, ch] = prod[:c] * decay
        amat = jnp.where(incl & (~eye_cat), colcat(nar[d]["beta"][rs], base) * prod[c:] * decay, 0.0)
        lvl1 = ((ri >> 1) == (jj >> 1))
        st[d, ch] = dict(amat=amat.astype(BF16), tinv=jnp.where(eye_cat, 1.0, 0.0) - jnp.where(lvl1, amat, 0.0))
    s = 2
    while s < c:
        ls = s.bit_length() - 1
        lvl = jnp.where(((ri >> (ls + 1)) == (jj >> (ls + 1))) & ((ri >> ls) != (jj >> ls)), 1.0, 0.0).astype(BF16)
        for key in probs:
            p = st[key]
            p["t_bf"] = p["tinv"].astype(BF16)
            p["x"] = _dot(p["t_bf"], block_diag(p["amat"] * lvl, bd_sq))
        for key in probs:
            p = st[key]
            p["tinv"] = p["tinv"] - _dot(p["x"], block_diag(p["t_bf"], bd_sq))
        s *= 2
    for d, ch in probs:
        rs = slice(ch * c, (ch + 1) * c)
        base = d * nh
        rhs = jnp.concatenate([
            jnp.concatenate([v_scr[d, rs, h * hd:(h + 1) * hd] * col_bcast(nar[d]["beta"][rs], base + h, hd),
                             k_scr[d, rs, h * hd:(h + 1) * hd] * col_bcast(nar[d]["be"][rs], base + h, hd)], axis=1)
            for h in range(nh)], axis=0)
        uw_scr[d, ch] = _dot(block_diag(st[d, ch]["tinv"], bd_sq), rhs)
    states = [[s_scr[d, h] for h in range(nh)] for d in range(2)]
    for step in range(nch):
        cur = [(0, step), (1, nch - 1 - step)]
        loc = {}
        for d, ch in cur:
            rs = slice(ch * c, (ch + 1) * c)
            base = d * nh
            gc_c = nar[d]["gc"][rs]
            gtot = gc_c[c - 1:c] if d == 0 else gc_c[0:1]
            e2 = jnp.exp(gtot - gc_c)
            r_ = []
            for h in range(nh):
                hs = slice(h * hd, (h + 1) * hd)
                w_h = uw_scr[d, ch, h * c:(h + 1) * c, hd:]
                qd_h = q_scr[d, rs, hs] * col_bcast(nar[d]["e1"][rs], base + h, hd)
                r_.append(_dot(jnp.concatenate([w_h, qd_h], axis=0), states[d][h]))
            loc[d] = dict(r=r_, e2=e2, gtot=gtot)
        for d, ch in cur:
            loc[d]["vn"] = [uw_scr[d, ch, h * c:(h + 1) * c, :hd] - loc[d]["r"][h][:c] for h in range(nh)]
        for d, ch in cur:
            rs = slice(ch * c, (ch + 1) * c)
            base = d * nh
            vn = loc[d]["vn"]
            o2 = _dot(block_diag(qkd_scr[d, ch], bd_sq), jnp.concatenate(vn, axis=0))
            for h in range(nh):
                hs = slice(h * hd, (h + 1) * hd)
                kd_h = k_scr[d, rs, hs] * col_bcast(loc[d]["e2"], base + h, hd)
                o_refs[d][rs, hs] = loc[d]["r"][h][c:] + o2[h * c:(h + 1) * c]
                eg = jnp.exp(jnp.broadcast_to(loc[d]["gtot"][0:1, base + h:base + h + 1], (hd, hd)))
                states[d][h] = states[d][h] * eg + _dot_tn(kd_h, vn[h])
    for d in range(2):
        for h in range(nh):
            s_scr[d, h] = states[d][h]

    if emit_state:
        @pl.when(i == nt - 1)
        def _():
            sout_ref[...] = s_scr[...]


def _gdn_call(p, conv_w, par, s0, rope, emit_state):
    b, t, _ = p.shape
    c = DN_CHUNK
    tr = GDN_TILE_CHUNKS * c
    nt = t // tr
    hb = tr // HALO
    nhb = t // HALO
    qkv_w = 3 * DN_WIDTH

    def tile_of(d, i):
        return i if d == 0 else nt - 1 - i

    in_specs = []
    args = []
    for d in range(2):
        in_specs += [
            pl.BlockSpec((None, tr, qkv_w), lambda bi, i, d=d: (bi, tile_of(d, i), 0)),
            pl.BlockSpec((None, HALO, qkv_w), lambda bi, i, d=d: (bi, jnp.maximum(tile_of(d, i) * hb - 1, 0), 0)),
            pl.BlockSpec((None, HALO, qkv_w), lambda bi, i, d=d: (bi, jnp.minimum(tile_of(d, i) * hb + hb, nhb - 1), 0)),
            pl.BlockSpec((None, tr, LANE), lambda bi, i, d=d: (bi, tile_of(d, i), AB_BLK)),
        ]
        args += [p, p, p, p]
    in_specs += [pl.BlockSpec(conv_w.shape, lambda bi, i: (0, 0)), pl.BlockSpec(par.shape, lambda bi, i: (0, 0))]
    args += [conv_w, par]
    if rope is not None:
        for d in range(2):
            in_specs += [pl.BlockSpec((tr, DN_HEAD_DIM), lambda bi, i, d=d: (tile_of(d, i), 0))] * 2
            args += [rope[0], rope[1]]
    sshape = (2, DN_HEADS, DN_HEAD_DIM, DN_HEAD_DIM)
    in_specs += [pl.BlockSpec((None,) + sshape, lambda bi, i: (bi, 0, 0, 0, 0))]
    args += [s0]
    out_specs = [
        pl.BlockSpec((None, tr, DN_WIDTH), lambda bi, i: (bi, i, 0)),
        pl.BlockSpec((None, tr, DN_WIDTH), lambda bi, i: (bi, nt - 1 - i, 0)),
    ]
    out_shape = [jax.ShapeDtypeStruct((b, t, DN_WIDTH), F32)] * 2
    if emit_state:
        out_specs += [pl.BlockSpec((None,) + sshape, lambda bi, i: (bi, 0, 0, 0, 0))]
        out_shape += [jax.ShapeDtypeStruct((b,) + sshape, F32)]
    cw = DN_HEADS * c
    scratch = [
        pltpu.VMEM(sshape, F32),
        pltpu.VMEM((2, tr, DN_WIDTH), F32), pltpu.VMEM((2, tr, DN_WIDTH), F32), pltpu.VMEM((2, tr, DN_WIDTH), F32),
        pltpu.VMEM((2, GDN_TILE_CHUNKS, cw, 2 * DN_HEAD_DIM), F32),
        pltpu.VMEM((2, GDN_TILE_CHUNKS, c, cw), F32),
    ]
    return pl.pallas_call(
        functools.partial(_gdn_kernel, nt=nt, use_rope=rope is not None, emit_state=emit_state),
        grid=(b, nt),
        in_specs=in_specs,
        out_specs=out_specs,
        out_shape=out_shape,
        scratch_shapes=scratch,
        compiler_params=_params("parallel", "arbitrary"),
        name="gdn_ctx" if emit_state else "gdn_lat",
    )(*args)


def _na_kernel(q_ref, k_ref, v_ref, kc_ref, vc_ref, bias_ref, o_ref, *, rows):
    w = GRID_W
    nl = NA_WIN_R * w
    nr = NA_ROWS_PER_ITER
    scale = NA_HEAD_DIM ** -0.5
    kc = kc_ref[...].astype(BF16)
    vc = vc_ref[...].astype(BF16)

    def stack_heads(q2):
        sel = lax.broadcasted_iota(jnp.int32, q2.shape, 1) < NA_HEAD_DIM
        return jnp.concatenate([jnp.where(sel, q2, 0.0), jnp.where(sel, 0.0, q2)], axis=0)

    def unstack_heads(x):
        n = x.shape[0] // 2
        sel = lax.broadcasted_iota(jnp.int32, (n, LANE), 1) < NA_HEAD_DIM
        return jnp.where(sel, jnp.broadcast_to(x[:n], (n, LANE)), jnp.broadcast_to(x[n:], (n, LANE)))

    def row_block(it, carry):
        rr = [it * nr + u for u in range(nr)]
        rss = [jnp.clip(r - NA_WIN_R // 2, 0, rows - NA_WIN_R) for r in rr]
        rsl = [pl.ds(pl.multiple_of(r * w, w), w) for r in rr]
        wsl = [pl.ds(pl.multiple_of(rs * w, w), nl) for rs in rss]
        q_all = q_ref[pl.ds(pl.multiple_of(it * (nr * w), nr * w), nr * w), :] * scale
        s_c = _dot_nt(stack_heads(q_all), kc)
        s = [_dot_nt(stack_heads(q_all[u * w:(u + 1) * w]), k_ref[wsl[u], :]) + bias_ref[rss[u] - rr[u] + NA_WIN_R - 1]
             for u in range(nr)]
        m_c = jnp.max(s_c, axis=-1, keepdims=True)
        m_l = [jnp.max(x, axis=-1, keepdims=True) for x in s]
        p_c = jnp.exp(s_c - m_c)
        p = [jnp.exp(x - m) for x, m in zip(s, m_l)]
        o_c = unstack_heads(_dot(p_c, vc))
        o_l = [_dot(p[u], v_ref[wsl[u], :]) for u in range(nr)]
        l_c = unstack_heads(jnp.sum(p_c, axis=-1, keepdims=True))
        m_c = unstack_heads(m_c)
        for u in range(nr):
            usl = slice(u * w, (u + 1) * w)
            l_u = unstack_heads(jnp.sum(p[u], axis=-1, keepdims=True))
            m_u = unstack_heads(m_l[u])
            m = jnp.maximum(m_u, m_c[usl])
            a_l = jnp.exp(m_u - m)
            a_c = jnp.exp(m_c[usl] - m)
            o_ref[rsl[u], :] = (unstack_heads(o_l[u]) * a_l + o_c[usl] * a_c) / (l_u * a_l + l_c[usl] * a_c)
        return carry

    lax.fori_loop(0, rows // nr, row_block, 0)


def _na_call(p, pc, bias):
    b, t, _ = p.shape
    l = pc.shape[1]
    rows = t // GRID_W
    npair = NA_HEADS // 2
    return pl.pallas_call(
        functools.partial(_na_kernel, rows=rows),
        grid=(npair, b),
        in_specs=[
            pl.BlockSpec((None, t, LANE), lambda j, bi: (bi, 0, NA_BLK0 + j)),
            pl.BlockSpec((None, t, LANE), lambda j, bi: (bi, 0, NA_BLK0 + npair + j)),
            pl.BlockSpec((None, t, LANE), lambda j, bi: (bi, 0, NA_BLK0 + 2 * npair + j)),
            pl.BlockSpec((None, l, LANE), lambda j, bi: (bi, 0, NA_BLK0 + npair + j)),
            pl.BlockSpec((None, l, LANE), lambda j, bi: (bi, 0, NA_BLK0 + 2 * npair + j)),
            pl.BlockSpec((None,) + bias.shape[1:], lambda j, bi: (j, 0, 0, 0)),
        ],
        out_specs=pl.BlockSpec((None, t, LANE), lambda j, bi: (bi, 0, j)),
        out_shape=jax.ShapeDtypeStruct((b, t, NA_WIDTH), F32),
        compiler_params=_params("parallel", "parallel"),
        name="na_lat",
    )(p, p, p, pc, pc, bias)


def _ctx_attn_kernel(q_ref, k_ref, v_ref, o_ref):
    scale = NA_HEAD_DIM ** -0.5
    q2 = q_ref[...] * scale
    k2 = k_ref[...]
    v2 = v_ref[...]
    head0 = lax.broadcasted_iota(jnp.int32, q2.shape, 1) < NA_HEAD_DIM
    out = None
    for hh in range(2):
        sel = head0 if hh == 0 else ~head0
        s = _dot_nt(jnp.where(sel, q2, 0.0), k2)
        p = jnp.exp(s - jnp.max(s, axis=-1, keepdims=True))
        o = _dot(p, v2) / jnp.sum(p, axis=-1, keepdims=True)
        out = o if out is None else jnp.where(sel, o, out)
    o_ref[...] = out


def _ctx_attn_call(pc):
    b, l, _ = pc.shape
    npair = NA_HEADS // 2
    return pl.pallas_call(
        _ctx_attn_kernel,
        grid=(b, npair),
        in_specs=[
            pl.BlockSpec((None, l, LANE), lambda bi, j: (bi, 0, NA_BLK0 + j)),
            pl.BlockSpec((None, l, LANE), lambda bi, j: (bi, 0, NA_BLK0 + npair + j)),
            pl.BlockSpec((None, l, LANE), lambda bi, j: (bi, 0, NA_BLK0 + 2 * npair + j)),
        ],
        out_specs=pl.BlockSpec((None, l, LANE), lambda bi, j: (bi, 0, j)),
        out_shape=jax.ShapeDtypeStruct((b, l, NA_WIDTH), F32),
        compiler_params=_params("parallel", "parallel"),
        name="ctx_attn",
    )(pc, pc, pc)


def _merge_kernel(of_ref, ob_ref, z_ref, bg_ref, cg_ref, hh_ref, cgp_ref, cgn_ref, hhp_ref, hhn_ref, yc_ref,
                  ga_ref, gb_ref, gcg_ref, x_ref, gt_ref, wpa_ref, wpb_ref, wpc_ref, wo_ref, ng_ref, scw_ref,
                  o_ref, *, nt):
    i = pl.program_id(1)
    tm = x_ref.shape[0]
    o = of_ref[...] + ob_ref[...]
    z = z_ref[...]
    ya = []
    for h in range(DN_HEADS):
        sl = slice(h * DN_HEAD_DIM, (h + 1) * DN_HEAD_DIM)
        ya.append(_rms(o[:, sl]) * ng_ref[...] * _silu(z[:, sl]))
    y_a = jnp.concatenate(ya, axis=1)
    has_prev = (i > 0).astype(F32)
    has_next = (i < nt - 1).astype(F32)
    full = jnp.concatenate([cgp_ref[...] * hhp_ref[...] * has_prev, cg_ref[...] * hh_ref[...],
                            cgn_ref[...] * hhn_ref[...] * has_next], axis=0)
    y_b = bg_ref[...] * _dwconv3_rows(full, scw_ref, tm)
    y = (_sigmoid(ga_ref[...]) * _dot(y_a, wpa_ref[...]) + _sigmoid(gb_ref[...]) * _dot(y_b, wpb_ref[...])
         + _sigmoid(gcg_ref[...]) * _dot(yc_ref[...], wpc_ref[...]))
    o_ref[...] = x_ref[...] + gt_ref[...] * _dot(y, wo_ref[...])


def _merge_call(of, ob, p, yc, x, mod, mrow, wpa, wpb, wpc, wo, ng, scw, tm):
    b, t, d = x.shape
    nt = t // tm
    hb = tm // HALO
    nhb = t // HALO
    w5 = SC_WIDTH

    def tile(width, col):
        return pl.BlockSpec((None, tm, width), lambda bi, i: (bi, i, col))

    def prev(col):
        return pl.BlockSpec((None, HALO, w5), lambda bi, i: (bi, jnp.maximum(i * hb - 1, 0), col))

    def nxt(col):
        return pl.BlockSpec((None, HALO, w5), lambda bi, i: (bi, jnp.minimum(i * hb + hb, nhb - 1), col))

    def full(a):
        return pl.BlockSpec(a.shape, lambda bi, i: (0,) * a.ndim)

    in_specs = [
        tile(DN_WIDTH, 0), tile(DN_WIDTH, 0),
        tile(w5, 3), tile(w5, 4), tile(w5, 5), tile(w5, 6),
        prev(5), nxt(5), prev(6), nxt(6),
        tile(NA_WIDTH, 0),
        tile(d, 5), tile(d, 6), tile(d, 7),
        tile(d, 0),
        pl.BlockSpec((None, 1, d), lambda bi, i: (mrow(bi), 0, 2)),
        full(wpa), full(wpb), full(wpc), full(wo), full(ng), full(scw),
    ]
    return pl.pallas_call(
        functools.partial(_merge_kernel, nt=nt),
        grid=(b, nt),
        in_specs=in_specs,
        out_specs=pl.BlockSpec((None, tm, d), lambda bi, i: (bi, i, 0)),
        out_shape=jax.ShapeDtypeStruct((b, t, d), F32),
        compiler_params=_params("parallel", "parallel"),
        name="merge",
    )(of, ob, p, p, p, p, p, p, p, p, yc, p, p, p, x, mod, wpa, wpb, wpc, wo, ng, scw)


def _ffn_kernel(*refs, nt, nk, final):
    it = iter(refs)
    x_ref, xp_ref, xn_ref, g_ref, sh_ref, sc_ref, gt_ref = (next(it) for _ in range(7))
    wa_ref, wb_ref, cwa_ref, cwb_ref, cba_ref, cbb_ref, wd_ref = (next(it) for _ in range(7))
    fg_ref = next(it) if final else None
    o_ref, h_scr, acc_scr = next(it), next(it), next(it)
    i = pl.program_id(1)
    k = pl.program_id(2)
    tm = x_ref.shape[0]

    @pl.when(k == 0)
    def _():
        xfull = jnp.concatenate([xp_ref[...], x_ref[...], xn_ref[...]], axis=0)
        h = (_rms(xfull) * g_ref[...]) * (1.0 + sc_ref[...]) + sh_ref[...]
        h_scr[...] = h.astype(BF16)
        acc_scr[...] = jnp.zeros_like(acc_scr)

    row = lax.broadcasted_iota(jnp.int32, (tm + 2 * HALO, 1), 0)
    keep = ((row >= HALO) | (i > 0)) & ((row < tm + HALO) | (i < nt - 1))
    hfull = h_scr[...]
    ua = jnp.where(keep, jnp.dot(hfull, wa_ref[...], preferred_element_type=F32), 0.0)
    ub = jnp.where(keep, jnp.dot(hfull, wb_ref[...], preferred_element_type=F32), 0.0)
    a = _dwconv3_rows(ua, cwa_ref, tm) + cba_ref[...]
    bb = _dwconv3_rows(ub, cwb_ref, tm) + cbb_ref[...]
    acc_scr[...] += _dot(_silu(a) * bb, wd_ref[...])

    @pl.when(k == nk - 1)
    def _():
        out = x_ref[...] + gt_ref[...] * acc_scr[...]
        if final:
            out = _rms(out) * fg_ref[...]
        o_ref[...] = out


def _ffn_call(x, g, mod, mrow, w_up, cw, cb, w_down, tm, final_g=None):
    b, t, d = x.shape
    dff = w_down.shape[0]
    fc = 256
    nk = dff // fc
    nt = t // tm
    hb = tm // HALO
    nhb = t // HALO
    final = final_g is not None
    in_specs = [
        pl.BlockSpec((None, tm, d), lambda bi, i, k: (bi, i, 0)),
        pl.BlockSpec((None, HALO, d), lambda bi, i, k: (bi, jnp.maximum(i * hb - 1, 0), 0)),
        pl.BlockSpec((None, HALO, d), lambda bi, i, k: (bi, jnp.minimum(i * hb + hb, nhb - 1), 0)),
        pl.BlockSpec((1, d), lambda bi, i, k: (0, 0)),
        pl.BlockSpec((None, 1, d), lambda bi, i, k: (mrow(bi), 0, 3)),
        pl.BlockSpec((None, 1, d), lambda bi, i, k: (mrow(bi), 0, 4)),
        pl.BlockSpec((None, 1, d), lambda bi, i, k: (mrow(bi), 0, 5)),
        pl.BlockSpec((d, fc), lambda bi, i, k: (0, k)),
        pl.BlockSpec((d, fc), lambda bi, i, k: (0, nk + k)),
        pl.BlockSpec((3, fc), lambda bi, i, k: (0, k)),
        pl.BlockSpec((3, fc), lambda bi, i, k: (0, nk + k)),
        pl.BlockSpec((1, fc), lambda bi, i, k: (0, k)),
        pl.BlockSpec((1, fc), lambda bi, i, k: (0, nk + k)),
        pl.BlockSpec((fc, d), lambda bi, i, k: (k, 0)),
    ]
    args = [x, x, x, g, mod, mod, mod, w_up, w_up, cw, cw, cb, cb, w_down]
    if final:
        in_specs.append(pl.BlockSpec((1, d), lambda bi, i, k: (0, 0)))
        args.append(final_g)
    return pl.pallas_call(
        functools.partial(_ffn_kernel, nt=nt, nk=nk, final=final),
        grid=(b, nt, nk),
        in_specs=in_specs,
        out_specs=pl.BlockSpec((None, tm, d), lambda bi, i, k: (bi, i, 0)),
        out_shape=jax.ShapeDtypeStruct((b, t, d), F32),
        scratch_shapes=[pltpu.VMEM((tm + 2 * HALO, d), BF16), pltpu.VMEM((tm, d), F32)],
        compiler_params=_params("parallel", "parallel", "arbitrary"),
        name="ffn_final" if final else "ffn",
    )(*args)


def _rope_tables(t):
    tok = jnp.arange(t, dtype=jnp.int32)
    rows = (tok // GRID_W).astype(F32)
    cols = (tok % GRID_W).astype(F32)
    nf = DN_HEAD_DIM // 4
    inv = ROPE_BASE ** (-jnp.arange(nf, dtype=F32) / nf)
    ang = jnp.concatenate([rows[:, None] * inv, cols[:, None] * inv], axis=-1)
    cos = jnp.repeat(jnp.cos(ang), 2, axis=-1)
    sin = jnp.repeat(jnp.sin(ang), 2, axis=-1)
    sign = jnp.where(jnp.arange(DN_HEAD_DIM) % 2 == 0, -1.0, 1.0).astype(F32)
    return cos, sin * sign


def _na_bias_table(rpb):
    w = GRID_W
    qc = np.arange(w)[:, None]
    kc = np.arange(w)[None, :]
    cstart = np.clip(qc - NA_WIN_C // 2, 0, w - NA_WIN_C)
    valid = (kc >= cstart) & (kc < cstart + NA_WIN_C)
    cidx = np.clip(kc - qc + NA_WIN_C - 1, 0, 2 * NA_WIN_C - 2)
    tbl = jnp.where(jnp.asarray(valid), rpb[:, :, cidx], NEG)
    didx = np.arange(NA_WIN_R)[:, None] + np.arange(NA_WIN_R)[None, :]
    tbl = tbl[:, didx]
    tbl = tbl.transpose(0, 1, 3, 2, 4).reshape(NA_HEADS // 2, 2, NA_WIN_R, w, NA_WIN_R * w)
    return tbl.transpose(0, 2, 1, 3, 4).reshape(NA_HEADS // 2, NA_WIN_R, 2 * w, NA_WIN_R * w)


def _permute_w_in(w_in):
    s = np.cumsum([0, 3 * DN_WIDTH, DN_WIDTH, 2 * DN_HEADS, 2 * DN_HEADS, 3 * SC_WIDTH, 3 * NA_WIDTH, 3 * 1024])
    qkv, z, a, b, sc, na, gates = (w_in[..., s[j]:s[j + 1]] for j in range(7))
    pad = jnp.zeros(w_in.shape[:-1] + (LANE - 4 * DN_HEADS,), w_in.dtype)
    return jnp.concatenate([qkv, z, sc, na, gates, a, b, pad], axis=-1)


def kernel(x, c, ctx, c_ctx, norm1_g, norm2_g, w_ada, b_ada, w_in, dn_conv_w, dn_a_log, dn_dt_bias, dn_norm_g, sc_conv_w, na_rpb, w_pa, w_pb, w_pc, w_o, w_up, ffn_conv_w, ffn_conv_b, w_down, final_norm_g):
    bsz, t, d = x.shape
    depth = w_in.shape[0]
    assert d == 1024 and t % (NA_WIN_R * GRID_W) == 0 and ctx.shape[1] % (GDN_TILE_CHUNKS * DN_CHUNK) == 0

    mod_rows = -(-(bsz + 1) // SUBLANE) * SUBLANE
    cc = jnp.zeros((mod_rows, d), F32).at[:bsz].set(c).at[bsz].set(c_ctx)
    mod = _ada_call(cc, w_ada, b_ada).reshape(depth, mod_rows, 1, 6 * d)
    lat_row = lambda bi: bi
    ctx_row = lambda bi: bsz

    w_in_p = _permute_w_in(w_in).astype(BF16)
    rope = _rope_tables(t)
    par = jnp.zeros((depth, SUBLANE, LANE), F32)
    par = par.at[:, 0, :2 * DN_HEADS].set(dn_a_log.reshape(depth, -1)).at[:, 1, :2 * DN_HEADS].set(dn_dt_bias.reshape(depth, -1))
    s_zero = jnp.zeros((bsz, 2, DN_HEADS, DN_HEAD_DIM, DN_HEAD_DIM), F32)
    tm_lat = 1024 if t % 1024 == 0 else 512
    tm_ctx = ctx.shape[1]

    xc = ctx
    for l in range(depth):
        need_ctx = l < depth - 1
        g1 = norm1_g[l][None]
        g2 = norm2_g[l][None]
        wpa, wpb, wpc, wo = (a[l].astype(BF16) for a in (w_pa, w_pb, w_pc, w_o))
        wup, wdn = w_up[l].astype(BF16), w_down[l].astype(BF16)
        ng = dn_norm_g[l][None]
        p = _inproj_call(x, g1, mod[l], lat_row, w_in_p[l], tm_lat)
        pc = _inproj_call(xc, g1, mod[l], ctx_row, w_in_p[l], tm_ctx)
        ocf, ocb, s_ctx = _gdn_call(pc, dn_conv_w[l], par[l], s_zero, None, True)
        olf, olb = _gdn_call(p, dn_conv_w[l], par[l], s_ctx, rope, False)
        y_c = _na_call(p, pc, _na_bias_table(na_rpb[l]))
        x = _merge_call(olf, olb, p, y_c, x, mod[l], lat_row, wpa, wpb, wpc, wo, ng, sc_conv_w[l], 256)
        fg = final_norm_g[None] if l == depth - 1 else None
        x = _ffn_call(x, g2, mod[l], lat_row, wup, ffn_conv_w[l], ffn_conv_b[l][None], wdn, tm_lat, fg)
        if need_ctx:
            yc_c = _ctx_attn_call(pc)
            xc = _merge_call(ocf, ocb, pc, yc_c, xc, mod[l], ctx_row, wpa, wpb, wpc, wo, ng, sc_conv_w[l], tm_ctx)
            xc = _ffn_call(xc, g2, mod[l], ctx_row, wup, ffn_conv_w[l], ffn_conv_b[l][None], wdn, tm_ctx)
    return x
```

```python
import functools

import numpy as np
import jax
import jax.numpy as jnp
from jax import lax
from jax.experimental import pallas as pl
from jax.experimental.pallas import tpu as pltpu

F32 = jnp.float32
BF16 = jnp.bfloat16

GRID_W = 64
DN_HEADS = 4
DN_HEAD_DIM = 128
DN_WIDTH = DN_HEADS * DN_HEAD_DIM
DN_CHUNK = 64
GDN_TILE_CHUNKS = 4
SC_WIDTH = 512
NA_HEADS = 8
NA_HEAD_DIM = 64
NA_WIDTH = NA_HEADS * NA_HEAD_DIM
NA_WIN_R = 8
NA_WIN_C = 16
NA_ROWS_PER_ITER = 4
ROPE_BASE = 10000.0
EPS = 1e-6
NEG = -1e30

LANE = 128
SUBLANE = 8
HALO = SUBLANE
HALO_P = 2 * SUBLANE
VMEM_LIMIT = 48 * 1024 * 1024

P_COLS = 3 * DN_WIDTH + DN_WIDTH + 3 * SC_WIDTH + 3 * NA_WIDTH + 3 * 1024
P_TILE = 1024
Z_TILE = (3 * DN_WIDTH) // P_TILE
GATE_TILE0 = (P_COLS - 3 * 1024) // P_TILE
NA_BLK0 = (3 * DN_WIDTH + DN_WIDTH + 3 * SC_WIDTH) // LANE


def _dot(a, b):
    return jnp.dot(a.astype(BF16), b.astype(BF16), preferred_element_type=F32)


def _dot_nt(a, b):
    return lax.dot_general(a.astype(BF16), b.astype(BF16), (((1,), (1,)), ((), ())), preferred_element_type=F32)


def _dot_tn(a, b):
    return lax.dot_general(a.astype(BF16), b.astype(BF16), (((0,), (0,)), ((), ())), preferred_element_type=F32)


def _sigmoid(x):
    return 1.0 / (1.0 + jnp.exp(-x))


def _silu(x):
    return x * _sigmoid(x)


def _softplus(x):
    return jnp.maximum(x, 0.0) + jnp.log1p(jnp.exp(-jnp.abs(x)))


def _rms(x, eps=EPS):
    return x * lax.rsqrt(jnp.mean(x * x, axis=-1, keepdims=True) + eps)


def _params(*sem):
    return pltpu.CompilerParams(dimension_semantics=sem, vmem_limit_bytes=VMEM_LIMIT)


def _dwconv3_rows(full, w_ref, n, halo=HALO):
    rows = full.shape[0]
    dn = pltpu.roll(full, 1, axis=0)
    up = pltpu.roll(full, rows - 1, axis=0)
    out = dn * w_ref[0:1, :] + full * w_ref[1:2, :] + up * w_ref[2:3, :]
    return out[halo:halo + n]


def _ada_kernel(c_ref, w_ref, b_ref, o_ref):
    o_ref[...] = _dot(_silu(c_ref[...]), w_ref[...]) + b_ref[...]


def _ada_call(cc, w_ada, b_ada):
    depth, d, n6 = w_ada.shape
    r = cc.shape[0]
    tn = 1536
    return pl.pallas_call(
        _ada_kernel,
        grid=(depth, n6 // tn),
        in_specs=[
            pl.BlockSpec((r, d), lambda l, j: (0, 0)),
            pl.BlockSpec((None, d, tn), lambda l, j: (l, 0, j)),
            pl.BlockSpec((None, 1, tn), lambda l, j: (l, 0, j)),
        ],
        out_specs=pl.BlockSpec((None, r, tn), lambda l, j: (l, 0, j)),
        out_shape=jax.ShapeDtypeStruct((depth, r, n6), F32),
        compiler_params=_params("parallel", "parallel"),
        name="ada_mod",
    )(cc, w_ada, b_ada.reshape(depth, 1, n6))


def _inproj_kernel(x_ref, g_ref, sh_ref, sc_ref, w_ref, wab_ref, o_ref, ab_ref, h_ref):
    j = pl.program_id(2)

    @pl.when(j == 0)
    def _():
        h = (_rms(x_ref[...]) * g_ref[...]) * (1.0 + sc_ref[...]) + sh_ref[...]
        h_ref[...] = h.astype(BF16)
        ab_ref[...] = jnp.dot(h_ref[...], wab_ref[...], preferred_element_type=F32)

    res = jnp.dot(h_ref[...], w_ref[...], preferred_element_type=F32)
    sg = _sigmoid(res)
    z_cols = lax.broadcasted_iota(jnp.int32, (1, res.shape[1]), 1) >= res.shape[1] // 2
    out = jnp.where(j >= GATE_TILE0, sg, jnp.where((j == Z_TILE) & z_cols, res * sg, res))
    o_ref[...] = out.astype(o_ref.dtype)


def _inproj_call(x, g, mod, mrow, w, wab, tm):
    b, t, d = x.shape
    n = w.shape[1]
    tn = P_TILE
    return pl.pallas_call(
        _inproj_kernel,
        grid=(b, t // tm, n // tn),
        in_specs=[
            pl.BlockSpec((None, tm, d), lambda bi, i, j: (bi, i, 0)),
            pl.BlockSpec((1, d), lambda bi, i, j: (0, 0)),
            pl.BlockSpec((None, 1, d), lambda bi, i, j: (mrow(bi), 0, 0)),
            pl.BlockSpec((None, 1, d), lambda bi, i, j: (mrow(bi), 0, 1)),
            pl.BlockSpec((d, tn), lambda bi, i, j: (0, j)),
            pl.BlockSpec((d, LANE), lambda bi, i, j: (0, 0)),
        ],
        out_specs=[pl.BlockSpec((None, tm, tn), lambda bi, i, j: (bi, i, j)),
                   pl.BlockSpec((None, tm, LANE), lambda bi, i, j: (bi, i, 0))],
        out_shape=[jax.ShapeDtypeStruct((b, t, n), BF16), jax.ShapeDtypeStruct((b, t, LANE), F32)],
        scratch_shapes=[pltpu.VMEM((tm, d), BF16)],
        compiler_params=_params("parallel", "parallel", "arbitrary"),
        name="inproj",
    )(x, g, mod, mod, w, wab)


def _swap_pairs(x, even):
    n = x.shape[-1]
    return jnp.where(even, pltpu.roll(x, n - 1, axis=1), pltpu.roll(x, 1, axis=1))


def _gdn_kernel(*refs, nt, use_rope, emit_state):
    it = iter(refs)
    xs = [[next(it) for _ in range(4)] for _ in range(2)]
    cw_ref = next(it)
    par_ref = next(it)
    ropes = [[next(it), next(it)] for _ in range(2)] if use_rope else None
    s0_ref = next(it)
    o_refs = [next(it), next(it)]
    sout_ref = next(it) if emit_state else None
    s_scr, q_scr, k_scr, v_scr, uw_scr, qkd_scr = (next(it) for _ in range(6))

    c = DN_CHUNK
    nh = DN_HEADS
    hd = DN_HEAD_DIM
    nch = GDN_TILE_CHUNKS
    tr = nch * c
    cw = nh * c
    lc = c.bit_length() - 1
    i = pl.program_id(1)

    @pl.when(i == 0)
    def _():
        s_scr[...] = s0_ref[...]

    ri = lax.broadcasted_iota(jnp.int32, (c, cw), 0)
    li = lax.broadcasted_iota(jnp.int32, (c, cw), 1)
    jj = li & (c - 1)
    seg = li >> lc
    eye_cat = ri == jj
    bd_sq = jnp.where((lax.broadcasted_iota(jnp.int32, (cw, cw), 0) >> lc)
                      == (lax.broadcasted_iota(jnp.int32, (cw, cw), 1) >> lc), 1.0, 0.0).astype(BF16)
    bd_k = jnp.where((lax.broadcasted_iota(jnp.int32, (cw, nh * hd), 0) >> lc)
                     == (lax.broadcasted_iota(jnp.int32, (cw, nh * hd), 1) // hd), 1.0, 0.0).astype(BF16)
    even = (lax.broadcasted_iota(jnp.int32, (tr, hd), 1) % 2) == 0
    rowi = lax.broadcasted_iota(jnp.int32, (tr, LANE), 0) & (c - 1)
    scale = hd ** -0.5
    neg_a = -jnp.exp(par_ref[0:1, :])
    dt_b = par_ref[1:2, :]

    def col_bcast(x, j, width):
        return jnp.broadcast_to(x[:, j:j + 1], (x.shape[0], width))

    def colcat(x, base):
        out = col_bcast(x, base + nh - 1, cw)
        for h in range(nh - 2, -1, -1):
            out = jnp.where(seg == h, col_bcast(x, base + h, cw), out)
        return out

    def block_diag(m, mask):
        return jnp.concatenate([m.astype(BF16)] * nh, axis=0) * mask

    nar = []
    for d in range(2):
        x_ref, xp_ref, xn_ref, ab_ref = xs[d]
        tile = i if d == 0 else nt - 1 - i
        has_prev = (tile > 0).astype(F32)
        has_next = (tile < nt - 1).astype(F32)
        full = jnp.concatenate([xp_ref[...].astype(F32) * has_prev, x_ref[...].astype(F32),
                                xn_ref[...].astype(F32) * has_next], axis=0)
        act = _silu(_dwconv3_rows(full, cw_ref, tr, HALO_P))
        for h in range(nh):
            sl = slice(h * hd, (h + 1) * hd)
            qh = act[:, h * hd:(h + 1) * hd]
            kh = act[:, DN_WIDTH + h * hd:DN_WIDTH + (h + 1) * hd]
            qh = qh * lax.rsqrt(jnp.sum(qh * qh, axis=-1, keepdims=True) + EPS)
            kh = kh * lax.rsqrt(jnp.sum(kh * kh, axis=-1, keepdims=True) + EPS)
            if use_rope:
                cos = ropes[d][0][...]
                sin = ropes[d][1][...]
                qh = qh * cos + _swap_pairs(qh, even) * sin
                kh = kh * cos + _swap_pairs(kh, even) * sin
            q_scr[d, :, sl] = qh * scale
            k_scr[d, :, sl] = kh
        v_scr[d] = act[:, 2 * DN_WIDTH:]
        ab = ab_ref[...]
        gc = neg_a * _softplus(ab + dt_b)
        beta = pltpu.roll(_sigmoid(ab), LANE - 2 * nh, axis=1)
        sft = 1
        while sft < c:
            if d == 0:
                gc = gc + jnp.where(rowi >= sft, pltpu.roll(gc, sft, axis=0), 0.0)
            else:
                gc = gc + jnp.where(rowi < c - sft, pltpu.roll(gc, tr - sft, axis=0), 0.0)
            sft *= 2
        e1 = jnp.exp(gc)
        nar.append(dict(gc=gc, e1=e1, beta=beta, be=beta * e1))

    probs = [(d, ch) for ch in range(nch) for d in range(2)]
    st = {}
    for d, ch in probs:
        rs = slice(ch * c, (ch + 1) * c)
        base = d * nh
        incl = (ri >= jj) if d == 0 else (ri <= jj)
        gc_c = nar[d]["gc"][rs]
        gcc = colcat(gc_c, base)
        gcr = jnp.broadcast_to(jnp.sum(jnp.where(eye_cat, gcc, 0.0), axis=0, keepdims=True), (c, cw))
        decay = jnp.where(incl, jnp.exp(jnp.where(incl, gcc - gcr, 0.0)), 0.0)
        kf = k_scr[d, rs, :]
        prod = _dot_nt(jnp.concatenate([q_scr[d, rs, :], kf], axis=0), block_diag(kf, bd_k))
        qkd_scr[d, ch] = prod[:c] * decay
        amat = jnp.where(incl & (~eye_cat), colcat(nar[d]["beta"][rs], base) * prod[c:] * decay, 0.0)
        lvl1 = ((ri >> 1) == (jj >> 1))
        st[d, ch] = dict(amat=amat.astype(BF16), tinv=jnp.where(eye_cat, 1.0, 0.0) - jnp.where(lvl1, amat, 0.0))
    s = 2
    while s < c:
        ls = s.bit_length() - 1
        lvl = jnp.where(((ri >> (ls + 1)) == (jj >> (ls + 1))) & ((ri >> ls) != (jj >> ls)), 1.0, 0.0).astype(BF16)
        for key in probs:
            p = st[key]
            p["t_bf"] = p["tinv"].astype(BF16)
            p["x"] = _dot(p["t_bf"], block_diag(p["amat"] * lvl, bd_sq))
        for key in probs:
            p = st[key]
            p["tinv"] = p["tinv"] - _dot(p["x"], block_diag(p["t_bf"], bd_sq))
        s *= 2
    for d, ch in probs:
        rs = slice(ch * c, (ch + 1) * c)
        base = d * nh
        rhs = jnp.concatenate([
            jnp.concatenate([v_scr[d, rs, h * hd:(h + 1) * hd] * col_bcast(nar[d]["beta"][rs], base + h, hd),
                             k_scr[d, rs, h * hd:(h + 1) * hd] * col_bcast(nar[d]["be"][rs], base + h, hd)], axis=1)
            for h in range(nh)], axis=0)
        uw_scr[d, ch] = _dot(block_diag(st[d, ch]["tinv"], bd_sq), rhs)
    states = [[s_scr[d, h] for h in range(nh)] for d in range(2)]
    for step in range(nch):
        cur = [(0, step), (1, nch - 1 - step)]
        loc = {}
        for d, ch in cur:
            rs = slice(ch * c, (ch + 1) * c)
            base = d * nh
            gc_c = nar[d]["gc"][rs]
            gtot = gc_c[c - 1:c] if d == 0 else gc_c[0:1]
            e2 = jnp.exp(gtot - gc_c)
            r_ = []
            for h in range(nh):
                hs = slice(h * hd, (h + 1) * hd)
                w_h = uw_scr[d, ch, h * c:(h + 1) * c, hd:]
                qd_h = q_scr[d, rs, hs] * col_bcast(nar[d]["e1"][rs], base + h, hd)
                r_.append(_dot(jnp.concatenate([w_h, qd_h], axis=0), states[d][h]))
            loc[d] = dict(r=r_, e2=e2, gtot=gtot)
        for d, ch in cur:
            loc[d]["vn"] = [uw_scr[d, ch, h * c:(h + 1) * c, :hd] - loc[d]["r"][h][:c] for h in range(nh)]
        for d, ch in cur:
            rs = slice(ch * c, (ch + 1) * c)
            base = d * nh
            vn = loc[d]["vn"]
            o2 = _dot(block_diag(qkd_scr[d, ch], bd_sq), jnp.concatenate(vn, axis=0))
            for h in range(nh):
                hs = slice(h * hd, (h + 1) * hd)
                kd_h = k_scr[d, rs, hs] * col_bcast(loc[d]["e2"], base + h, hd)
                o_refs[d][rs, hs] = loc[d]["r"][h][c:] + o2[h * c:(h + 1) * c]
                eg = jnp.exp(jnp.broadcast_to(loc[d]["gtot"][0:1, base + h:base + h + 1], (hd, hd)))
                states[d][h] = states[d][h] * eg + _dot_tn(kd_h, vn[h])
    for d in range(2):
        for h in range(nh):
            s_scr[d, h] = states[d][h]

    if emit_state:
        @pl.when(i == nt - 1)
        def _():
            sout_ref[...] = s_scr[...]


def _gdn_call(p, ab, conv_w, par, s0, rope, emit_state):
    b, t, _ = p.shape
    c = DN_CHUNK
    tr = GDN_TILE_CHUNKS * c
    nt = t // tr
    hb = tr // HALO_P
    nhb = t // HALO_P
    qkv_w = 3 * DN_WIDTH

    def tile_of(d, i):
        return i if d == 0 else nt - 1 - i

    in_specs = []
    args = []
    for d in range(2):
        in_specs += [
            pl.BlockSpec((None, tr, qkv_w), lambda bi, i, d=d: (bi, tile_of(d, i), 0)),
            pl.BlockSpec((None, HALO_P, qkv_w), lambda bi, i, d=d: (bi, jnp.maximum(tile_of(d, i) * hb - 1, 0), 0)),
            pl.BlockSpec((None, HALO_P, qkv_w), lambda bi, i, d=d: (bi, jnp.minimum(tile_of(d, i) * hb + hb, nhb - 1), 0)),
            pl.BlockSpec((None, tr, LANE), lambda bi, i, d=d: (bi, tile_of(d, i), 0)),
        ]
        args += [p, p, p, ab]
    in_specs += [pl.BlockSpec(conv_w.shape, lambda bi, i: (0, 0)), pl.BlockSpec(par.shape, lambda bi, i: (0, 0))]
    args += [conv_w, par]
    if rope is not None:
        for d in range(2):
            in_specs += [pl.BlockSpec((tr, DN_HEAD_DIM), lambda bi, i, d=d: (tile_of(d, i), 0))] * 2
            args += [rope[0], rope[1]]
    sshape = (2, DN_HEADS, DN_HEAD_DIM, DN_HEAD_DIM)
    in_specs += [pl.BlockSpec((None,) + sshape, lambda bi, i: (bi, 0, 0, 0, 0))]
    args += [s0]
    out_specs = [
        pl.BlockSpec((None, tr, DN_WIDTH), lambda bi, i: (bi, i, 0)),
        pl.BlockSpec((None, tr, DN_WIDTH), lambda bi, i: (bi, nt - 1 - i, 0)),
    ]
    out_shape = [jax.ShapeDtypeStruct((b, t, DN_WIDTH), F32)] * 2
    if emit_state:
        out_specs += [pl.BlockSpec((None,) + sshape, lambda bi, i: (bi, 0, 0, 0, 0))]
        out_shape += [jax.ShapeDtypeStruct((b,) + sshape, F32)]
    cw = DN_HEADS * c
    scratch = [
        pltpu.VMEM(sshape, F32),
        pltpu.VMEM((2, tr, DN_WIDTH), F32), pltpu.VMEM((2, tr, DN_WIDTH), F32), pltpu.VMEM((2, tr, DN_WIDTH), F32),
        pltpu.VMEM((2, GDN_TILE_CHUNKS, cw, 2 * DN_HEAD_DIM), F32),
        pltpu.VMEM((2, GDN_TILE_CHUNKS, c, cw), F32),
    ]
    return pl.pallas_call(
        functools.partial(_gdn_kernel, nt=nt, use_rope=rope is not None, emit_state=emit_state),
        grid=(b, nt),
        in_specs=in_specs,
        out_specs=out_specs,
        out_shape=out_shape,
        scratch_shapes=scratch,
        compiler_params=_params("parallel", "arbitrary"),
        name="gdn_ctx" if emit_state else "gdn_lat",
    )(*args)


def _na_kernel(q_ref, k_ref, v_ref, kc_ref, vc_ref, bias_ref, o_ref, *, rows):
    w = GRID_W
    nl = NA_WIN_R * w
    nr = NA_ROWS_PER_ITER
    scale = NA_HEAD_DIM ** -0.5
    kc = kc_ref[...].astype(BF16)
    vc = vc_ref[...].astype(BF16)

    def stack_heads(q2):
        sel = lax.broadcasted_iota(jnp.int32, q2.shape, 1) < NA_HEAD_DIM
        zero = jnp.zeros_like(q2)
        return jnp.concatenate([jnp.where(sel, q2, zero), jnp.where(sel, zero, q2)], axis=0)

    def unstack_heads(x):
        n = x.shape[0] // 2
        sel = lax.broadcasted_iota(jnp.int32, (n, LANE), 1) < NA_HEAD_DIM
        return jnp.where(sel, jnp.broadcast_to(x[:n], (n, LANE)), jnp.broadcast_to(x[n:], (n, LANE)))

    def row_block(it, carry):
        rr = [it * nr + u for u in range(nr)]
        rss = [jnp.clip(r - NA_WIN_R // 2, 0, rows - NA_WIN_R) for r in rr]
        rsl = [pl.ds(pl.multiple_of(r * w, w), w) for r in rr]
        wsl = [pl.ds(pl.multiple_of(rs * w, w), nl) for rs in rss]
        q_all = q_ref[pl.ds(pl.multiple_of(it * (nr * w), nr * w), nr * w), :] * scale
        s_c = _dot_nt(stack_heads(q_all), kc)
        s = [_dot_nt(stack_heads(q_all[u * w:(u + 1) * w]), k_ref[wsl[u], :]) + bias_ref[rss[u] - rr[u] + NA_WIN_R - 1]
             for u in range(nr)]
        m_c = jnp.max(s_c, axis=-1, keepdims=True)
        m_l = [jnp.max(x, axis=-1, keepdims=True) for x in s]
        p_c = jnp.exp(s_c - m_c)
        p = [jnp.exp(x - m) for x, m in zip(s, m_l)]
        o_c = unstack_heads(_dot(p_c, vc))
        o_l = [_dot(p[u], v_ref[wsl[u], :]) for u in range(nr)]
        l_c = unstack_heads(jnp.sum(p_c, axis=-1, keepdims=True))
        m_c = unstack_heads(m_c)
        for u in range(nr):
            usl = slice(u * w, (u + 1) * w)
            l_u = unstack_heads(jnp.sum(p[u], axis=-1, keepdims=True))
            m_u = unstack_heads(m_l[u])
            m = jnp.maximum(m_u, m_c[usl])
            a_l = jnp.exp(m_u - m)
            a_c = jnp.exp(m_c[usl] - m)
            o_ref[rsl[u], :] = (unstack_heads(o_l[u]) * a_l + o_c[usl] * a_c) / (l_u * a_l + l_c[usl] * a_c)
        return carry

    lax.fori_loop(0, rows // nr, row_block, 0)


def _na_call(p, pc, bias):
    b, t, _ = p.shape
    l = pc.shape[1]
    rows = t // GRID_W
    npair = NA_HEADS // 2
    return pl.pallas_call(
        functools.partial(_na_kernel, rows=rows),
        grid=(npair, b),
        in_specs=[
            pl.BlockSpec((None, t, LANE), lambda j, bi: (bi, 0, NA_BLK0 + j)),
            pl.BlockSpec((None, t, LANE), lambda j, bi: (bi, 0, NA_BLK0 + npair + j)),
            pl.BlockSpec((None, t, LANE), lambda j, bi: (bi, 0, NA_BLK0 + 2 * npair + j)),
            pl.BlockSpec((None, l, LANE), lambda j, bi: (bi, 0, NA_BLK0 + npair + j)),
            pl.BlockSpec((None, l, LANE), lambda j, bi: (bi, 0, NA_BLK0 + 2 * npair + j)),
            pl.BlockSpec((None,) + bias.shape[1:], lambda j, bi: (j, 0, 0, 0)),
        ],
        out_specs=pl.BlockSpec((None, t, LANE), lambda j, bi: (bi, 0, j)),
        out_shape=jax.ShapeDtypeStruct((b, t, NA_WIDTH), F32),
        compiler_params=_params("parallel", "parallel"),
        name="na_lat",
    )(p, p, p, pc, pc, bias)


def _ctx_attn_kernel(q_ref, k_ref, v_ref, o_ref):
    scale = NA_HEAD_DIM ** -0.5
    q2 = q_ref[...] * scale
    k2 = k_ref[...]
    v2 = v_ref[...]
    head0 = lax.broadcasted_iota(jnp.int32, q2.shape, 1) < NA_HEAD_DIM
    out = None
    for hh in range(2):
        sel = head0 if hh == 0 else ~head0
        s = _dot_nt(jnp.where(sel, q2, jnp.zeros_like(q2)), k2)
        p = jnp.exp(s - jnp.max(s, axis=-1, keepdims=True))
        o = _dot(p, v2) / jnp.sum(p, axis=-1, keepdims=True)
        out = o if out is None else jnp.where(sel, o, out)
    o_ref[...] = out


def _ctx_attn_call(pc):
    b, l, _ = pc.shape
    npair = NA_HEADS // 2
    return pl.pallas_call(
        _ctx_attn_kernel,
        grid=(b, npair),
        in_specs=[
            pl.BlockSpec((None, l, LANE), lambda bi, j: (bi, 0, NA_BLK0 + j)),
            pl.BlockSpec((None, l, LANE), lambda bi, j: (bi, 0, NA_BLK0 + npair + j)),
            pl.BlockSpec((None, l, LANE), lambda bi, j: (bi, 0, NA_BLK0 + 2 * npair + j)),
        ],
        out_specs=pl.BlockSpec((None, l, LANE), lambda bi, j: (bi, 0, j)),
        out_shape=jax.ShapeDtypeStruct((b, l, NA_WIDTH), F32),
        compiler_params=_params("parallel", "parallel"),
        name="ctx_attn",
    )(pc, pc, pc)


def _merge_kernel(of_ref, ob_ref, z_ref, bg_ref, cg_ref, hh_ref, cgp_ref, cgn_ref, hhp_ref, hhn_ref, yc_ref,
                  ga_ref, gb_ref, gcg_ref, x_ref, gt_ref, wpa_ref, wpb_ref, wpc_ref, wo_ref, ng_ref, scw_ref,
                  o_ref, *, nt):
    i = pl.program_id(1)
    tm = x_ref.shape[0]
    o = of_ref[...] + ob_ref[...]
    sz = z_ref[...].astype(F32)
    ya = []
    for h in range(DN_HEADS):
        sl = slice(h * DN_HEAD_DIM, (h + 1) * DN_HEAD_DIM)
        ya.append(_rms(o[:, sl]) * ng_ref[...] * sz[:, sl])
    y_a = jnp.concatenate(ya, axis=1)
    has_prev = (i > 0).astype(F32)
    has_next = (i < nt - 1).astype(F32)
    f32 = lambda r: r[...].astype(F32)
    full = jnp.concatenate([f32(cgp_ref) * f32(hhp_ref) * has_prev, f32(cg_ref) * f32(hh_ref),
                            f32(cgn_ref) * f32(hhn_ref) * has_next], axis=0)
    y_b = f32(bg_ref) * _dwconv3_rows(full, scw_ref, tm, HALO_P)
    y = (f32(ga_ref) * _dot(y_a, wpa_ref[...]) + f32(gb_ref) * _dot(y_b, wpb_ref[...])
         + f32(gcg_ref) * _dot(yc_ref[...], wpc_ref[...]))
    o_ref[...] = x_ref[...] + gt_ref[...] * _dot(y, wo_ref[...])


def _merge_call(of, ob, p, yc, x, mod, mrow, wpa, wpb, wpc, wo, ng, scw, tm):
    b, t, d = x.shape
    nt = t // tm
    hb = tm // HALO_P
    nhb = t // HALO_P
    w5 = SC_WIDTH

    def tile(width, col):
        return pl.BlockSpec((None, tm, width), lambda bi, i: (bi, i, col))

    def prev(col):
        return pl.BlockSpec((None, HALO_P, w5), lambda bi, i: (bi, jnp.maximum(i * hb - 1, 0), col))

    def nxt(col):
        return pl.BlockSpec((None, HALO_P, w5), lambda bi, i: (bi, jnp.minimum(i * hb + hb, nhb - 1), col))

    def full(a):
        return pl.BlockSpec(a.shape, lambda bi, i: (0,) * a.ndim)

    in_specs = [
        tile(DN_WIDTH, 0), tile(DN_WIDTH, 0),
        tile(w5, 3), tile(w5, 4), tile(w5, 5), tile(w5, 6),
        prev(5), nxt(5), prev(6), nxt(6),
        tile(NA_WIDTH, 0),
        tile(d, 5), tile(d, 6), tile(d, 7),
        tile(d, 0),
        pl.BlockSpec((None, 1, d), lambda bi, i: (mrow(bi), 0, 2)),
        full(wpa), full(wpb), full(wpc), full(wo), full(ng), full(scw),
    ]
    return pl.pallas_call(
        functools.partial(_merge_kernel, nt=nt),
        grid=(b, nt),
        in_specs=in_specs,
        out_specs=pl.BlockSpec((None, tm, d), lambda bi, i: (bi, i, 0)),
        out_shape=jax.ShapeDtypeStruct((b, t, d), F32),
        compiler_params=_params("parallel", "parallel"),
        name="merge",
    )(of, ob, p, p, p, p, p, p, p, p, yc, p, p, p, x, mod, wpa, wpb, wpc, wo, ng, scw)


def _ffn_kernel(*refs, nt, nk, nsub, final):
    it = iter(refs)
    x_ref, xp_ref, xn_ref, g_ref, sh_ref, sc_ref, gt_ref = (next(it) for _ in range(7))
    wa_ref, wb_ref, cwa_ref, cwb_ref, cba_ref, cbb_ref, wd_ref = (next(it) for _ in range(7))
    fg_ref = next(it) if final else None
    o_ref, h_scr, acc_scr = next(it), next(it), next(it)
    i = pl.program_id(1)
    k = pl.program_id(2)
    tm = x_ref.shape[0]

    @pl.when(k == 0)
    def _():
        xfull = jnp.concatenate([xp_ref[...], x_ref[...], xn_ref[...]], axis=0)
        h = (_rms(xfull) * g_ref[...]) * (1.0 + sc_ref[...]) + sh_ref[...]
        h_scr[...] = h.astype(BF16)
        acc_scr[...] = jnp.zeros_like(acc_scr)

    sb = tm // nsub
    row = lax.broadcasted_iota(jnp.int32, (sb + 2 * HALO, 1), 0)

    def up_proj(s):
        hs = h_scr[s * sb:s * sb + sb + 2 * HALO, :]
        us = [jnp.dot(hs, w_ref[...], preferred_element_type=F32) for w_ref in (wa_ref, wb_ref)]
        if s == 0:
            us = [jnp.where((row >= HALO) | (i > 0), u, 0.0) for u in us]
        if s == nsub - 1:
            us = [jnp.where((row < sb + HALO) | (i < nt - 1), u, 0.0) for u in us]
        return us

    ups = {0: up_proj(0)}
    for s in range(nsub):
        if s + 1 < nsub:
            ups[s + 1] = up_proj(s + 1)
        ua, ub = ups.pop(s)
        a = _dwconv3_rows(ua, cwa_ref, sb) + cba_ref[...]
        bb = _dwconv3_rows(ub, cwb_ref, sb) + cbb_ref[...]
        acc_scr[s * sb:(s + 1) * sb, :] += _dot(_silu(a) * bb, wd_ref[...])

    @pl.when(k == nk - 1)
    def _():
        out = x_ref[...] + gt_ref[...] * acc_scr[...]
        if final:
            out = _rms(out) * fg_ref[...]
        o_ref[...] = out


def _ffn_call(x, g, mod, mrow, w_up, cw, cb, w_down, tm, final_g=None):
    b, t, d = x.shape
    dff = w_down.shape[0]
    fc = 256
    nk = dff // fc
    nt = t // tm
    hb = tm // HALO
    nhb = t // HALO
    final = final_g is not None
    nsub = max(tm // 256, 1)
    in_specs = [
        pl.BlockSpec((None, tm, d), lambda bi, i, k: (bi, i, 0)),
        pl.BlockSpec((None, HALO, d), lambda bi, i, k: (bi, jnp.maximum(i * hb - 1, 0), 0)),
        pl.BlockSpec((None, HALO, d), lambda bi, i, k: (bi, jnp.minimum(i * hb + hb, nhb - 1), 0)),
        pl.BlockSpec((1, d), lambda bi, i, k: (0, 0)),
        pl.BlockSpec((None, 1, d), lambda bi, i, k: (mrow(bi), 0, 3)),
        pl.BlockSpec((None, 1, d), lambda bi, i, k: (mrow(bi), 0, 4)),
        pl.BlockSpec((None, 1, d), lambda bi, i, k: (mrow(bi), 0, 5)),
        pl.BlockSpec((d, fc), lambda bi, i, k: (0, k)),
        pl.BlockSpec((d, fc), lambda bi, i, k: (0, nk + k)),
        pl.BlockSpec((3, fc), lambda bi, i, k: (0, k)),
        pl.BlockSpec((3, fc), lambda bi, i, k: (0, nk + k)),
        pl.BlockSpec((1, fc), lambda bi, i, k: (0, k)),
        pl.BlockSpec((1, fc), lambda bi, i, k: (0, nk + k)),
        pl.BlockSpec((fc, d), lambda bi, i, k: (k, 0)),
    ]
    args = [x, x, x, g, mod, mod, mod, w_up, w_up, cw, cw, cb, cb, w_down]
    if final:
        in_specs.append(pl.BlockSpec((1, d), lambda bi, i, k: (0, 0)))
        args.append(final_g)
    return pl.pallas_call(
        functools.partial(_ffn_kernel, nt=nt, nk=nk, nsub=nsub, final=final),
        grid=(b, nt, nk),
        in_specs=in_specs,
        out_specs=pl.BlockSpec((None, tm, d), lambda bi, i, k: (bi, i, 0)),
        out_shape=jax.ShapeDtypeStruct((b, t, d), F32),
        scratch_shapes=[pltpu.VMEM((tm + 2 * HALO, d), BF16), pltpu.VMEM((tm, d), F32)],
        compiler_params=_params("parallel", "parallel", "arbitrary"),
        name="ffn_final" if final else "ffn",
    )(*args)


def _rope_tables(t):
    tok = jnp.arange(t, dtype=jnp.int32)
    rows = (tok // GRID_W).astype(F32)
    cols = (tok % GRID_W).astype(F32)
    nf = DN_HEAD_DIM // 4
    inv = ROPE_BASE ** (-jnp.arange(nf, dtype=F32) / nf)
    ang = jnp.concatenate([rows[:, None] * inv, cols[:, None] * inv], axis=-1)
    cos = jnp.repeat(jnp.cos(ang), 2, axis=-1)
    sin = jnp.repeat(jnp.sin(ang), 2, axis=-1)
    sign = jnp.where(jnp.arange(DN_HEAD_DIM) % 2 == 0, -1.0, 1.0).astype(F32)
    return cos, sin * sign


def _na_bias_table(rpb):
    w = GRID_W
    qc = np.arange(w)[:, None]
    kc = np.arange(w)[None, :]
    cstart = np.clip(qc - NA_WIN_C // 2, 0, w - NA_WIN_C)
    valid = (kc >= cstart) & (kc < cstart + NA_WIN_C)
    cidx = np.clip(kc - qc + NA_WIN_C - 1, 0, 2 * NA_WIN_C - 2)
    tbl = jnp.where(jnp.asarray(valid), rpb[:, :, cidx], NEG)
    didx = np.arange(NA_WIN_R)[:, None] + np.arange(NA_WIN_R)[None, :]
    tbl = tbl[:, didx]
    tbl = tbl.transpose(0, 1, 3, 2, 4).reshape(NA_HEADS // 2, 2, NA_WIN_R, w, NA_WIN_R * w)
    return tbl.transpose(0, 2, 1, 3, 4).reshape(NA_HEADS // 2, NA_WIN_R, 2 * w, NA_WIN_R * w)


def _permute_w_in(w_in):
    s = np.cumsum([0, 3 * DN_WIDTH, DN_WIDTH, 2 * DN_HEADS, 2 * DN_HEADS, 3 * SC_WIDTH, 3 * NA_WIDTH, 3 * 1024])
    qkv, z, a, b, sc, na, gates = (w_in[..., s[j]:s[j + 1]] for j in range(7))
    pad = jnp.zeros(w_in.shape[:-1] + (LANE - 4 * DN_HEADS,), w_in.dtype)
    return jnp.concatenate([qkv, z, sc, na, gates], axis=-1), jnp.concatenate([a, b, pad], axis=-1)


def kernel(x, c, ctx, c_ctx, norm1_g, norm2_g, w_ada, b_ada, w_in, dn_conv_w, dn_a_log, dn_dt_bias, dn_norm_g, sc_conv_w, na_rpb, w_pa, w_pb, w_pc, w_o, w_up, ffn_conv_w, ffn_conv_b, w_down, final_norm_g):
    bsz, t, d = x.shape
    depth = w_in.shape[0]
    assert d == 1024 and t % (NA_WIN_R * GRID_W) == 0 and ctx.shape[1] % (GDN_TILE_CHUNKS * DN_CHUNK) == 0

    mod_rows = -(-(bsz + 1) // SUBLANE) * SUBLANE
    cc = jnp.zeros((mod_rows, d), F32).at[:bsz].set(c).at[bsz].set(c_ctx)
    mod = _ada_call(cc, w_ada, b_ada).reshape(depth, mod_rows, 1, 6 * d)
    lat_row = lambda bi: bi
    ctx_row = lambda bi: bsz

    w_in_p, w_ab = (a.astype(BF16) for a in _permute_w_in(w_in))
    rope = _rope_tables(t)
    par = jnp.zeros((depth, SUBLANE, LANE), F32)
    par = par.at[:, 0, :2 * DN_HEADS].set(dn_a_log.reshape(depth, -1)).at[:, 1, :2 * DN_HEADS].set(dn_dt_bias.reshape(depth, -1))
    s_zero = jnp.zeros((bsz, 2, DN_HEADS, DN_HEAD_DIM, DN_HEAD_DIM), F32)
    tm_lat = 1024 if t % 1024 == 0 else 512
    tm_ctx = ctx.shape[1]

    xc = ctx
    for l in range(depth):
        need_ctx = l < depth - 1
        g1 = norm1_g[l][None]
        g2 = norm2_g[l][None]
        wpa, wpb, wpc, wo = (a[l].astype(BF16) for a in (w_pa, w_pb, w_pc, w_o))
        wup, wdn = w_up[l].astype(BF16), w_down[l].astype(BF16)
        ng = dn_norm_g[l][None]
        p, ab = _inproj_call(x, g1, mod[l], lat_row, w_in_p[l], w_ab[l], tm_lat)
        pc, abc = _inproj_call(xc, g1, mod[l], ctx_row, w_in_p[l], w_ab[l], tm_ctx)
        ocf, ocb, s_ctx = _gdn_call(pc, abc, dn_conv_w[l], par[l], s_zero, None, True)
        olf, olb = _gdn_call(p, ab, dn_conv_w[l], par[l], s_ctx, rope, False)
        y_c = _na_call(p, pc, _na_bias_table(na_rpb[l]))
        x = _merge_call(olf, olb, p, y_c, x, mod[l], lat_row, wpa, wpb, wpc, wo, ng, sc_conv_w[l], 256)
        fg = final_norm_g[None] if l == depth - 1 else None
        x = _ffn_call(x, g2, mod[l], lat_row, wup, ffn_conv_w[l], ffn_conv_b[l][None], wdn, tm_lat, fg)
        if need_ctx:
            yc_c = _ctx_attn_call(pc)
            xc = _merge_call(ocf, ocb, pc, yc_c, xc, mod[l], ctx_row, wpa, wpb, wpc, wo, ng, sc_conv_w[l], tm_ctx)
            xc = _ffn_call(xc, g2, mod[l], ctx_row, wup, ffn_conv_w[l], ffn_conv_b[l][None], wdn, tm_ctx)
    return x
```

```python
import functools

import numpy as np
import jax
import jax.numpy as jnp
from jax import lax
from jax.experimental import pallas as pl
from jax.experimental.pallas import tpu as pltpu

F32 = jnp.float32
BF16 = jnp.bfloat16

GRID_W = 64
DN_HEADS = 4
DN_HEAD_DIM = 128
DN_WIDTH = DN_HEADS * DN_HEAD_DIM
DN_CHUNK = 64
GDN_TILE_CHUNKS = 4
SC_WIDTH = 512
NA_HEADS = 8
NA_HEAD_DIM = 64
NA_WIDTH = NA_HEADS * NA_HEAD_DIM
NA_WIN_R = 8
NA_WIN_C = 16
NA_ROWS_PER_ITER = 4
ROPE_BASE = 10000.0
EPS = 1e-6
NEG = -1e30

LANE = 128
SUBLANE = 8
HALO = SUBLANE
HALO_P = 2 * SUBLANE
VMEM_LIMIT = 48 * 1024 * 1024

P_COLS = 3 * DN_WIDTH + DN_WIDTH + 3 * SC_WIDTH + 3 * NA_WIDTH + 3 * 1024
P_TILE = 2048
Z_COL0 = 3 * DN_WIDTH
GATE_COL0 = P_COLS - 3 * 1024
NA_BLK0 = (3 * DN_WIDTH + DN_WIDTH + 3 * SC_WIDTH) // LANE


def _dot(a, b):
    return jnp.dot(a.astype(BF16), b.astype(BF16), preferred_element_type=F32)


def _dot_nt(a, b):
    return lax.dot_general(a.astype(BF16), b.astype(BF16), (((1,), (1,)), ((), ())), preferred_element_type=F32)


def _dot_tn(a, b):
    return lax.dot_general(a.astype(BF16), b.astype(BF16), (((0,), (0,)), ((), ())), preferred_element_type=F32)


def _sigmoid(x):
    return 1.0 / (1.0 + jnp.exp(-x))


def _silu(x):
    return x * _sigmoid(x)


def _softplus(x):
    return jnp.maximum(x, 0.0) + jnp.log1p(jnp.exp(-jnp.abs(x)))


def _rms(x, eps=EPS):
    return x * lax.rsqrt(jnp.mean(x * x, axis=-1, keepdims=True) + eps)


def _params(*sem):
    return pltpu.CompilerParams(dimension_semantics=sem, vmem_limit_bytes=VMEM_LIMIT)


def _dwconv3_rows(full, w_ref, n, halo=HALO):
    rows = full.shape[0]
    dn = pltpu.roll(full, 1, axis=0)
    up = pltpu.roll(full, rows - 1, axis=0)
    out = dn * w_ref[0:1, :] + full * w_ref[1:2, :] + up * w_ref[2:3, :]
    return out[halo:halo + n]


def _ada_kernel(c_ref, w_ref, b_ref, o_ref):
    o_ref[...] = _dot(_silu(c_ref[...]), w_ref[...]) + b_ref[...]


def _ada_call(cc, w_ada, b_ada):
    depth, d, n6 = w_ada.shape
    r = cc.shape[0]
    tn = 1536
    return pl.pallas_call(
        _ada_kernel,
        grid=(depth, n6 // tn),
        in_specs=[
            pl.BlockSpec((r, d), lambda l, j: (0, 0)),
            pl.BlockSpec((None, d, tn), lambda l, j: (l, 0, j)),
            pl.BlockSpec((None, 1, tn), lambda l, j: (l, 0, j)),
        ],
        out_specs=pl.BlockSpec((None, r, tn), lambda l, j: (l, 0, j)),
        out_shape=jax.ShapeDtypeStruct((depth, r, n6), F32),
        compiler_params=_params("parallel", "parallel"),
        name="ada_mod",
    )(cc, w_ada, b_ada.reshape(depth, 1, n6))


def _inproj_kernel(x_ref, g_ref, sh_ref, sc_ref, w_ref, wab_ref, o_ref, ab_ref, h_ref):
    j = pl.program_id(2)

    @pl.when(j == 0)
    def _():
        h = (_rms(x_ref[...]) * g_ref[...]) * (1.0 + sc_ref[...]) + sh_ref[...]
        h_ref[...] = h.astype(BF16)
        ab_ref[...] = jnp.dot(h_ref[...], wab_ref[...], preferred_element_type=F32)

    res = jnp.dot(h_ref[...], w_ref[...], preferred_element_type=F32)
    sg = _sigmoid(res)
    col = j * res.shape[1] + lax.broadcasted_iota(jnp.int32, (1, res.shape[1]), 1)
    is_z = (col >= Z_COL0) & (col < Z_COL0 + DN_WIDTH)
    out = jnp.where(col >= GATE_COL0, sg, jnp.where(is_z, res * sg, res))
    o_ref[...] = out.astype(o_ref.dtype)


def _inproj_call(x, g, mod, mrow, w, wab, tm):
    b, t, d = x.shape
    n = w.shape[1]
    tn = P_TILE
    return pl.pallas_call(
        _inproj_kernel,
        grid=(b, t // tm, n // tn),
        in_specs=[
            pl.BlockSpec((None, tm, d), lambda bi, i, j: (bi, i, 0)),
            pl.BlockSpec((1, d), lambda bi, i, j: (0, 0)),
            pl.BlockSpec((None, 1, d), lambda bi, i, j: (mrow(bi), 0, 0)),
            pl.BlockSpec((None, 1, d), lambda bi, i, j: (mrow(bi), 0, 1)),
            pl.BlockSpec((d, tn), lambda bi, i, j: (0, j)),
            pl.BlockSpec((d, LANE), lambda bi, i, j: (0, 0)),
        ],
        out_specs=[pl.BlockSpec((None, tm, tn), lambda bi, i, j: (bi, i, j)),
                   pl.BlockSpec((None, tm, LANE), lambda bi, i, j: (bi, i, 0))],
        out_shape=[jax.ShapeDtypeStruct((b, t, n), BF16), jax.ShapeDtypeStruct((b, t, LANE), F32)],
        scratch_shapes=[pltpu.VMEM((tm, d), BF16)],
        compiler_params=_params("parallel", "parallel", "arbitrary"),
        name="inproj",
    )(x, g, mod, mod, w, wab)


def _swap_pairs(x, even):
    n = x.shape[-1]
    return jnp.where(even, pltpu.roll(x, n - 1, axis=1), pltpu.roll(x, 1, axis=1))


def _gdn_kernel(*refs, nt, use_rope, emit_state):
    it = iter(refs)
    xs = [[next(it) for _ in range(4)] for _ in range(2)]
    cw_ref = next(it)
    par_ref = next(it)
    ropes = [[next(it), next(it)] for _ in range(2)] if use_rope else None
    s0_ref = next(it)
    o_refs = [next(it), next(it)]
    sout_ref = next(it) if emit_state else None
    s_scr, q_scr, k_scr, v_scr, uw_scr, qkd_scr = (next(it) for _ in range(6))

    c = DN_CHUNK
    nh = DN_HEADS
    hd = DN_HEAD_DIM
    nch = GDN_TILE_CHUNKS
    tr = nch * c
    cw = nh * c
    lc = c.bit_length() - 1
    i = pl.program_id(1)

    @pl.when(i == 0)
    def _():
        s_scr[...] = s0_ref[...]

    ri = lax.broadcasted_iota(jnp.int32, (c, cw), 0)
    li = lax.broadcasted_iota(jnp.int32, (c, cw), 1)
    jj = li & (c - 1)
    seg = li >> lc
    eye_cat = ri == jj
    bd_sq = jnp.where((lax.broadcasted_iota(jnp.int32, (cw, cw), 0) >> lc)
                      == (lax.broadcasted_iota(jnp.int32, (cw, cw), 1) >> lc), 1.0, 0.0).astype(BF16)
    bd_k = jnp.where((lax.broadcasted_iota(jnp.int32, (cw, nh * hd), 0) >> lc)
                     == (lax.broadcasted_iota(jnp.int32, (cw, nh * hd), 1) // hd), 1.0, 0.0).astype(BF16)
    even = (lax.broadcasted_iota(jnp.int32, (tr, hd), 1) % 2) == 0
    rowi = lax.broadcasted_iota(jnp.int32, (tr, LANE), 0) & (c - 1)
    scale = hd ** -0.5
    neg_a = -jnp.exp(par_ref[0:1, :])
    dt_b = par_ref[1:2, :]

    def col_bcast(x, j, width):
        return jnp.broadcast_to(x[:, j:j + 1], (x.shape[0], width))

    def colcat(x, base):
        out = col_bcast(x, base + nh - 1, cw)
        for h in range(nh - 2, -1, -1):
            out = jnp.where(seg == h, col_bcast(x, base + h, cw), out)
        return out

    def block_diag(m, mask):
        return jnp.concatenate([m.astype(BF16)] * nh, axis=0) * mask

    nar = []
    for d in range(2):
        x_ref, xp_ref, xn_ref, ab_ref = xs[d]
        tile = i if d == 0 else nt - 1 - i
        has_prev = (tile > 0).astype(F32)
        has_next = (tile < nt - 1).astype(F32)
        full = jnp.concatenate([xp_ref[...].astype(F32) * has_prev, x_ref[...].astype(F32),
                                xn_ref[...].astype(F32) * has_next], axis=0)
        act = _silu(_dwconv3_rows(full, cw_ref, tr, HALO_P))
        for h in range(nh):
            sl = slice(h * hd, (h + 1) * hd)
            qh = act[:, h * hd:(h + 1) * hd]
            kh = act[:, DN_WIDTH + h * hd:DN_WIDTH + (h + 1) * hd]
            qh = qh * lax.rsqrt(jnp.sum(qh * qh, axis=-1, keepdims=True) + EPS)
            kh = kh * lax.rsqrt(jnp.sum(kh * kh, axis=-1, keepdims=True) + EPS)
            if use_rope:
                cos = ropes[d][0][...]
                sin = ropes[d][1][...]
                qh = qh * cos + _swap_pairs(qh, even) * sin
                kh = kh * cos + _swap_pairs(kh, even) * sin
            q_scr[d, :, sl] = qh * scale
            k_scr[d, :, sl] = kh
        v_scr[d] = act[:, 2 * DN_WIDTH:]
        ab = ab_ref[...]
        gc = neg_a * _softplus(ab + dt_b)
        beta = pltpu.roll(_sigmoid(ab), LANE - 2 * nh, axis=1)
        sft = 1
        while sft < c:
            if d == 0:
                gc = gc + jnp.where(rowi >= sft, pltpu.roll(gc, sft, axis=0), 0.0)
            else:
                gc = gc + jnp.where(rowi < c - sft, pltpu.roll(gc, tr - sft, axis=0), 0.0)
            sft *= 2
        e1 = jnp.exp(gc)
        nar.append(dict(gc=gc, e1=e1, beta=beta, be=beta * e1))

    probs = [(d, ch) for ch in range(nch) for d in range(2)]
    st = {}
    for d, ch in probs:
        rs = slice(ch * c, (ch + 1) * c)
        base = d * nh
        incl = (ri >= jj) if d == 0 else (ri <= jj)
        gc_c = nar[d]["gc"][rs]
        gcc = colcat(gc_c, base)
        gcr = jnp.broadcast_to(jnp.sum(jnp.where(eye_cat, gcc, 0.0), axis=0, keepdims=True), (c, cw))
        decay = jnp.where(incl, jnp.exp(jnp.where(incl, gcc - gcr, 0.0)), 0.0)
        kf = k_scr[d, rs, :]
        prod = _dot_nt(jnp.concatenate([q_scr[d, rs, :], kf], axis=0), block_diag(kf, bd_k))
        qkd_scr[d, ch] = prod[:c] * decay
        amat = jnp.where(incl & (~eye_cat), colcat(nar[d]["beta"][rs], base) * prod[c:] * decay, 0.0)
        lvl1 = ((ri >> 1) == (jj >> 1))
        st[d, ch] = dict(amat=amat.astype(BF16), tinv=jnp.where(eye_cat, 1.0, 0.0) - jnp.where(lvl1, amat, 0.0))
    s = 2
    while s < c:
        ls = s.bit_length() - 1
        lvl = jnp.where(((ri >> (ls + 1)) == (jj >> (ls + 1))) & ((ri >> ls) != (jj >> ls)), 1.0, 0.0).astype(BF16)
        for key in probs:
            p = st[key]
            p["t_bf"] = p["tinv"].astype(BF16)
            p["x"] = _dot(p["t_bf"], block_diag(p["amat"] * lvl, bd_sq))
        for key in probs:
            p = st[key]
            p["tinv"] = p["tinv"] - _dot(p["x"], block_diag(p["t_bf"], bd_sq))
        s *= 2
    for d, ch in probs:
        rs = slice(ch * c, (ch + 1) * c)
        base = d * nh
        rhs = jnp.concatenate([
            jnp.concatenate([v_scr[d, rs, h * hd:(h + 1) * hd] * col_bcast(nar[d]["beta"][rs], base + h, hd),
                             k_scr[d, rs, h * hd:(h + 1) * hd] * col_bcast(nar[d]["be"][rs], base + h, hd)], axis=1)
            for h in range(nh)], axis=0)
        uw_scr[d, ch] = _dot(block_diag(st[d, ch]["tinv"], bd_sq), rhs)
    states = [[s_scr[d, h] for h in range(nh)] for d in range(2)]
    for step in range(nch):
        cur = [(0, step), (1, nch - 1 - step)]
        loc = {}
        for d, ch in cur:
            rs = slice(ch * c, (ch + 1) * c)
            base = d * nh
            gc_c = nar[d]["gc"][rs]
            gtot = gc_c[c - 1:c] if d == 0 else gc_c[0:1]
            e2 = jnp.exp(gtot - gc_c)
            r_ = []
            for h in range(nh):
                hs = slice(h * hd, (h + 1) * hd)
                w_h = uw_scr[d, ch, h * c:(h + 1) * c, hd:]
                qd_h = q_scr[d, rs, hs] * col_bcast(nar[d]["e1"][rs], base + h, hd)
                r_.append(_dot(jnp.concatenate([w_h, qd_h], axis=0), states[d][h]))
            loc[d] = dict(r=r_, e2=e2, gtot=gtot)
        for d, ch in cur:
            loc[d]["vn"] = [uw_scr[d, ch, h * c:(h + 1) * c, :hd] - loc[d]["r"][h][:c] for h in range(nh)]
        for d, ch in cur:
            rs = slice(ch * c, (ch + 1) * c)
            base = d * nh
            vn = loc[d]["vn"]
            o2 = _dot(block_diag(qkd_scr[d, ch], bd_sq), jnp.concatenate(vn, axis=0))
            for h in range(nh):
                hs = slice(h * hd, (h + 1) * hd)
                kd_h = k_scr[d, rs, hs] * col_bcast(loc[d]["e2"], base + h, hd)
                o_refs[d][rs, hs] = loc[d]["r"][h][c:] + o2[h * c:(h + 1) * c]
                eg = jnp.exp(jnp.broadcast_to(loc[d]["gtot"][0:1, base + h:base + h + 1], (hd, hd)))
                states[d][h] = states[d][h] * eg + _dot_tn(kd_h, vn[h])
    for d in range(2):
        for h in range(nh):
            s_scr[d, h] = states[d][h]

    if emit_state:
        @pl.when(i == nt - 1)
        def _():
            sout_ref[...] = s_scr[...]


def _gdn_call(p, ab, conv_w, par, s0, rope, emit_state):
    b, t, _ = p.shape
    c = DN_CHUNK
    tr = GDN_TILE_CHUNKS * c
    nt = t // tr
    hb = tr // HALO_P
    nhb = t // HALO_P
    qkv_w = 3 * DN_WIDTH

    def tile_of(d, i):
        return i if d == 0 else nt - 1 - i

    in_specs = []
    args = []
    for d in range(2):
        in_specs += [
            pl.BlockSpec((None, tr, qkv_w), lambda bi, i, d=d: (bi, tile_of(d, i), 0)),
            pl.BlockSpec((None, HALO_P, qkv_w), lambda bi, i, d=d: (bi, jnp.maximum(tile_of(d, i) * hb - 1, 0), 0)),
            pl.BlockSpec((None, HALO_P, qkv_w), lambda bi, i, d=d: (bi, jnp.minimum(tile_of(d, i) * hb + hb, nhb - 1), 0)),
            pl.BlockSpec((None, tr, LANE), lambda bi, i, d=d: (bi, tile_of(d, i), 0)),
        ]
        args += [p, p, p, ab]
    in_specs += [pl.BlockSpec(conv_w.shape, lambda bi, i: (0, 0)), pl.BlockSpec(par.shape, lambda bi, i: (0, 0))]
    args += [conv_w, par]
    if rope is not None:
        for d in range(2):
            in_specs += [pl.BlockSpec((tr, DN_HEAD_DIM), lambda bi, i, d=d: (tile_of(d, i), 0))] * 2
            args += [rope[0], rope[1]]
    sshape = (2, DN_HEADS, DN_HEAD_DIM, DN_HEAD_DIM)
    in_specs += [pl.BlockSpec((None,) + sshape, lambda bi, i: (bi, 0, 0, 0, 0))]
    args += [s0]
    out_specs = [
        pl.BlockSpec((None, tr, DN_WIDTH), lambda bi, i: (bi, i, 0)),
        pl.BlockSpec((None, tr, DN_WIDTH), lambda bi, i: (bi, nt - 1 - i, 0)),
    ]
    out_shape = [jax.ShapeDtypeStruct((b, t, DN_WIDTH), F32)] * 2
    if emit_state:
        out_specs += [pl.BlockSpec((None,) + sshape, lambda bi, i: (bi, 0, 0, 0, 0))]
        out_shape += [jax.ShapeDtypeStruct((b,) + sshape, F32)]
    cw = DN_HEADS * c
    scratch = [
        pltpu.VMEM(sshape, F32),
        pltpu.VMEM((2, tr, DN_WIDTH), F32), pltpu.VMEM((2, tr, DN_WIDTH), F32), pltpu.VMEM((2, tr, DN_WIDTH), F32),
        pltpu.VMEM((2, GDN_TILE_CHUNKS, cw, 2 * DN_HEAD_DIM), F32),
        pltpu.VMEM((2, GDN_TILE_CHUNKS, c, cw), F32),
    ]
    return pl.pallas_call(
        functools.partial(_gdn_kernel, nt=nt, use_rope=rope is not None, emit_state=emit_state),
        grid=(b, nt),
        in_specs=in_specs,
        out_specs=out_specs,
        out_shape=out_shape,
        scratch_shapes=scratch,
        compiler_params=_params("parallel", "arbitrary"),
        name="gdn_ctx" if emit_state else "gdn_lat",
    )(*args)


def _na_kernel(q_ref, k_ref, v_ref, kc_ref, vc_ref, bias_ref, o_ref, *, rows):
    w = GRID_W
    nl = NA_WIN_R * w
    nr = NA_ROWS_PER_ITER
    scale = NA_HEAD_DIM ** -0.5
    kc = kc_ref[...].astype(BF16)
    vc = vc_ref[...].astype(BF16)

    def stack_heads(q2):
        sel = lax.broadcasted_iota(jnp.int32, q2.shape, 1) < NA_HEAD_DIM
        zero = jnp.zeros_like(q2)
        return jnp.concatenate([jnp.where(sel, q2, zero), jnp.where(sel, zero, q2)], axis=0)

    def unstack_heads(x):
        n = x.shape[0] // 2
        sel = lax.broadcasted_iota(jnp.int32, (n, LANE), 1) < NA_HEAD_DIM
        return jnp.where(sel, jnp.broadcast_to(x[:n], (n, LANE)), jnp.broadcast_to(x[n:], (n, LANE)))

    def row_block(it, carry):
        rr = [it * nr + u for u in range(nr)]
        rss = [jnp.clip(r - NA_WIN_R // 2, 0, rows - NA_WIN_R) for r in rr]
        rsl = [pl.ds(pl.multiple_of(r * w, w), w) for r in rr]
        wsl = [pl.ds(pl.multiple_of(rs * w, w), nl) for rs in rss]
        q_all = q_ref[pl.ds(pl.multiple_of(it * (nr * w), nr * w), nr * w), :] * scale
        s_c = _dot_nt(stack_heads(q_all), kc)
        s = [_dot_nt(stack_heads(q_all[u * w:(u + 1) * w]), k_ref[wsl[u], :]) + bias_ref[rss[u] - rr[u] + NA_WIN_R - 1]
             for u in range(nr)]
        m_c = jnp.max(s_c, axis=-1, keepdims=True)
        m_l = [jnp.max(x, axis=-1, keepdims=True) for x in s]
        p_c = jnp.exp(s_c - m_c)
        p = [jnp.exp(x - m) for x, m in zip(s, m_l)]
        o_c = unstack_heads(_dot(p_c, vc))
        o_l = [_dot(p[u], v_ref[wsl[u], :]) for u in range(nr)]
        l_c = unstack_heads(jnp.sum(p_c, axis=-1, keepdims=True))
        m_c = unstack_heads(m_c)
        for u in range(nr):
            usl = slice(u * w, (u + 1) * w)
            l_u = unstack_heads(jnp.sum(p[u], axis=-1, keepdims=True))
            m_u = unstack_heads(m_l[u])
            m = jnp.maximum(m_u, m_c[usl])
            a_l = jnp.exp(m_u - m)
            a_c = jnp.exp(m_c[usl] - m)
            o_ref[rsl[u], :] = (unstack_heads(o_l[u]) * a_l + o_c[usl] * a_c) / (l_u * a_l + l_c[usl] * a_c)
        return carry

    lax.fori_loop(0, rows // nr, row_block, 0)


def _na_call(p, pc, bias):
    b, t, _ = p.shape
    l = pc.shape[1]
    rows = t // GRID_W
    npair = NA_HEADS // 2
    return pl.pallas_call(
        functools.partial(_na_kernel, rows=rows),
        grid=(npair, b),
        in_specs=[
            pl.BlockSpec((None, t, LANE), lambda j, bi: (bi, 0, NA_BLK0 + j)),
            pl.BlockSpec((None, t, LANE), lambda j, bi: (bi, 0, NA_BLK0 + npair + j)),
            pl.BlockSpec((None, t, LANE), lambda j, bi: (bi, 0, NA_BLK0 + 2 * npair + j)),
            pl.BlockSpec((None, l, LANE), lambda j, bi: (bi, 0, NA_BLK0 + npair + j)),
            pl.BlockSpec((None, l, LANE), lambda j, bi: (bi, 0, NA_BLK0 + 2 * npair + j)),
            pl.BlockSpec((None,) + bias.shape[1:], lambda j, bi: (j, 0, 0, 0)),
        ],
        out_specs=pl.BlockSpec((None, t, LANE), lambda j, bi: (bi, 0, j)),
        out_shape=jax.ShapeDtypeStruct((b, t, NA_WIDTH), F32),
        compiler_params=_params("parallel", "parallel"),
        name="na_lat",
    )(p, p, p, pc, pc, bias)


def _ctx_attn_kernel(q_ref, k_ref, v_ref, o_ref):
    scale = NA_HEAD_DIM ** -0.5
    q2 = q_ref[...] * scale
    k2 = k_ref[...]
    v2 = v_ref[...]
    head0 = lax.broadcasted_iota(jnp.int32, q2.shape, 1) < NA_HEAD_DIM
    out = None
    for hh in range(2):
        sel = head0 if hh == 0 else ~head0
        s = _dot_nt(jnp.where(sel, q2, jnp.zeros_like(q2)), k2)
        p = jnp.exp(s - jnp.max(s, axis=-1, keepdims=True))
        o = _dot(p, v2) / jnp.sum(p, axis=-1, keepdims=True)
        out = o if out is None else jnp.where(sel, o, out)
    o_ref[...] = out


def _ctx_attn_call(pc):
    b, l, _ = pc.shape
    npair = NA_HEADS // 2
    return pl.pallas_call(
        _ctx_attn_kernel,
        grid=(b, npair),
        in_specs=[
            pl.BlockSpec((None, l, LANE), lambda bi, j: (bi, 0, NA_BLK0 + j)),
            pl.BlockSpec((None, l, LANE), lambda bi, j: (bi, 0, NA_BLK0 + npair + j)),
            pl.BlockSpec((None, l, LANE), lambda bi, j: (bi, 0, NA_BLK0 + 2 * npair + j)),
        ],
        out_specs=pl.BlockSpec((None, l, LANE), lambda bi, j: (bi, 0, j)),
        out_shape=jax.ShapeDtypeStruct((b, l, NA_WIDTH), F32),
        compiler_params=_params("parallel", "parallel"),
        name="ctx_attn",
    )(pc, pc, pc)


def _merge_kernel(of_ref, ob_ref, z_ref, bg_ref, cg_ref, hh_ref, cgp_ref, cgn_ref, hhp_ref, hhn_ref, yc_ref,
                  ga_ref, gb_ref, gcg_ref, x_ref, gt_ref, wpa_ref, wpb_ref, wpc_ref, wo_ref, ng_ref, scw_ref,
                  o_ref, *, nt):
    i = pl.program_id(1)
    tm = x_ref.shape[0]
    o = of_ref[...] + ob_ref[...]
    sz = z_ref[...].astype(F32)
    ya = []
    for h in range(DN_HEADS):
        sl = slice(h * DN_HEAD_DIM, (h + 1) * DN_HEAD_DIM)
        ya.append(_rms(o[:, sl]) * ng_ref[...] * sz[:, sl])
    y_a = jnp.concatenate(ya, axis=1)
    has_prev = (i > 0).astype(F32)
    has_next = (i < nt - 1).astype(F32)
    f32 = lambda r: r[...].astype(F32)
    full = jnp.concatenate([f32(cgp_ref) * f32(hhp_ref) * has_prev, f32(cg_ref) * f32(hh_ref),
                            f32(cgn_ref) * f32(hhn_ref) * has_next], axis=0)
    y_b = f32(bg_ref) * _dwconv3_rows(full, scw_ref, tm, HALO_P)
    y = (f32(ga_ref) * _dot(y_a, wpa_ref[...]) + f32(gb_ref) * _dot(y_b, wpb_ref[...])
         + f32(gcg_ref) * _dot(yc_ref[...], wpc_ref[...]))
    o_ref[...] = x_ref[...] + gt_ref[...] * _dot(y, wo_ref[...])


def _merge_call(of, ob, p, yc, x, mod, mrow, wpa, wpb, wpc, wo, ng, scw, tm):
    b, t, d = x.shape
    nt = t // tm
    hb = tm // HALO_P
    nhb = t // HALO_P
    w5 = SC_WIDTH

    def tile(width, col):
        return pl.BlockSpec((None, tm, width), lambda bi, i: (bi, i, col))

    def prev(col):
        return pl.BlockSpec((None, HALO_P, w5), lambda bi, i: (bi, jnp.maximum(i * hb - 1, 0), col))

    def nxt(col):
        return pl.BlockSpec((None, HALO_P, w5), lambda bi, i: (bi, jnp.minimum(i * hb + hb, nhb - 1), col))

    def full(a):
        return pl.BlockSpec(a.shape, lambda bi, i: (0,) * a.ndim)

    in_specs = [
        tile(DN_WIDTH, 0), tile(DN_WIDTH, 0),
        tile(w5, 3), tile(w5, 4), tile(w5, 5), tile(w5, 6),
        prev(5), nxt(5), prev(6), nxt(6),
        tile(NA_WIDTH, 0),
        tile(d, 5), tile(d, 6), tile(d, 7),
        tile(d, 0),
        pl.BlockSpec((None, 1, d), lambda bi, i: (mrow(bi), 0, 2)),
        full(wpa), full(wpb), full(wpc), full(wo), full(ng), full(scw),
    ]
    return pl.pallas_call(
        functools.partial(_merge_kernel, nt=nt),
        grid=(b, nt),
        in_specs=in_specs,
        out_specs=pl.BlockSpec((None, tm, d), lambda bi, i: (bi, i, 0)),
        out_shape=jax.ShapeDtypeStruct((b, t, d), F32),
        compiler_params=_params("parallel", "parallel"),
        name="merge",
    )(of, ob, p, p, p, p, p, p, p, p, yc, p, p, p, x, mod, wpa, wpb, wpc, wo, ng, scw)


def _ffn_kernel(*refs, nt, nk, fc, final):
    it = iter(refs)
    x_ref, xp_ref, xn_ref, g_ref, sh_ref, sc_ref, gt_ref = (next(it) for _ in range(7))
    wup_ref, cw_ref, cb_ref, wd_ref = (next(it) for _ in range(4))
    fg_ref = next(it) if final else None
    o_ref, h_scr, acc_scr = next(it), next(it), next(it)
    i = pl.program_id(1)
    tm = x_ref.shape[0]
    dff = nk * fc

    xfull = jnp.concatenate([xp_ref[...], x_ref[...], xn_ref[...]], axis=0)
    h = (_rms(xfull) * g_ref[...]) * (1.0 + sc_ref[...]) + sh_ref[...]
    h_scr[...] = h.astype(BF16)

    row = lax.broadcasted_iota(jnp.int32, (tm + 2 * HALO, 1), 0)
    keep = ((row >= HALO) | (i > 0)) & ((row < tm + HALO) | (i < nt - 1))

    def up_proj(k):
        return [jnp.where(keep, jnp.dot(h_scr[...], wup_ref[:, c0:c0 + fc], preferred_element_type=F32), 0.0)
                for c0 in (k * fc, dff + k * fc)]

    ups = up_proj(0)
    for k in range(nk):
        nxt = up_proj(k + 1) if k + 1 < nk else None
        a = _dwconv3_rows(ups[0], cw_ref[:, k * fc:(k + 1) * fc], tm) + cb_ref[:, k * fc:(k + 1) * fc]
        bb = (_dwconv3_rows(ups[1], cw_ref[:, dff + k * fc:dff + (k + 1) * fc], tm)
              + cb_ref[:, dff + k * fc:dff + (k + 1) * fc])
        contrib = _dot(_silu(a) * bb, wd_ref[k * fc:(k + 1) * fc, :])
        if k == 0:
            acc_scr[...] = contrib
        else:
            acc_scr[...] += contrib
        ups = nxt

    out = x_ref[...] + gt_ref[...] * acc_scr[...]
    if final:
        out = _rms(out) * fg_ref[...]
    o_ref[...] = out


def _ffn_call(x, g, mod, mrow, w_up, cw, cb, w_down, tm, final_g=None):
    b, t, d = x.shape
    dff = w_down.shape[0]
    fc = 256
    nk = dff // fc
    nt = t // tm
    hb = tm // HALO
    nhb = t // HALO
    final = final_g is not None

    def resident(a):
        return pl.BlockSpec(a.shape, lambda bi, i: (0,) * a.ndim, pipeline_mode=pl.Buffered(1))

    in_specs = [
        pl.BlockSpec((None, tm, d), lambda bi, i: (bi, i, 0)),
        pl.BlockSpec((None, HALO, d), lambda bi, i: (bi, jnp.maximum(i * hb - 1, 0), 0)),
        pl.BlockSpec((None, HALO, d), lambda bi, i: (bi, jnp.minimum(i * hb + hb, nhb - 1), 0)),
        pl.BlockSpec((1, d), lambda bi, i: (0, 0)),
        pl.BlockSpec((None, 1, d), lambda bi, i: (mrow(bi), 0, 3)),
        pl.BlockSpec((None, 1, d), lambda bi, i: (mrow(bi), 0, 4)),
        pl.BlockSpec((None, 1, d), lambda bi, i: (mrow(bi), 0, 5)),
        resident(w_up), resident(cw), resident(cb), resident(w_down),
    ]
    args = [x, x, x, g, mod, mod, mod, w_up, cw, cb, w_down]
    if final:
        in_specs.append(pl.BlockSpec((1, d), lambda bi, i: (0, 0)))
        args.append(final_g)
    return pl.pallas_call(
        functools.partial(_ffn_kernel, nt=nt, nk=nk, fc=fc, final=final),
        grid=(b, nt),
        in_specs=in_specs,
        out_specs=pl.BlockSpec((None, tm, d), lambda bi, i: (bi, i, 0)),
        out_shape=jax.ShapeDtypeStruct((b, t, d), F32),
        scratch_shapes=[pltpu.VMEM((tm + 2 * HALO, d), BF16), pltpu.VMEM((tm, d), F32)],
        compiler_params=_params("parallel", "parallel"),
        name="ffn_final" if final else "ffn",
    )(*args)


def _rope_tables(t):
    tok = jnp.arange(t, dtype=jnp.int32)
    rows = (tok // GRID_W).astype(F32)
    cols = (tok % GRID_W).astype(F32)
    nf = DN_HEAD_DIM // 4
    inv = ROPE_BASE ** (-jnp.arange(nf, dtype=F32) / nf)
    ang = jnp.concatenate([rows[:, None] * inv, cols[:, None] * inv], axis=-1)
    cos = jnp.repeat(jnp.cos(ang), 2, axis=-1)
    sin = jnp.repeat(jnp.sin(ang), 2, axis=-1)
    sign = jnp.where(jnp.arange(DN_HEAD_DIM) % 2 == 0, -1.0, 1.0).astype(F32)
    return cos, sin * sign


def _na_bias_table(rpb):
    w = GRID_W
    qc = np.arange(w)[:, None]
    kc = np.arange(w)[None, :]
    cstart = np.clip(qc - NA_WIN_C // 2, 0, w - NA_WIN_C)
    valid = (kc >= cstart) & (kc < cstart + NA_WIN_C)
    cidx = np.clip(kc - qc + NA_WIN_C - 1, 0, 2 * NA_WIN_C - 2)
    tbl = jnp.where(jnp.asarray(valid), rpb[:, :, cidx], NEG)
    didx = np.arange(NA_WIN_R)[:, None] + np.arange(NA_WIN_R)[None, :]
    tbl = tbl[:, didx]
    tbl = tbl.transpose(0, 1, 3, 2, 4).reshape(NA_HEADS // 2, 2, NA_WIN_R, w, NA_WIN_R * w)
    return tbl.transpose(0, 2, 1, 3, 4).reshape(NA_HEADS // 2, NA_WIN_R, 2 * w, NA_WIN_R * w)


def _permute_w_in(w_in):
    s = np.cumsum([0, 3 * DN_WIDTH, DN_WIDTH, 2 * DN_HEADS, 2 * DN_HEADS, 3 * SC_WIDTH, 3 * NA_WIDTH, 3 * 1024])
    qkv, z, a, b, sc, na, gates = (w_in[..., s[j]:s[j + 1]] for j in range(7))
    pad = jnp.zeros(w_in.shape[:-1] + (LANE - 4 * DN_HEADS,), w_in.dtype)
    return jnp.concatenate([qkv, z, sc, na, gates], axis=-1), jnp.concatenate([a, b, pad], axis=-1)


def kernel(x, c, ctx, c_ctx, norm1_g, norm2_g, w_ada, b_ada, w_in, dn_conv_w, dn_a_log, dn_dt_bias, dn_norm_g, sc_conv_w, na_rpb, w_pa, w_pb, w_pc, w_o, w_up, ffn_conv_w, ffn_conv_b, w_down, final_norm_g):
    bsz, t, d = x.shape
    depth = w_in.shape[0]
    assert d == 1024 and t % (NA_WIN_R * GRID_W) == 0 and ctx.shape[1] % (GDN_TILE_CHUNKS * DN_CHUNK) == 0

    mod_rows = -(-(bsz + 1) // SUBLANE) * SUBLANE
    cc = jnp.zeros((mod_rows, d), F32).at[:bsz].set(c).at[bsz].set(c_ctx)
    mod = _ada_call(cc, w_ada, b_ada).reshape(depth, mod_rows, 1, 6 * d)
    lat_row = lambda bi: bi
    ctx_row = lambda bi: bsz

    w_in_p, w_ab = (a.astype(BF16) for a in _permute_w_in(w_in))
    rope = _rope_tables(t)
    par = jnp.zeros((depth, SUBLANE, LANE), F32)
    par = par.at[:, 0, :2 * DN_HEADS].set(dn_a_log.reshape(depth, -1)).at[:, 1, :2 * DN_HEADS].set(dn_dt_bias.reshape(depth, -1))
    s_zero = jnp.zeros((bsz, 2, DN_HEADS, DN_HEAD_DIM, DN_HEAD_DIM), F32)
    tm_lat = 1024 if t % 1024 == 0 else 512
    tm_ctx = ctx.shape[1]

    xc = ctx
    for l in range(depth):
        need_ctx = l < depth - 1
        g1 = norm1_g[l][None]
        g2 = norm2_g[l][None]
        wpa, wpb, wpc, wo = (a[l].astype(BF16) for a in (w_pa, w_pb, w_pc, w_o))
        wup, wdn = w_up[l].astype(BF16), w_down[l].astype(BF16)
        ng = dn_norm_g[l][None]
        p, ab = _inproj_call(x, g1, mod[l], lat_row, w_in_p[l], w_ab[l], tm_lat)
        pc, abc = _inproj_call(xc, g1, mod[l], ctx_row, w_in_p[l], w_ab[l], tm_ctx)
        ocf, ocb, s_ctx = _gdn_call(pc, abc, dn_conv_w[l], par[l], s_zero, None, True)
        olf, olb = _gdn_call(p, ab, dn_conv_w[l], par[l], s_ctx, rope, False)
        y_c = _na_call(p, pc, _na_bias_table(na_rpb[l]))
        x = _merge_call(olf, olb, p, y_c, x, mod[l], lat_row, wpa, wpb, wpc, wo, ng, sc_conv_w[l], 512)
        fg = final_norm_g[None] if l == depth - 1 else None
        x = _ffn_call(x, g2, mod[l], lat_row, wup, ffn_conv_w[l], ffn_conv_b[l][None], wdn, tm_lat, fg)
        if need_ctx:
            yc_c = _ctx_attn_call(pc)
            xc = _merge_call(ocf, ocb, pc, yc_c, xc, mod[l], ctx_row, wpa, wpb, wpc, wo, ng, sc_conv_w[l], tm_ctx)
            xc = _ffn_call(xc, g2, mod[l], ctx_row, wup, ffn_conv_w[l], ffn_conv_b[l][None], wdn, tm_ctx)
    return x
```

```python
import functools

import numpy as np
import jax
import jax.numpy as jnp
from jax import lax
from jax.experimental import pallas as pl
from jax.experimental.pallas import tpu as pltpu

F32 = jnp.float32
BF16 = jnp.bfloat16

GRID_W = 64
DN_HEADS = 4
DN_HEAD_DIM = 128
DN_WIDTH = DN_HEADS * DN_HEAD_DIM
DN_CHUNK = 64
GDN_TILE_CHUNKS = 4
FFN_DOWN_GROUP = 4
SC_WIDTH = 512
NA_HEADS = 8
NA_HEAD_DIM = 64
NA_WIDTH = NA_HEADS * NA_HEAD_DIM
NA_WIN_R = 8
NA_WIN_C = 16
NA_ROWS_PER_ITER = 4
ROPE_BASE = 10000.0
EPS = 1e-6
NEG = -1e30

LANE = 128
SUBLANE = 8
HALO = SUBLANE
HALO_P = 2 * SUBLANE
VMEM_LIMIT = 48 * 1024 * 1024

P_COLS = 3 * DN_WIDTH + DN_WIDTH + 3 * SC_WIDTH + 3 * NA_WIDTH + 3 * 1024
P_TILE = 2048
Z_COL0 = 3 * DN_WIDTH
GATE_COL0 = P_COLS - 3 * 1024
NA_BLK0 = (3 * DN_WIDTH + DN_WIDTH + 3 * SC_WIDTH) // LANE


def _dot(a, b):
    return jnp.dot(a.astype(BF16), b.astype(BF16), preferred_element_type=F32)


def _dot_nt(a, b):
    return lax.dot_general(a.astype(BF16), b.astype(BF16), (((1,), (1,)), ((), ())), preferred_element_type=F32)


def _dot_tn(a, b):
    return lax.dot_general(a.astype(BF16), b.astype(BF16), (((0,), (0,)), ((), ())), preferred_element_type=F32)


def _sigmoid(x):
    return 1.0 / (1.0 + jnp.exp(-x))


def _silu(x):
    return x * _sigmoid(x)


def _softplus(x):
    return jnp.maximum(x, 0.0) + jnp.log1p(jnp.exp(-jnp.abs(x)))


def _rms(x, eps=EPS):
    return x * lax.rsqrt(jnp.mean(x * x, axis=-1, keepdims=True) + eps)


def _params(*sem):
    return pltpu.CompilerParams(dimension_semantics=sem, vmem_limit_bytes=VMEM_LIMIT)


def _dwconv3_rows(full, w_ref, n, halo=HALO):
    rows = full.shape[0]
    dn = pltpu.roll(full, 1, axis=0)
    up = pltpu.roll(full, rows - 1, axis=0)
    out = dn * w_ref[0:1, :] + full * w_ref[1:2, :] + up * w_ref[2:3, :]
    return out[halo:halo + n]


def _ada_kernel(c_ref, w_ref, b_ref, o_ref):
    o_ref[...] = _dot(_silu(c_ref[...]), w_ref[...]) + b_ref[...]


def _ada_call(cc, w_ada, b_ada):
    depth, d, n6 = w_ada.shape
    r = cc.shape[0]
    tn = 1536
    return pl.pallas_call(
        _ada_kernel,
        grid=(depth, n6 // tn),
        in_specs=[
            pl.BlockSpec((r, d), lambda l, j: (0, 0)),
            pl.BlockSpec((None, d, tn), lambda l, j: (l, 0, j)),
            pl.BlockSpec((None, 1, tn), lambda l, j: (l, 0, j)),
        ],
        out_specs=pl.BlockSpec((None, r, tn), lambda l, j: (l, 0, j)),
        out_shape=jax.ShapeDtypeStruct((depth, r, n6), F32),
        compiler_params=_params("parallel", "parallel"),
        name="ada_mod",
    )(cc, w_ada, b_ada.reshape(depth, 1, n6))


def _inproj_kernel(x_ref, g_ref, sh_ref, sc_ref, w_ref, wab_ref, o_ref, ab_ref, h_ref):
    j = pl.program_id(2)

    @pl.when(j == 0)
    def _():
        h = (_rms(x_ref[...]) * g_ref[...]) * (1.0 + sc_ref[...]) + sh_ref[...]
        h_ref[...] = h.astype(BF16)
        ab_ref[...] = jnp.dot(h_ref[...], wab_ref[...], preferred_element_type=F32)

    res = jnp.dot(h_ref[...], w_ref[...], preferred_element_type=F32)
    sg = _sigmoid(res)
    col = j * res.shape[1] + lax.broadcasted_iota(jnp.int32, (1, res.shape[1]), 1)
    is_z = (col >= Z_COL0) & (col < Z_COL0 + DN_WIDTH)
    out = jnp.where(col >= GATE_COL0, sg, jnp.where(is_z, res * sg, res))
    o_ref[...] = out.astype(o_ref.dtype)


def _inproj_call(x, g, mod, mrow, w, wab, tm):
    b, t, d = x.shape
    n = w.shape[1]
    tn = P_TILE
    return pl.pallas_call(
        _inproj_kernel,
        grid=(b, t // tm, n // tn),
        in_specs=[
            pl.BlockSpec((None, tm, d), lambda bi, i, j: (bi, i, 0)),
            pl.BlockSpec((1, d), lambda bi, i, j: (0, 0)),
            pl.BlockSpec((None, 1, d), lambda bi, i, j: (mrow(bi), 0, 0)),
            pl.BlockSpec((None, 1, d), lambda bi, i, j: (mrow(bi), 0, 1)),
            pl.BlockSpec((d, tn), lambda bi, i, j: (0, j)),
            pl.BlockSpec((d, LANE), lambda bi, i, j: (0, 0)),
        ],
        out_specs=[pl.BlockSpec((None, tm, tn), lambda bi, i, j: (bi, i, j)),
                   pl.BlockSpec((None, tm, LANE), lambda bi, i, j: (bi, i, 0))],
        out_shape=[jax.ShapeDtypeStruct((b, t, n), BF16), jax.ShapeDtypeStruct((b, t, LANE), F32)],
        scratch_shapes=[pltpu.VMEM((tm, d), BF16)],
        compiler_params=_params("parallel", "parallel", "arbitrary"),
        name="inproj",
    )(x, g, mod, mod, w, wab)


def _swap_pairs(x, even):
    n = x.shape[-1]
    return jnp.where(even, pltpu.roll(x, n - 1, axis=1), pltpu.roll(x, 1, axis=1))


def _gdn_prep_kernel(*refs, nt, use_rope):
    x_ref, xp_ref, xn_ref, cw_ref = refs[:4]
    cos_ref, sin_ref = refs[4:6] if use_rope else (None, None)
    o_ref = refs[-1]
    i = pl.program_id(1)
    tr = x_ref.shape[0]
    hd = DN_HEAD_DIM
    has_prev = (i > 0).astype(F32)
    has_next = (i < nt - 1).astype(F32)
    full = jnp.concatenate([xp_ref[...].astype(F32) * has_prev, x_ref[...].astype(F32),
                            xn_ref[...].astype(F32) * has_next], axis=0)
    act = _silu(_dwconv3_rows(full, cw_ref, tr, HALO_P))
    even = (lax.broadcasted_iota(jnp.int32, (tr, hd), 1) % 2) == 0
    for off, scale in ((0, hd ** -0.5), (DN_WIDTH, 1.0)):
        for h in range(DN_HEADS):
            sl = slice(off + h * hd, off + (h + 1) * hd)
            xh = act[:, sl]
            xh = xh * lax.rsqrt(jnp.sum(xh * xh, axis=-1, keepdims=True) + EPS)
            if use_rope:
                xh = xh * cos_ref[...] + _swap_pairs(xh, even) * sin_ref[...]
            o_ref[:, sl] = (xh * scale).astype(o_ref.dtype)
    o_ref[:, 2 * DN_WIDTH:] = act[:, 2 * DN_WIDTH:].astype(o_ref.dtype)


def _gdn_prep_call(p, conv_w, rope):
    b, t, _ = p.shape
    tr = GDN_TILE_CHUNKS * DN_CHUNK
    nt = t // tr
    hb = tr // HALO_P
    nhb = t // HALO_P
    qkv_w = 3 * DN_WIDTH
    in_specs = [
        pl.BlockSpec((None, tr, qkv_w), lambda bi, i: (bi, i, 0)),
        pl.BlockSpec((None, HALO_P, qkv_w), lambda bi, i: (bi, jnp.maximum(i * hb - 1, 0), 0)),
        pl.BlockSpec((None, HALO_P, qkv_w), lambda bi, i: (bi, jnp.minimum(i * hb + hb, nhb - 1), 0)),
        pl.BlockSpec(conv_w.shape, lambda bi, i: (0, 0)),
    ]
    args = [p, p, p, conv_w]
    if rope is not None:
        in_specs += [pl.BlockSpec((tr, DN_HEAD_DIM), lambda bi, i: (i, 0))] * 2
        args += [rope[0], rope[1]]
    return pl.pallas_call(
        functools.partial(_gdn_prep_kernel, nt=nt, use_rope=rope is not None),
        grid=(b, nt),
        in_specs=in_specs,
        out_specs=pl.BlockSpec((None, tr, qkv_w), lambda bi, i: (bi, i, 0)),
        out_shape=jax.ShapeDtypeStruct((b, t, qkv_w), p.dtype),
        compiler_params=_params("parallel", "parallel"),
        name="gdn_prep",
    )(*args)


def _gdn_kernel(*refs, nt, emit_state):
    it = iter(refs)
    xs = [[next(it) for _ in range(2)] for _ in range(2)]
    par_ref = next(it)
    s0_ref = next(it)
    o_refs = [next(it), next(it)]
    sout_ref = next(it) if emit_state else None
    s_scr, uw_scr, qkd_scr = (next(it) for _ in range(3))

    c = DN_CHUNK
    nh = DN_HEADS
    hd = DN_HEAD_DIM
    nch = GDN_TILE_CHUNKS
    tr = nch * c
    cw = nh * c
    lc = c.bit_length() - 1
    i = pl.program_id(1)

    @pl.when(i == 0)
    def _():
        s_scr[...] = s0_ref[...]

    ri = lax.broadcasted_iota(jnp.int32, (c, cw), 0)
    li = lax.broadcasted_iota(jnp.int32, (c, cw), 1)
    jj = li & (c - 1)
    seg = li >> lc
    eye_cat = ri == jj
    bd_sq = jnp.where((lax.broadcasted_iota(jnp.int32, (cw, cw), 0) >> lc)
                      == (lax.broadcasted_iota(jnp.int32, (cw, cw), 1) >> lc), 1.0, 0.0).astype(BF16)
    bd_k = jnp.where((lax.broadcasted_iota(jnp.int32, (cw, nh * hd), 0) >> lc)
                     == (lax.broadcasted_iota(jnp.int32, (cw, nh * hd), 1) // hd), 1.0, 0.0).astype(BF16)
    rowi = lax.broadcasted_iota(jnp.int32, (tr, LANE), 0) & (c - 1)
    neg_a = -jnp.exp(par_ref[0:1, :])
    dt_b = par_ref[1:2, :]

    def col_bcast(x, j, width):
        return jnp.broadcast_to(x[:, j:j + 1], (x.shape[0], width))

    def colcat(x, base):
        out = col_bcast(x, base + nh - 1, cw)
        for h in range(nh - 2, -1, -1):
            out = jnp.where(seg == h, col_bcast(x, base + h, cw), out)
        return out

    def block_diag(m, mask):
        return jnp.concatenate([m.astype(BF16)] * nh, axis=0) * mask

    nar = []
    for d in range(2):
        ab = xs[d][1][...]
        gc = neg_a * _softplus(ab + dt_b)
        beta = pltpu.roll(_sigmoid(ab), LANE - 2 * nh, axis=1)
        sft = 1
        while sft < c:
            if d == 0:
                gc = gc + jnp.where(rowi >= sft, pltpu.roll(gc, sft, axis=0), 0.0)
            else:
                gc = gc + jnp.where(rowi < c - sft, pltpu.roll(gc, tr - sft, axis=0), 0.0)
            sft *= 2
        e1 = jnp.exp(gc)
        nar.append(dict(gc=gc, e1=e1, beta=beta, be=beta * e1))

    probs = [(d, ch) for ch in range(nch) for d in range(2)]
    st = {}
    for d, ch in probs:
        rs = slice(ch * c, (ch + 1) * c)
        base = d * nh
        incl = (ri >= jj) if d == 0 else (ri <= jj)
        gc_c = nar[d]["gc"][rs]
        gcc = colcat(gc_c, base)
        gcr = jnp.broadcast_to(jnp.sum(jnp.where(eye_cat, gcc, 0.0), axis=0, keepdims=True), (c, cw))
        decay = jnp.where(incl, jnp.exp(jnp.where(incl, gcc - gcr, 0.0)), 0.0)
        kf = xs[d][0][rs, DN_WIDTH:2 * DN_WIDTH]
        prod = _dot_nt(jnp.concatenate([xs[d][0][rs, :DN_WIDTH], kf], axis=0), block_diag(kf, bd_k))
        qkd_scr[d, ch] = prod[:c] * decay
        amat = jnp.where(incl & (~eye_cat), colcat(nar[d]["beta"][rs], base) * prod[c:] * decay, 0.0)
        lvl1 = ((ri >> 1) == (jj >> 1))
        st[d, ch] = dict(amat=amat.astype(BF16), tinv=jnp.where(eye_cat, 1.0, 0.0) - jnp.where(lvl1, amat, 0.0))
    s = 2
    while s < c:
        ls = s.bit_length() - 1
        lvl = jnp.where(((ri >> (ls + 1)) == (jj >> (ls + 1))) & ((ri >> ls) != (jj >> ls)), 1.0, 0.0).astype(BF16)
        for key in probs:
            p = st[key]
            p["t_bf"] = p["tinv"].astype(BF16)
            p["x"] = _dot(p["t_bf"], block_diag(p["amat"] * lvl, bd_sq))
        for key in probs:
            p = st[key]
            p["tinv"] = p["tinv"] - _dot(p["x"], block_diag(p["t_bf"], bd_sq))
        s *= 2
    for d, ch in probs:
        rs = slice(ch * c, (ch + 1) * c)
        base = d * nh
        x_ref = xs[d][0]
        rhs = jnp.concatenate([
            jnp.concatenate([x_ref[rs, 2 * DN_WIDTH + h * hd:2 * DN_WIDTH + (h + 1) * hd].astype(F32)
                             * col_bcast(nar[d]["beta"][rs], base + h, hd),
                             x_ref[rs, DN_WIDTH + h * hd:DN_WIDTH + (h + 1) * hd].astype(F32)
                             * col_bcast(nar[d]["be"][rs], base + h, hd)], axis=1)
            for h in range(nh)], axis=0)
        uw_scr[d, ch] = _dot(block_diag(st[d, ch]["tinv"], bd_sq), rhs)
    states = [[s_scr[d, h] for h in range(nh)] for d in range(2)]
    for step in range(nch):
        cur = [(0, step), (1, nch - 1 - step)]
        loc = {}
        for d, ch in cur:
            rs = slice(ch * c, (ch + 1) * c)
            base = d * nh
            gc_c = nar[d]["gc"][rs]
            gtot = gc_c[c - 1:c] if d == 0 else gc_c[0:1]
            e2 = jnp.exp(gtot - gc_c)
            r_ = []
            for h in range(nh):
                hs = slice(h * hd, (h + 1) * hd)
                w_h = uw_scr[d, ch, h * c:(h + 1) * c, hd:]
                qd_h = xs[d][0][rs, hs].astype(F32) * col_bcast(nar[d]["e1"][rs], base + h, hd)
                r_.append(_dot(jnp.concatenate([w_h, qd_h], axis=0), states[d][h]))
            loc[d] = dict(r=r_, e2=e2, gtot=gtot)
        for d, ch in cur:
            loc[d]["vn"] = [uw_scr[d, ch, h * c:(h + 1) * c, :hd] - loc[d]["r"][h][:c] for h in range(nh)]
        for d, ch in cur:
            rs = slice(ch * c, (ch + 1) * c)
            base = d * nh
            vn = loc[d]["vn"]
            o2 = _dot(block_diag(qkd_scr[d, ch], bd_sq), jnp.concatenate(vn, axis=0))
            for h in range(nh):
                hs = slice(h * hd, (h + 1) * hd)
                kd_h = (xs[d][0][rs, DN_WIDTH + h * hd:DN_WIDTH + (h + 1) * hd].astype(F32)
                        * col_bcast(loc[d]["e2"], base + h, hd))
                o_refs[d][rs, hs] = loc[d]["r"][h][c:] + o2[h * c:(h + 1) * c]
                eg = jnp.exp(jnp.broadcast_to(loc[d]["gtot"][0:1, base + h:base + h + 1], (hd, hd)))
                states[d][h] = states[d][h] * eg + _dot_tn(kd_h, vn[h])
    for d in range(2):
        for h in range(nh):
            s_scr[d, h] = states[d][h]

    if emit_state:
        @pl.when(i == nt - 1)
        def _():
            sout_ref[...] = s_scr[...]


def _gdn_call(qkv, ab, par, s0, emit_state):
    b, t, qkv_w = qkv.shape
    c = DN_CHUNK
    tr = GDN_TILE_CHUNKS * c
    nt = t // tr

    def tile_of(d, i):
        return i if d == 0 else nt - 1 - i

    in_specs = []
    args = []
    for d in range(2):
        in_specs += [
            pl.BlockSpec((None, tr, qkv_w), lambda bi, i, d=d: (bi, tile_of(d, i), 0)),
            pl.BlockSpec((None, tr, LANE), lambda bi, i, d=d: (bi, tile_of(d, i), 0)),
        ]
        args += [qkv, ab]
    in_specs += [pl.BlockSpec(par.shape, lambda bi, i: (0, 0))]
    args += [par]
    sshape = (2, DN_HEADS, DN_HEAD_DIM, DN_HEAD_DIM)
    in_specs += [pl.BlockSpec((None,) + sshape, lambda bi, i: (bi, 0, 0, 0, 0))]
    args += [s0]
    out_specs = [
        pl.BlockSpec((None, tr, DN_WIDTH), lambda bi, i: (bi, i, 0)),
        pl.BlockSpec((None, tr, DN_WIDTH), lambda bi, i: (bi, nt - 1 - i, 0)),
    ]
    out_shape = [jax.ShapeDtypeStruct((b, t, DN_WIDTH), F32)] * 2
    if emit_state:
        out_specs += [pl.BlockSpec((None,) + sshape, lambda bi, i: (bi, 0, 0, 0, 0))]
        out_shape += [jax.ShapeDtypeStruct((b,) + sshape, F32)]
    cw = DN_HEADS * c
    scratch = [
        pltpu.VMEM(sshape, F32),
        pltpu.VMEM((2, GDN_TILE_CHUNKS, cw, 2 * DN_HEAD_DIM), F32),
        pltpu.VMEM((2, GDN_TILE_CHUNKS, c, cw), F32),
    ]
    return pl.pallas_call(
        functools.partial(_gdn_kernel, nt=nt, emit_state=emit_state),
        grid=(b, nt),
        in_specs=in_specs,
        out_specs=out_specs,
        out_shape=out_shape,
        scratch_shapes=scratch,
        compiler_params=_params("parallel", "arbitrary"),
        name="gdn_ctx" if emit_state else "gdn_lat",
    )(*args)


def _na_kernel(q_ref, k_ref, v_ref, kc_ref, vc_ref, bias_ref, o_ref, *, rows):
    w = GRID_W
    nl = NA_WIN_R * w
    nr = NA_ROWS_PER_ITER
    scale = NA_HEAD_DIM ** -0.5
    kc = kc_ref[...].astype(BF16)
    vc = vc_ref[...].astype(BF16)

    def stack_heads(q2):
        sel = lax.broadcasted_iota(jnp.int32, q2.shape, 1) < NA_HEAD_DIM
        zero = jnp.zeros_like(q2)
        return jnp.concatenate([jnp.where(sel, q2, zero), jnp.where(sel, zero, q2)], axis=0)

    def unstack_heads(x):
        n = x.shape[0] // 2
        sel = lax.broadcasted_iota(jnp.int32, (n, LANE), 1) < NA_HEAD_DIM
        return jnp.where(sel, jnp.broadcast_to(x[:n], (n, LANE)), jnp.broadcast_to(x[n:], (n, LANE)))

    def row_block(it, carry):
        rr = [it * nr + u for u in range(nr)]
        rss = [jnp.clip(r - NA_WIN_R // 2, 0, rows - NA_WIN_R) for r in rr]
        rsl = [pl.ds(pl.multiple_of(r * w, w), w) for r in rr]
        wsl = [pl.ds(pl.multiple_of(rs * w, w), nl) for rs in rss]
        q_all = q_ref[pl.ds(pl.multiple_of(it * (nr * w), nr * w), nr * w), :] * scale
        s_c = _dot_nt(stack_heads(q_all), kc)
        s = [_dot_nt(stack_heads(q_all[u * w:(u + 1) * w]), k_ref[wsl[u], :]) + bias_ref[rss[u] - rr[u] + NA_WIN_R - 1]
             for u in range(nr)]
        m_c = jnp.max(s_c, axis=-1, keepdims=True)
        m_l = [jnp.max(x, axis=-1, keepdims=True) for x in s]
        p_c = jnp.exp(s_c - m_c)
        p = [jnp.exp(x - m) for x, m in zip(s, m_l)]
        o_c = unstack_heads(_dot(p_c, vc))
        o_l = [_dot(p[u], v_ref[wsl[u], :]) for u in range(nr)]
        l_c = unstack_heads(jnp.sum(p_c, axis=-1, keepdims=True))
        m_c = unstack_heads(m_c)
        for u in range(nr):
            usl = slice(u * w, (u + 1) * w)
            l_u = unstack_heads(jnp.sum(p[u], axis=-1, keepdims=True))
            m_u = unstack_heads(m_l[u])
            m = jnp.maximum(m_u, m_c[usl])
            a_l = jnp.exp(m_u - m)
            a_c = jnp.exp(m_c[usl] - m)
            o_ref[rsl[u], :] = (unstack_heads(o_l[u]) * a_l + o_c[usl] * a_c) / (l_u * a_l + l_c[usl] * a_c)
        return carry

    lax.fori_loop(0, rows // nr, row_block, 0)


def _na_call(p, pc, bias):
    b, t, _ = p.shape
    l = pc.shape[1]
    rows = t // GRID_W
    npair = NA_HEADS // 2
    return pl.pallas_call(
        functools.partial(_na_kernel, rows=rows),
        grid=(npair, b),
        in_specs=[
            pl.BlockSpec((None, t, LANE), lambda j, bi: (bi, 0, NA_BLK0 + j)),
            pl.BlockSpec((None, t, LANE), lambda j, bi: (bi, 0, NA_BLK0 + npair + j)),
            pl.BlockSpec((None, t, LANE), lambda j, bi: (bi, 0, NA_BLK0 + 2 * npair + j)),
            pl.BlockSpec((None, l, LANE), lambda j, bi: (bi, 0, NA_BLK0 + npair + j)),
            pl.BlockSpec((None, l, LANE), lambda j, bi: (bi, 0, NA_BLK0 + 2 * npair + j)),
            pl.BlockSpec((None,) + bias.shape[1:], lambda j, bi: (j, 0, 0, 0)),
        ],
        out_specs=pl.BlockSpec((None, t, LANE), lambda j, bi: (bi, 0, j)),
        out_shape=jax.ShapeDtypeStruct((b, t, NA_WIDTH), F32),
        compiler_params=_params("parallel", "parallel"),
        name="na_lat",
    )(p, p, p, pc, pc, bias)


def _ctx_attn_kernel(q_ref, k_ref, v_ref, o_ref):
    scale = NA_HEAD_DIM ** -0.5
    q2 = q_ref[...] * scale
    k2 = k_ref[...]
    v2 = v_ref[...]
    head0 = lax.broadcasted_iota(jnp.int32, q2.shape, 1) < NA_HEAD_DIM
    out = None
    for hh in range(2):
        sel = head0 if hh == 0 else ~head0
        s = _dot_nt(jnp.where(sel, q2, jnp.zeros_like(q2)), k2)
        p = jnp.exp(s - jnp.max(s, axis=-1, keepdims=True))
        o = _dot(p, v2) / jnp.sum(p, axis=-1, keepdims=True)
        out = o if out is None else jnp.where(sel, o, out)
    o_ref[...] = out


def _ctx_attn_call(pc):
    b, l, _ = pc.shape
    npair = NA_HEADS // 2
    return pl.pallas_call(
        _ctx_attn_kernel,
        grid=(b, npair),
        in_specs=[
            pl.BlockSpec((None, l, LANE), lambda bi, j: (bi, 0, NA_BLK0 + j)),
            pl.BlockSpec((None, l, LANE), lambda bi, j: (bi, 0, NA_BLK0 + npair + j)),
            pl.BlockSpec((None, l, LANE), lambda bi, j: (bi, 0, NA_BLK0 + 2 * npair + j)),
        ],
        out_specs=pl.BlockSpec((None, l, LANE), lambda bi, j: (bi, 0, j)),
        out_shape=jax.ShapeDtypeStruct((b, l, NA_WIDTH), F32),
        compiler_params=_params("parallel", "parallel"),
        name="ctx_attn",
    )(pc, pc, pc)


def _merge_kernel(of_ref, ob_ref, z_ref, bg_ref, cg_ref, hh_ref, cgp_ref, cgn_ref, hhp_ref, hhn_ref, yc_ref,
                  ga_ref, gb_ref, gcg_ref, x_ref, gt_ref, wpa_ref, wpb_ref, wpc_ref, wo_ref, ng_ref, scw_ref,
                  o_ref, *, nt):
    i = pl.program_id(1)
    tm = x_ref.shape[0]
    o = of_ref[...] + ob_ref[...]
    sz = z_ref[...].astype(F32)
    ya = []
    for h in range(DN_HEADS):
        sl = slice(h * DN_HEAD_DIM, (h + 1) * DN_HEAD_DIM)
        ya.append(_rms(o[:, sl]) * ng_ref[...] * sz[:, sl])
    y_a = jnp.concatenate(ya, axis=1)
    has_prev = (i > 0).astype(F32)
    has_next = (i < nt - 1).astype(F32)
    f32 = lambda r: r[...].astype(F32)
    full = jnp.concatenate([f32(cgp_ref) * f32(hhp_ref) * has_prev, f32(cg_ref) * f32(hh_ref),
                            f32(cgn_ref) * f32(hhn_ref) * has_next], axis=0)
    y_b = f32(bg_ref) * _dwconv3_rows(full, scw_ref, tm, HALO_P)
    y = (f32(ga_ref) * _dot(y_a, wpa_ref[...]) + f32(gb_ref) * _dot(y_b, wpb_ref[...])
         + f32(gcg_ref) * _dot(yc_ref[...], wpc_ref[...]))
    o_ref[...] = x_ref[...] + gt_ref[...] * _dot(y, wo_ref[...])


def _merge_call(of, ob, p, yc, x, mod, mrow, wpa, wpb, wpc, wo, ng, scw, tm):
    b, t, d = x.shape
    nt = t // tm
    hb = tm // HALO_P
    nhb = t // HALO_P
    w5 = SC_WIDTH

    def tile(width, col):
        return pl.BlockSpec((None, tm, width), lambda bi, i: (bi, i, col))

    def prev(col):
        return pl.BlockSpec((None, HALO_P, w5), lambda bi, i: (bi, jnp.maximum(i * hb - 1, 0), col))

    def nxt(col):
        return pl.BlockSpec((None, HALO_P, w5), lambda bi, i: (bi, jnp.minimum(i * hb + hb, nhb - 1), col))

    def full(a):
        return pl.BlockSpec(a.shape, lambda bi, i: (0,) * a.ndim)

    in_specs = [
        tile(DN_WIDTH, 0), tile(DN_WIDTH, 0),
        tile(w5, 3), tile(w5, 4), tile(w5, 5), tile(w5, 6),
        prev(5), nxt(5), prev(6), nxt(6),
        tile(NA_WIDTH, 0),
        tile(d, 5), tile(d, 6), tile(d, 7),
        tile(d, 0),
        pl.BlockSpec((None, 1, d), lambda bi, i: (mrow(bi), 0, 2)),
        full(wpa), full(wpb), full(wpc), full(wo), full(ng), full(scw),
    ]
    return pl.pallas_call(
        functools.partial(_merge_kernel, nt=nt),
        grid=(b, nt),
        in_specs=in_specs,
        out_specs=pl.BlockSpec((None, tm, d), lambda bi, i: (bi, i, 0)),
        out_shape=jax.ShapeDtypeStruct((b, t, d), F32),
        compiler_params=_params("parallel", "parallel"),
        name="merge",
    )(of, ob, p, p, p, p, p, p, p, p, yc, p, p, p, x, mod, wpa, wpb, wpc, wo, ng, scw)


def _ffn_kernel(*refs, nt, nk, fc, final):
    it = iter(refs)
    x_ref, xp_ref, xn_ref, g_ref, sh_ref, sc_ref, gt_ref = (next(it) for _ in range(7))
    wup_ref, cw_ref, cb_ref, wd_ref = (next(it) for _ in range(4))
    fg_ref = next(it) if final else None
    o_ref, h_scr, act_scr = next(it), next(it), next(it)
    i = pl.program_id(1)
    tm = x_ref.shape[0]
    dff = nk * fc

    xfull = jnp.concatenate([xp_ref[...], x_ref[...], xn_ref[...]], axis=0)
    h = (_rms(xfull) * g_ref[...]) * (1.0 + sc_ref[...]) + sh_ref[...]
    h_scr[...] = h.astype(BF16)

    row = lax.broadcasted_iota(jnp.int32, (tm + 2 * HALO, 1), 0)
    keep = ((row >= HALO) | (i > 0)) & ((row < tm + HALO) | (i < nt - 1))

    def up_proj(k):
        return [jnp.where(keep, jnp.dot(h_scr[...], wup_ref[:, c0:c0 + fc], preferred_element_type=F32), 0.0)
                for c0 in (k * fc, dff + k * fc)]

    ups = up_proj(0)
    g0 = 0
    for k in range(nk):
        nxt = up_proj(k + 1) if k + 1 < nk else None
        a = _dwconv3_rows(ups[0], cw_ref[:, k * fc:(k + 1) * fc], tm) + cb_ref[:, k * fc:(k + 1) * fc]
        bb = (_dwconv3_rows(ups[1], cw_ref[:, dff + k * fc:dff + (k + 1) * fc], tm)
              + cb_ref[:, dff + k * fc:dff + (k + 1) * fc])
        act_scr[:, k * fc:(k + 1) * fc] = (_silu(a) * bb).astype(BF16)
        if (k + 1) % FFN_DOWN_GROUP == 0 or k == nk - 1:
            g1 = (k + 1) * fc
            contrib = jnp.dot(act_scr[:, g0:g1], wd_ref[g0:g1, :], preferred_element_type=F32)
            if g0 == 0:
                o_ref[...] = contrib
            else:
                o_ref[...] += contrib
            g0 = g1
        ups = nxt

    out = x_ref[...] + gt_ref[...] * o_ref[...]
    if final:
        out = _rms(out) * fg_ref[...]
    o_ref[...] = out


def _ffn_call(x, g, mod, mrow, w_up, cw, cb, w_down, tm, final_g=None):
    b, t, d = x.shape
    dff = w_down.shape[0]
    fc = 256
    nk = dff // fc
    nt = t // tm
    hb = tm // HALO
    nhb = t // HALO
    final = final_g is not None

    def resident(a):
        return pl.BlockSpec(a.shape, lambda bi, i: (0,) * a.ndim, pipeline_mode=pl.Buffered(1))

    in_specs = [
        pl.BlockSpec((None, tm, d), lambda bi, i: (bi, i, 0)),
        pl.BlockSpec((None, HALO, d), lambda bi, i: (bi, jnp.maximum(i * hb - 1, 0), 0)),
        pl.BlockSpec((None, HALO, d), lambda bi, i: (bi, jnp.minimum(i * hb + hb, nhb - 1), 0)),
        pl.BlockSpec((1, d), lambda bi, i: (0, 0)),
        pl.BlockSpec((None, 1, d), lambda bi, i: (mrow(bi), 0, 3)),
        pl.BlockSpec((None, 1, d), lambda bi, i: (mrow(bi), 0, 4)),
        pl.BlockSpec((None, 1, d), lambda bi, i: (mrow(bi), 0, 5)),
        resident(w_up), resident(cw), resident(cb), resident(w_down),
    ]
    args = [x, x, x, g, mod, mod, mod, w_up, cw, cb, w_down]
    if final:
        in_specs.append(pl.BlockSpec((1, d), lambda bi, i: (0, 0)))
        args.append(final_g)
    return pl.pallas_call(
        functools.partial(_ffn_kernel, nt=nt, nk=nk, fc=fc, final=final),
        grid=(b, nt),
        in_specs=in_specs,
        out_specs=pl.BlockSpec((None, tm, d), lambda bi, i: (bi, i, 0)),
        out_shape=jax.ShapeDtypeStruct((b, t, d), F32),
        scratch_shapes=[pltpu.VMEM((tm + 2 * HALO, d), BF16), pltpu.VMEM((tm, dff), BF16)],
        compiler_params=_params("parallel", "parallel"),
        name="ffn_final" if final else "ffn",
    )(*args)


def _rope_tables(t):
    tok = jnp.arange(t, dtype=jnp.int32)
    rows = (tok // GRID_W).astype(F32)
    cols = (tok % GRID_W).astype(F32)
    nf = DN_HEAD_DIM // 4
    inv = ROPE_BASE ** (-jnp.arange(nf, dtype=F32) / nf)
    ang = jnp.concatenate([rows[:, None] * inv, cols[:, None] * inv], axis=-1)
    cos = jnp.repeat(jnp.cos(ang), 2, axis=-1)
    sin = jnp.repeat(jnp.sin(ang), 2, axis=-1)
    sign = jnp.where(jnp.arange(DN_HEAD_DIM) % 2 == 0, -1.0, 1.0).astype(F32)
    return cos, sin * sign


def _na_bias_table(rpb):
    w = GRID_W
    depth = rpb.shape[0]
    qc = np.arange(w)[:, None]
    kc = np.arange(w)[None, :]
    cstart = np.clip(qc - NA_WIN_C // 2, 0, w - NA_WIN_C)
    valid = (kc >= cstart) & (kc < cstart + NA_WIN_C)
    onehot = ((kc - qc + NA_WIN_C - 1)[None] == np.arange(2 * NA_WIN_C - 1)[:, None, None]) & valid[None]
    tbl = jnp.einsum("lhdj,jck->lhdck", rpb, jnp.asarray(onehot, F32), precision=lax.Precision.HIGHEST)
    tbl = tbl + jnp.asarray(np.where(valid, 0.0, NEG), F32)
    tbl = jnp.stack([tbl[:, :, d0:d0 + NA_WIN_R] for d0 in range(NA_WIN_R)], axis=2)
    tbl = tbl.transpose(0, 1, 2, 4, 3, 5).reshape(depth, NA_HEADS // 2, 2, NA_WIN_R, w, NA_WIN_R * w)
    return tbl.transpose(0, 1, 3, 2, 4, 5).reshape(depth, NA_HEADS // 2, NA_WIN_R, 2 * w, NA_WIN_R * w)


def _permute_w_in(w_in):
    s = np.cumsum([0, 3 * DN_WIDTH, DN_WIDTH, 2 * DN_HEADS, 2 * DN_HEADS, 3 * SC_WIDTH, 3 * NA_WIDTH, 3 * 1024])
    qkv, z, a, b, sc, na, gates = (w_in[..., s[j]:s[j + 1]] for j in range(7))
    pad = jnp.zeros(w_in.shape[:-1] + (LANE - 4 * DN_HEADS,), w_in.dtype)
    return jnp.concatenate([qkv, z, sc, na, gates], axis=-1), jnp.concatenate([a, b, pad], axis=-1)


def kernel(x, c, ctx, c_ctx, norm1_g, norm2_g, w_ada, b_ada, w_in, dn_conv_w, dn_a_log, dn_dt_bias, dn_norm_g, sc_conv_w, na_rpb, w_pa, w_pb, w_pc, w_o, w_up, ffn_conv_w, ffn_conv_b, w_down, final_norm_g):
    bsz, t, d = x.shape
    depth = w_in.shape[0]
    assert d == 1024 and t % (NA_WIN_R * GRID_W) == 0 and ctx.shape[1] % (GDN_TILE_CHUNKS * DN_CHUNK) == 0

    mod_rows = -(-(bsz + 1) // SUBLANE) * SUBLANE
    cc = jnp.zeros((mod_rows, d), F32).at[:bsz].set(c).at[bsz].set(c_ctx)
    mod = _ada_call(cc, w_ada, b_ada).reshape(depth, mod_rows, 1, 6 * d)
    lat_row = lambda bi: bi
    ctx_row = lambda bi: bsz

    w_in_p, w_ab = (a.astype(BF16) for a in _permute_w_in(w_in))
    rope = _rope_tables(t)
    na_bias = _na_bias_table(na_rpb)
    par = jnp.zeros((depth, SUBLANE, LANE), F32)
    par = par.at[:, 0, :2 * DN_HEADS].set(dn_a_log.reshape(depth, -1)).at[:, 1, :2 * DN_HEADS].set(dn_dt_bias.reshape(depth, -1))
    s_zero = jnp.zeros((bsz, 2, DN_HEADS, DN_HEAD_DIM, DN_HEAD_DIM), F32)
    tm_lat = 1024 if t % 1024 == 0 else 512
    tm_ctx = ctx.shape[1]

    xc = ctx
    for l in range(depth):
        need_ctx = l < depth - 1
        g1 = norm1_g[l][None]
        g2 = norm2_g[l][None]
        wpa, wpb, wpc, wo = (a[l].astype(BF16) for a in (w_pa, w_pb, w_pc, w_o))
        wup, wdn = w_up[l].astype(BF16), w_down[l].astype(BF16)
        ng = dn_norm_g[l][None]
        p, ab = _inproj_call(x, g1, mod[l], lat_row, w_in_p[l], w_ab[l], tm_lat)
        pc, abc = _inproj_call(xc, g1, mod[l], ctx_row, w_in_p[l], w_ab[l], tm_ctx)
        ocf, ocb, s_ctx = _gdn_call(_gdn_prep_call(pc, dn_conv_w[l], None), abc, par[l], s_zero, True)
        olf, olb = _gdn_call(_gdn_prep_call(p, dn_conv_w[l], rope), ab, par[l], s_ctx, False)
        y_c = _na_call(p, pc, na_bias[l])
        x = _merge_call(olf, olb, p, y_c, x, mod[l], lat_row, wpa, wpb, wpc, wo, ng, sc_conv_w[l], 512)
        fg = final_norm_g[None] if l == depth - 1 else None
        x = _ffn_call(x, g2, mod[l], lat_row, wup, ffn_conv_w[l], ffn_conv_b[l][None], wdn, tm_lat, fg)
        if need_ctx:
            yc_c = _ctx_attn_call(pc)
            xc = _merge_call(ocf, ocb, pc, yc_c, xc, mod[l], ctx_row, wpa, wpb, wpc, wo, ng, sc_conv_w[l], tm_ctx)
            xc = _ffn_call(xc, g2, mod[l], ctx_row, wup, ffn_conv_w[l], ffn_conv_b[l][None], wdn, tm_ctx)
    return x
```

```python
import functools

import numpy as np
import jax
import jax.numpy as jnp
from jax import lax
from jax.experimental import pallas as pl
from jax.experimental.pallas import tpu as pltpu

F32 = jnp.float32
BF16 = jnp.bfloat16

GRID_W = 64
DN_HEADS = 4
DN_HEAD_DIM = 128
DN_WIDTH = DN_HEADS * DN_HEAD_DIM
DN_CHUNK = 64
GDN_TILE_CHUNKS = 4
FFN_DOWN_GROUP = 4
SC_WIDTH = 512
NA_HEADS = 8
NA_HEAD_DIM = 64
NA_WIDTH = NA_HEADS * NA_HEAD_DIM
NA_WIN_R = 8
NA_WIN_C = 16
NA_ROWS_PER_ITER = 4
ROPE_BASE = 10000.0
EPS = 1e-6
NEG = -1e30

LANE = 128
SUBLANE = 8
HALO = SUBLANE
HALO_P = 2 * SUBLANE
VMEM_LIMIT = 48 * 1024 * 1024

P_COLS = 3 * DN_WIDTH + DN_WIDTH + 3 * SC_WIDTH + 3 * NA_WIDTH + 3 * 1024
P_TILE = 2048
Z_COL0 = 3 * DN_WIDTH
GATE_COL0 = P_COLS - 3 * 1024
NA_BLK0 = (3 * DN_WIDTH + DN_WIDTH + 3 * SC_WIDTH) // LANE


def _dot(a, b):
    return jnp.dot(a.astype(BF16), b.astype(BF16), preferred_element_type=F32)


def _dot_nt(a, b):
    return lax.dot_general(a.astype(BF16), b.astype(BF16), (((1,), (1,)), ((), ())), preferred_element_type=F32)


def _dot_tn(a, b):
    return lax.dot_general(a.astype(BF16), b.astype(BF16), (((0,), (0,)), ((), ())), preferred_element_type=F32)


def _sigmoid(x):
    return 0.5 * jnp.tanh(0.5 * x) + 0.5


def _silu(x):
    return x * _sigmoid(x)


def _softplus(x):
    return jnp.maximum(x, 0.0) + jnp.log1p(jnp.exp(-jnp.abs(x)))


def _rms(x, eps=EPS):
    return x * lax.rsqrt(jnp.mean(x * x, axis=-1, keepdims=True) + eps)


def _params(*sem):
    return pltpu.CompilerParams(dimension_semantics=sem, vmem_limit_bytes=VMEM_LIMIT)


def _dwconv3_rows(full, w_ref, n, halo=HALO):
    rows = full.shape[0]
    dn = pltpu.roll(full, 1, axis=0)
    up = pltpu.roll(full, rows - 1, axis=0)
    out = dn * w_ref[0:1, :] + full * w_ref[1:2, :] + up * w_ref[2:3, :]
    return out[halo:halo + n]


def _ada_kernel(c_ref, w_ref, b_ref, o_ref):
    o_ref[...] = _dot(_silu(c_ref[...]), w_ref[...]) + b_ref[...]


def _ada_call(cc, w_ada, b_ada):
    depth, d, n6 = w_ada.shape
    r = cc.shape[0]
    tn = 1536
    return pl.pallas_call(
        _ada_kernel,
        grid=(depth, n6 // tn),
        in_specs=[
            pl.BlockSpec((r, d), lambda l, j: (0, 0)),
            pl.BlockSpec((None, d, tn), lambda l, j: (l, 0, j)),
            pl.BlockSpec((None, 1, tn), lambda l, j: (l, 0, j)),
        ],
        out_specs=pl.BlockSpec((None, r, tn), lambda l, j: (l, 0, j)),
        out_shape=jax.ShapeDtypeStruct((depth, r, n6), F32),
        compiler_params=_params("parallel", "parallel"),
        name="ada_mod",
    )(cc, w_ada, b_ada.reshape(depth, 1, n6))


def _inproj_kernel(x_ref, g_ref, sh_ref, sc_ref, w_ref, wab_ref, o_ref, ab_ref, h_ref):
    j = pl.program_id(2)

    @pl.when(j == 0)
    def _():
        h = (_rms(x_ref[...]) * g_ref[...]) * (1.0 + sc_ref[...]) + sh_ref[...]
        h_ref[...] = h.astype(BF16)
        ab_ref[...] = jnp.dot(h_ref[...], wab_ref[...], preferred_element_type=F32)

    res = jnp.dot(h_ref[...], w_ref[...], preferred_element_type=F32)
    sg = _sigmoid(res)
    col = j * res.shape[1] + lax.broadcasted_iota(jnp.int32, (1, res.shape[1]), 1)
    is_z = (col >= Z_COL0) & (col < Z_COL0 + DN_WIDTH)
    out = jnp.where(col >= GATE_COL0, sg, jnp.where(is_z, res * sg, res))
    o_ref[...] = out.astype(o_ref.dtype)


def _inproj_call(x, g, mod, mrow, w, wab, tm):
    b, t, d = x.shape
    n = w.shape[1]
    tn = P_TILE
    return pl.pallas_call(
        _inproj_kernel,
        grid=(b, t // tm, n // tn),
        in_specs=[
            pl.BlockSpec((None, tm, d), lambda bi, i, j: (bi, i, 0)),
            pl.BlockSpec((1, d), lambda bi, i, j: (0, 0)),
            pl.BlockSpec((None, 1, d), lambda bi, i, j: (mrow(bi), 0, 0)),
            pl.BlockSpec((None, 1, d), lambda bi, i, j: (mrow(bi), 0, 1)),
            pl.BlockSpec((d, tn), lambda bi, i, j: (0, j)),
            pl.BlockSpec((d, LANE), lambda bi, i, j: (0, 0)),
        ],
        out_specs=[pl.BlockSpec((None, tm, tn), lambda bi, i, j: (bi, i, j)),
                   pl.BlockSpec((None, tm, LANE), lambda bi, i, j: (bi, i, 0))],
        out_shape=[jax.ShapeDtypeStruct((b, t, n), BF16), jax.ShapeDtypeStruct((b, t, LANE), F32)],
        scratch_shapes=[pltpu.VMEM((tm, d), BF16)],
        compiler_params=_params("parallel", "parallel", "arbitrary"),
        name="inproj",
    )(x, g, mod, mod, w, wab)


def _swap_pairs(x, even):
    n = x.shape[-1]
    return jnp.where(even, pltpu.roll(x, n - 1, axis=1), pltpu.roll(x, 1, axis=1))


def _gdn_prep_kernel(*refs, nt, use_rope):
    x_ref, xp_ref, xn_ref, cw_ref = refs[:4]
    cos_ref, sin_ref = refs[4:6] if use_rope else (None, None)
    o_ref = refs[-1]
    i = pl.program_id(1)
    tr = x_ref.shape[0]
    hd = DN_HEAD_DIM
    has_prev = (i > 0).astype(F32)
    has_next = (i < nt - 1).astype(F32)
    full = jnp.concatenate([xp_ref[...].astype(F32) * has_prev, x_ref[...].astype(F32),
                            xn_ref[...].astype(F32) * has_next], axis=0)
    act = _silu(_dwconv3_rows(full, cw_ref, tr, HALO_P))
    even = (lax.broadcasted_iota(jnp.int32, (tr, hd), 1) % 2) == 0
    for off, scale in ((0, hd ** -0.5), (DN_WIDTH, 1.0)):
        for h in range(DN_HEADS):
            sl = slice(off + h * hd, off + (h + 1) * hd)
            xh = act[:, sl]
            xh = xh * lax.rsqrt(jnp.sum(xh * xh, axis=-1, keepdims=True) + EPS)
            if use_rope:
                xh = xh * cos_ref[...] + _swap_pairs(xh, even) * sin_ref[...]
            o_ref[:, sl] = (xh * scale).astype(o_ref.dtype)
    o_ref[:, 2 * DN_WIDTH:] = act[:, 2 * DN_WIDTH:].astype(o_ref.dtype)


def _gdn_prep_call(p, conv_w, rope):
    b, t, _ = p.shape
    tr = GDN_TILE_CHUNKS * DN_CHUNK
    nt = t // tr
    hb = tr // HALO_P
    nhb = t // HALO_P
    qkv_w = 3 * DN_WIDTH
    in_specs = [
        pl.BlockSpec((None, tr, qkv_w), lambda bi, i: (bi, i, 0)),
        pl.BlockSpec((None, HALO_P, qkv_w), lambda bi, i: (bi, jnp.maximum(i * hb - 1, 0), 0)),
        pl.BlockSpec((None, HALO_P, qkv_w), lambda bi, i: (bi, jnp.minimum(i * hb + hb, nhb - 1), 0)),
        pl.BlockSpec(conv_w.shape, lambda bi, i: (0, 0)),
    ]
    args = [p, p, p, conv_w]
    if rope is not None:
        in_specs += [pl.BlockSpec((tr, DN_HEAD_DIM), lambda bi, i: (i, 0))] * 2
        args += [rope[0], rope[1]]
    return pl.pallas_call(
        functools.partial(_gdn_prep_kernel, nt=nt, use_rope=rope is not None),
        grid=(b, nt),
        in_specs=in_specs,
        out_specs=pl.BlockSpec((None, tr, qkv_w), lambda bi, i: (bi, i, 0)),
        out_shape=jax.ShapeDtypeStruct((b, t, qkv_w), p.dtype),
        compiler_params=_params("parallel", "parallel"),
        name="gdn_prep",
    )(*args)


def _gdn_kernel(*refs, nt, emit_state):
    it = iter(refs)
    xs = [[next(it) for _ in range(2)] for _ in range(2)]
    par_ref = next(it)
    s0_ref = next(it)
    o_refs = [next(it), next(it)]
    sout_ref = next(it) if emit_state else None
    s_scr, uw_scr, qkd_scr = (next(it) for _ in range(3))

    c = DN_CHUNK
    nh = DN_HEADS
    hd = DN_HEAD_DIM
    nch = GDN_TILE_CHUNKS
    tr = nch * c
    cw = nh * c
    lc = c.bit_length() - 1
    i = pl.program_id(1)

    @pl.when(i == 0)
    def _():
        s_scr[...] = s0_ref[...]

    ri = lax.broadcasted_iota(jnp.int32, (c, cw), 0)
    li = lax.broadcasted_iota(jnp.int32, (c, cw), 1)
    jj = li & (c - 1)
    seg = li >> lc
    eye_cat = ri == jj
    bd_sq = jnp.where((lax.broadcasted_iota(jnp.int32, (cw, cw), 0) >> lc)
                      == (lax.broadcasted_iota(jnp.int32, (cw, cw), 1) >> lc), 1.0, 0.0).astype(BF16)
    bd_k = jnp.where((lax.broadcasted_iota(jnp.int32, (cw, nh * hd), 0) >> lc)
                     == (lax.broadcasted_iota(jnp.int32, (cw, nh * hd), 1) // hd), 1.0, 0.0).astype(BF16)
    rowi = lax.broadcasted_iota(jnp.int32, (tr, LANE), 0) & (c - 1)
    neg_a = -jnp.exp(par_ref[0:1, :])
    dt_b = par_ref[1:2, :]

    def col_bcast(x, j, width):
        return jnp.broadcast_to(x[:, j:j + 1], (x.shape[0], width))

    def colcat(x, base):
        out = col_bcast(x, base + nh - 1, cw)
        for h in range(nh - 2, -1, -1):
            out = jnp.where(seg == h, col_bcast(x, base + h, cw), out)
        return out

    def block_diag(m, mask):
        return jnp.concatenate([m.astype(BF16)] * nh, axis=0) * mask

    nar = []
    for d in range(2):
        ab = xs[d][1][...]
        gc = neg_a * _softplus(ab + dt_b)
        beta = pltpu.roll(_sigmoid(ab), LANE - 2 * nh, axis=1)
        sft = 1
        while sft < c:
            if d == 0:
                gc = gc + jnp.where(rowi >= sft, pltpu.roll(gc, sft, axis=0), 0.0)
            else:
                gc = gc + jnp.where(rowi < c - sft, pltpu.roll(gc, tr - sft, axis=0), 0.0)
            sft *= 2
        e1 = jnp.exp(gc)
        nar.append(dict(gc=gc, e1=e1, beta=beta, be=beta * e1))

    st = {}

    def setup(key):
        d, ch = key
        rs = slice(ch * c, (ch + 1) * c)
        base = d * nh
        incl = (ri >= jj) if d == 0 else (ri <= jj)
        gcc = colcat(nar[d]["gc"][rs], base)
        gcr = jnp.broadcast_to(jnp.sum(jnp.where(eye_cat, gcc, 0.0), axis=0, keepdims=True), (c, cw))
        decay = jnp.where(incl, jnp.exp(jnp.where(incl, gcc - gcr, 0.0)), 0.0)
        kf = xs[d][0][rs, DN_WIDTH:2 * DN_WIDTH]
        prod = _dot_nt(jnp.concatenate([xs[d][0][rs, :DN_WIDTH], kf], axis=0), block_diag(kf, bd_k))
        qkd_scr[d, ch] = prod[:c] * decay
        amat = jnp.where(incl & (~eye_cat), colcat(nar[d]["beta"][rs], base) * prod[c:] * decay, 0.0)
        lvl1 = ((ri >> 1) == (jj >> 1))
        st[key] = dict(amat=amat.astype(BF16), tinv=jnp.where(eye_cat, 1.0, 0.0) - jnp.where(lvl1, amat, 0.0))

    def invert_level(keys, s):
        ls = s.bit_length() - 1
        lvl = jnp.where(((ri >> (ls + 1)) == (jj >> (ls + 1))) & ((ri >> ls) != (jj >> ls)), 1.0, 0.0).astype(BF16)
        for key in keys:
            p = st[key]
            p["t_bf"] = p["tinv"].astype(BF16)
            p["x"] = _dot(p["t_bf"], block_diag(p["amat"] * lvl, bd_sq))
        for key in keys:
            p = st[key]
            p["tinv"] = p["tinv"] - _dot(p["x"], block_diag(p["t_bf"], bd_sq))

    def solve(key):
        d, ch = key
        rs = slice(ch * c, (ch + 1) * c)
        base = d * nh
        x_ref = xs[d][0]
        rhs = jnp.concatenate([
            jnp.concatenate([x_ref[rs, 2 * DN_WIDTH + h * hd:2 * DN_WIDTH + (h + 1) * hd].astype(F32)
                             * col_bcast(nar[d]["beta"][rs], base + h, hd),
                             x_ref[rs, DN_WIDTH + h * hd:DN_WIDTH + (h + 1) * hd].astype(F32)
                             * col_bcast(nar[d]["be"][rs], base + h, hd)], axis=1)
            for h in range(nh)], axis=0)
        uw_scr[d, ch] = _dot(block_diag(st.pop(key)["tinv"], bd_sq), rhs)

    states = [[s_scr[d, h] for h in range(nh)] for d in range(2)]

    def scan_step(step):
        cur = [(0, step), (1, nch - 1 - step)]
        loc = {}
        for d, ch in cur:
            rs = slice(ch * c, (ch + 1) * c)
            base = d * nh
            gc_c = nar[d]["gc"][rs]
            gtot = gc_c[c - 1:c] if d == 0 else gc_c[0:1]
            e2 = jnp.exp(gtot - gc_c)
            r_ = []
            for h in range(nh):
                hs = slice(h * hd, (h + 1) * hd)
                w_h = uw_scr[d, ch, h * c:(h + 1) * c, hd:]
                qd_h = xs[d][0][rs, hs].astype(F32) * col_bcast(nar[d]["e1"][rs], base + h, hd)
                r_.append(_dot(jnp.concatenate([w_h, qd_h], axis=0), states[d][h]))
            loc[d] = dict(r=r_, e2=e2, gtot=gtot)
        for d, ch in cur:
            loc[d]["vn"] = [uw_scr[d, ch, h * c:(h + 1) * c, :hd] - loc[d]["r"][h][:c] for h in range(nh)]
        for d, ch in cur:
            rs = slice(ch * c, (ch + 1) * c)
            base = d * nh
            vn = loc[d]["vn"]
            o2 = _dot(block_diag(qkd_scr[d, ch], bd_sq), jnp.concatenate(vn, axis=0))
            for h in range(nh):
                hs = slice(h * hd, (h + 1) * hd)
                kd_h = (xs[d][0][rs, DN_WIDTH + h * hd:DN_WIDTH + (h + 1) * hd].astype(F32)
                        * col_bcast(loc[d]["e2"], base + h, hd))
                o_refs[d][rs, hs] = loc[d]["r"][h][c:] + o2[h * c:(h + 1) * c]
                eg = jnp.exp(jnp.broadcast_to(loc[d]["gtot"][0:1, base + h:base + h + 1], (hd, hd)))
                states[d][h] = states[d][h] * eg + _dot_tn(kd_h, vn[h])

    probs = [(d, ch) for ch in range(nch) for d in range(2)]
    for key in probs:
        setup(key)
    for e in range(1, lc):
        invert_level(probs, 1 << e)
    for key in probs:
        solve(key)
    for step in range(nch):
        scan_step(step)
    for d in range(2):
        for h in range(nh):
            s_scr[d, h] = states[d][h]

    if emit_state:
        @pl.when(i == nt - 1)
        def _():
            sout_ref[...] = s_scr[...]


def _gdn_call(qkv, ab, par, s0, emit_state):
    b, t, qkv_w = qkv.shape
    c = DN_CHUNK
    tr = GDN_TILE_CHUNKS * c
    nt = t // tr

    def tile_of(d, i):
        return i if d == 0 else nt - 1 - i

    in_specs = []
    args = []
    for d in range(2):
        in_specs += [
            pl.BlockSpec((None, tr, qkv_w), lambda bi, i, d=d: (bi, tile_of(d, i), 0)),
            pl.BlockSpec((None, tr, LANE), lambda bi, i, d=d: (bi, tile_of(d, i), 0)),
        ]
        args += [qkv, ab]
    in_specs += [pl.BlockSpec(par.shape, lambda bi, i: (0, 0))]
    args += [par]
    sshape = (2, DN_HEADS, DN_HEAD_DIM, DN_HEAD_DIM)
    in_specs += [pl.BlockSpec((None,) + sshape, lambda bi, i: (bi, 0, 0, 0, 0))]
    args += [s0]
    out_specs = [
        pl.BlockSpec((None, tr, DN_WIDTH), lambda bi, i: (bi, i, 0)),
        pl.BlockSpec((None, tr, DN_WIDTH), lambda bi, i: (bi, nt - 1 - i, 0)),
    ]
    out_shape = [jax.ShapeDtypeStruct((b, t, DN_WIDTH), F32)] * 2
    if emit_state:
        out_specs += [pl.BlockSpec((None,) + sshape, lambda bi, i: (bi, 0, 0, 0, 0))]
        out_shape += [jax.ShapeDtypeStruct((b,) + sshape, F32)]
    cw = DN_HEADS * c
    scratch = [
        pltpu.VMEM(sshape, F32),
        pltpu.VMEM((2, GDN_TILE_CHUNKS, cw, 2 * DN_HEAD_DIM), F32),
        pltpu.VMEM((2, GDN_TILE_CHUNKS, c, cw), F32),
    ]
    return pl.pallas_call(
        functools.partial(_gdn_kernel, nt=nt, emit_state=emit_state),
        grid=(b, nt),
        in_specs=in_specs,
        out_specs=out_specs,
        out_shape=out_shape,
        scratch_shapes=scratch,
        compiler_params=_params("parallel", "arbitrary"),
        name="gdn_ctx" if emit_state else "gdn_lat",
    )(*args)


def _na_kernel(q_ref, k_ref, v_ref, kc_ref, vc_ref, bias_ref, o_ref, *, rows):
    w = GRID_W
    nl = NA_WIN_R * w
    nr = NA_ROWS_PER_ITER
    scale = NA_HEAD_DIM ** -0.5
    kc = kc_ref[...].astype(BF16)
    vc = vc_ref[...].astype(BF16)

    def stack_heads(q2):
        sel = lax.broadcasted_iota(jnp.int32, q2.shape, 1) < NA_HEAD_DIM
        zero = jnp.zeros_like(q2)
        return jnp.concatenate([jnp.where(sel, q2, zero), jnp.where(sel, zero, q2)], axis=0)

    def unstack_heads(x):
        n = x.shape[0] // 2
        sel = lax.broadcasted_iota(jnp.int32, (n, LANE), 1) < NA_HEAD_DIM
        return jnp.where(sel, jnp.broadcast_to(x[:n], (n, LANE)), jnp.broadcast_to(x[n:], (n, LANE)))

    def row_block(it, carry):
        rr = [it * nr + u for u in range(nr)]
        rss = [jnp.clip(r - NA_WIN_R // 2, 0, rows - NA_WIN_R) for r in rr]
        rsl = [pl.ds(pl.multiple_of(r * w, w), w) for r in rr]
        wsl = [pl.ds(pl.multiple_of(rs * w, w), nl) for rs in rss]
        q_all = q_ref[pl.ds(pl.multiple_of(it * (nr * w), nr * w), nr * w), :] * scale
        s_c = _dot_nt(stack_heads(q_all), kc)
        s = [_dot_nt(stack_heads(q_all[u * w:(u + 1) * w]), k_ref[wsl[u], :]) + bias_ref[rss[u] - rr[u] + NA_WIN_R - 1]
             for u in range(nr)]
        m_c = jnp.max(s_c, axis=-1, keepdims=True)
        m_l = [jnp.max(x, axis=-1, keepdims=True) for x in s]
        p_c = jnp.exp(s_c - m_c)
        p = [jnp.exp(x - m) for x, m in zip(s, m_l)]
        o_c = unstack_heads(_dot(p_c, vc))
        o_l = [_dot(p[u], v_ref[wsl[u], :]) for u in range(nr)]
        l_c = unstack_heads(jnp.sum(p_c, axis=-1, keepdims=True))
        m_c = unstack_heads(m_c)
        for u in range(nr):
            usl = slice(u * w, (u + 1) * w)
            l_u = unstack_heads(jnp.sum(p[u], axis=-1, keepdims=True))
            m_u = unstack_heads(m_l[u])
            m = jnp.maximum(m_u, m_c[usl])
            a_l = jnp.exp(m_u - m)
            a_c = jnp.exp(m_c[usl] - m)
            o_ref[rsl[u], :] = (unstack_heads(o_l[u]) * a_l + o_c[usl] * a_c) / (l_u * a_l + l_c[usl] * a_c)
        return carry

    lax.fori_loop(0, rows // nr, row_block, 0)


def _na_call(p, pc, bias):
    b, t, _ = p.shape
    l = pc.shape[1]
    rows = t // GRID_W
    npair = NA_HEADS // 2
    return pl.pallas_call(
        functools.partial(_na_kernel, rows=rows),
        grid=(npair, b),
        in_specs=[
            pl.BlockSpec((None, t, LANE), lambda j, bi: (bi, 0, NA_BLK0 + j)),
            pl.BlockSpec((None, t, LANE), lambda j, bi: (bi, 0, NA_BLK0 + npair + j)),
            pl.BlockSpec((None, t, LANE), lambda j, bi: (bi, 0, NA_BLK0 + 2 * npair + j)),
            pl.BlockSpec((None, l, LANE), lambda j, bi: (bi, 0, NA_BLK0 + npair + j)),
            pl.BlockSpec((None, l, LANE), lambda j, bi: (bi, 0, NA_BLK0 + 2 * npair + j)),
            pl.BlockSpec((None,) + bias.shape[1:], lambda j, bi: (j, 0, 0, 0)),
        ],
        out_specs=pl.BlockSpec((None, t, LANE), lambda j, bi: (bi, 0, j)),
        out_shape=jax.ShapeDtypeStruct((b, t, NA_WIDTH), F32),
        compiler_params=_params("parallel", "parallel"),
        name="na_lat",
    )(p, p, p, pc, pc, bias)


def _ctx_attn_kernel(q_ref, k_ref, v_ref, o_ref):
    scale = NA_HEAD_DIM ** -0.5
    q2 = q_ref[...] * scale
    k2 = k_ref[...]
    v2 = v_ref[...]
    head0 = lax.broadcasted_iota(jnp.int32, q2.shape, 1) < NA_HEAD_DIM
    out = None
    for hh in range(2):
        sel = head0 if hh == 0 else ~head0
        s = _dot_nt(jnp.where(sel, q2, jnp.zeros_like(q2)), k2)
        p = jnp.exp(s - jnp.max(s, axis=-1, keepdims=True))
        o = _dot(p, v2) / jnp.sum(p, axis=-1, keepdims=True)
        out = o if out is None else jnp.where(sel, o, out)
    o_ref[...] = out


def _ctx_attn_call(pc):
    b, l, _ = pc.shape
    npair = NA_HEADS // 2
    return pl.pallas_call(
        _ctx_attn_kernel,
        grid=(b, npair),
        in_specs=[
            pl.BlockSpec((None, l, LANE), lambda bi, j: (bi, 0, NA_BLK0 + j)),
            pl.BlockSpec((None, l, LANE), lambda bi, j: (bi, 0, NA_BLK0 + npair + j)),
            pl.BlockSpec((None, l, LANE), lambda bi, j: (bi, 0, NA_BLK0 + 2 * npair + j)),
        ],
        out_specs=pl.BlockSpec((None, l, LANE), lambda bi, j: (bi, 0, j)),
        out_shape=jax.ShapeDtypeStruct((b, l, NA_WIDTH), F32),
        compiler_params=_params("parallel", "parallel"),
        name="ctx_attn",
    )(pc, pc, pc)


def _merge_kernel(of_ref, ob_ref, z_ref, bg_ref, cg_ref, hh_ref, cgp_ref, cgn_ref, hhp_ref, hhn_ref, yc_ref,
                  ga_ref, gb_ref, gcg_ref, x_ref, gt_ref, wpa_ref, wpb_ref, wpc_ref, wo_ref, ng_ref, scw_ref,
                  o_ref, *, nt):
    i = pl.program_id(1)
    tm = x_ref.shape[0]
    o = of_ref[...] + ob_ref[...]
    sz = z_ref[...].astype(F32)
    ya = []
    for h in range(DN_HEADS):
        sl = slice(h * DN_HEAD_DIM, (h + 1) * DN_HEAD_DIM)
        ya.append(_rms(o[:, sl]) * ng_ref[...] * sz[:, sl])
    y_a = jnp.concatenate(ya, axis=1)
    has_prev = (i > 0).astype(F32)
    has_next = (i < nt - 1).astype(F32)
    f32 = lambda r: r[...].astype(F32)
    full = jnp.concatenate([f32(cgp_ref) * f32(hhp_ref) * has_prev, f32(cg_ref) * f32(hh_ref),
                            f32(cgn_ref) * f32(hhn_ref) * has_next], axis=0)
    y_b = f32(bg_ref) * _dwconv3_rows(full, scw_ref, tm, HALO_P)
    y = (f32(ga_ref) * _dot(y_a, wpa_ref[...]) + f32(gb_ref) * _dot(y_b, wpb_ref[...])
         + f32(gcg_ref) * _dot(yc_ref[...], wpc_ref[...]))
    o_ref[...] = x_ref[...] + gt_ref[...] * _dot(y, wo_ref[...])


def _merge_call(of, ob, p, yc, x, mod, mrow, wpa, wpb, wpc, wo, ng, scw, tm):
    b, t, d = x.shape
    nt = t // tm
    hb = tm // HALO_P
    nhb = t // HALO_P
    w5 = SC_WIDTH

    def tile(width, col):
        return pl.BlockSpec((None, tm, width), lambda bi, i: (bi, i, col))

    def prev(col):
        return pl.BlockSpec((None, HALO_P, w5), lambda bi, i: (bi, jnp.maximum(i * hb - 1, 0), col))

    def nxt(col):
        return pl.BlockSpec((None, HALO_P, w5), lambda bi, i: (bi, jnp.minimum(i * hb + hb, nhb - 1), col))

    def full(a):
        return pl.BlockSpec(a.shape, lambda bi, i: (0,) * a.ndim)

    in_specs = [
        tile(DN_WIDTH, 0), tile(DN_WIDTH, 0),
        tile(w5, 3), tile(w5, 4), tile(w5, 5), tile(w5, 6),
        prev(5), nxt(5), prev(6), nxt(6),
        tile(NA_WIDTH, 0),
        tile(d, 5), tile(d, 6), tile(d, 7),
        tile(d, 0),
        pl.BlockSpec((None, 1, d), lambda bi, i: (mrow(bi), 0, 2)),
        full(wpa), full(wpb), full(wpc), full(wo), full(ng), full(scw),
    ]
    return pl.pallas_call(
        functools.partial(_merge_kernel, nt=nt),
        grid=(b, nt),
        in_specs=in_specs,
        out_specs=pl.BlockSpec((None, tm, d), lambda bi, i: (bi, i, 0)),
        out_shape=jax.ShapeDtypeStruct((b, t, d), F32),
        compiler_params=_params("parallel", "parallel"),
        name="merge",
    )(of, ob, p, p, p, p, p, p, p, p, yc, p, p, p, x, mod, wpa, wpb, wpc, wo, ng, scw)


def _ffn_kernel(*refs, nt, nk, fc, final):
    it = iter(refs)
    x_ref, xp_ref, xn_ref, g_ref, sh_ref, sc_ref, gt_ref = (next(it) for _ in range(7))
    wup_ref, cw_ref, cb_ref, wd_ref = (next(it) for _ in range(4))
    fg_ref = next(it) if final else None
    o_ref, h_scr, act_scr = next(it), next(it), next(it)
    i = pl.program_id(1)
    tm = x_ref.shape[0]
    dff = nk * fc

    xfull = jnp.concatenate([xp_ref[...], x_ref[...], xn_ref[...]], axis=0)
    h = (_rms(xfull) * g_ref[...]) * (1.0 + sc_ref[...]) + sh_ref[...]
    h_scr[...] = h.astype(BF16)

    row = lax.broadcasted_iota(jnp.int32, (tm + 2 * HALO, 1), 0)
    keep = ((row >= HALO) | (i > 0)) & ((row < tm + HALO) | (i < nt - 1))

    def up_proj(k):
        return [jnp.where(keep, jnp.dot(h_scr[...], wup_ref[:, c0:c0 + fc], preferred_element_type=F32), 0.0)
                for c0 in (k * fc, dff + k * fc)]

    ups = up_proj(0)
    g0 = 0
    for k in range(nk):
        nxt = up_proj(k + 1) if k + 1 < nk else None
        a = _dwconv3_rows(ups[0], cw_ref[:, k * fc:(k + 1) * fc], tm) + cb_ref[:, k * fc:(k + 1) * fc]
        bb = (_dwconv3_rows(ups[1], cw_ref[:, dff + k * fc:dff + (k + 1) * fc], tm)
              + cb_ref[:, dff + k * fc:dff + (k + 1) * fc])
        act_scr[:, k * fc:(k + 1) * fc] = (_silu(a) * bb).astype(BF16)
        if (k + 1) % FFN_DOWN_GROUP == 0 or k == nk - 1:
            g1 = (k + 1) * fc
            contrib = jnp.dot(act_scr[:, g0:g1], wd_ref[g0:g1, :], preferred_element_type=F32)
            if g0 == 0:
                o_ref[...] = contrib
            else:
                o_ref[...] += contrib
            g0 = g1
        ups = nxt

    out = x_ref[...] + gt_ref[...] * o_ref[...]
    if final:
        out = _rms(out) * fg_ref[...]
    o_ref[...] = out


def _ffn_call(x, g, mod, mrow, w_up, cw, cb, w_down, tm, final_g=None):
    b, t, d = x.shape
    dff = w_down.shape[0]
    fc = 256
    nk = dff // fc
    nt = t // tm
    hb = tm // HALO
    nhb = t // HALO
    final = final_g is not None

    def resident(a):
        return pl.BlockSpec(a.shape, lambda bi, i: (0,) * a.ndim, pipeline_mode=pl.Buffered(1))

    in_specs = [
        pl.BlockSpec((None, tm, d), lambda bi, i: (bi, i, 0)),
        pl.BlockSpec((None, HALO, d), lambda bi, i: (bi, jnp.maximum(i * hb - 1, 0), 0)),
        pl.BlockSpec((None, HALO, d), lambda bi, i: (bi, jnp.minimum(i * hb + hb, nhb - 1), 0)),
        pl.BlockSpec((1, d), lambda bi, i: (0, 0)),
        pl.BlockSpec((None, 1, d), lambda bi, i: (mrow(bi), 0, 3)),
        pl.BlockSpec((None, 1, d), lambda bi, i: (mrow(bi), 0, 4)),
        pl.BlockSpec((None, 1, d), lambda bi, i: (mrow(bi), 0, 5)),
        resident(w_up), resident(cw), resident(cb), resident(w_down),
    ]
    args = [x, x, x, g, mod, mod, mod, w_up, cw, cb, w_down]
    if final:
        in_specs.append(pl.BlockSpec((1, d), lambda bi, i: (0, 0)))
        args.append(final_g)
    return pl.pallas_call(
        functools.partial(_ffn_kernel, nt=nt, nk=nk, fc=fc, final=final),
        grid=(b, nt),
        in_specs=in_specs,
        out_specs=pl.BlockSpec((None, tm, d), lambda bi, i: (bi, i, 0)),
        out_shape=jax.ShapeDtypeStruct((b, t, d), F32),
        scratch_shapes=[pltpu.VMEM((tm + 2 * HALO, d), BF16), pltpu.VMEM((tm, dff), BF16)],
        compiler_params=_params("parallel", "parallel"),
        name="ffn_final" if final else "ffn",
    )(*args)


def _rope_tables(t):
    tok = jnp.arange(t, dtype=jnp.int32)
    rows = (tok // GRID_W).astype(F32)
    cols = (tok % GRID_W).astype(F32)
    nf = DN_HEAD_DIM // 4
    inv = ROPE_BASE ** (-jnp.arange(nf, dtype=F32) / nf)
    ang = jnp.concatenate([rows[:, None] * inv, cols[:, None] * inv], axis=-1)
    cos = jnp.repeat(jnp.cos(ang), 2, axis=-1)
    sin = jnp.repeat(jnp.sin(ang), 2, axis=-1)
    sign = jnp.where(jnp.arange(DN_HEAD_DIM) % 2 == 0, -1.0, 1.0).astype(F32)
    return cos, sin * sign


def _na_bias_table(rpb):
    w = GRID_W
    depth = rpb.shape[0]
    qc = np.arange(w)[:, None]
    kc = np.arange(w)[None, :]
    cstart = np.clip(qc - NA_WIN_C // 2, 0, w - NA_WIN_C)
    valid = (kc >= cstart) & (kc < cstart + NA_WIN_C)
    onehot = ((kc - qc + NA_WIN_C - 1)[None] == np.arange(2 * NA_WIN_C - 1)[:, None, None]) & valid[None]
    tbl = jnp.einsum("lhdj,jck->lhdck", rpb, jnp.asarray(onehot, F32), precision=lax.Precision.HIGHEST)
    tbl = tbl + jnp.asarray(np.where(valid, 0.0, NEG), F32)
    tbl = jnp.stack([tbl[:, :, d0:d0 + NA_WIN_R] for d0 in range(NA_WIN_R)], axis=2)
    tbl = tbl.transpose(0, 1, 2, 4, 3, 5).reshape(depth, NA_HEADS // 2, 2, NA_WIN_R, w, NA_WIN_R * w)
    return tbl.transpose(0, 1, 3, 2, 4, 5).reshape(depth, NA_HEADS // 2, NA_WIN_R, 2 * w, NA_WIN_R * w)


def _permute_w_in(w_in):
    s = np.cumsum([0, 3 * DN_WIDTH, DN_WIDTH, 2 * DN_HEADS, 2 * DN_HEADS, 3 * SC_WIDTH, 3 * NA_WIDTH, 3 * 1024])
    qkv, z, a, b, sc, na, gates = (w_in[..., s[j]:s[j + 1]] for j in range(7))
    pad = jnp.zeros(w_in.shape[:-1] + (LANE - 4 * DN_HEADS,), w_in.dtype)
    return jnp.concatenate([qkv, z, sc, na, gates], axis=-1), jnp.concatenate([a, b, pad], axis=-1)


def kernel(x, c, ctx, c_ctx, norm1_g, norm2_g, w_ada, b_ada, w_in, dn_conv_w, dn_a_log, dn_dt_bias, dn_norm_g, sc_conv_w, na_rpb, w_pa, w_pb, w_pc, w_o, w_up, ffn_conv_w, ffn_conv_b, w_down, final_norm_g):
    bsz, t, d = x.shape
    depth = w_in.shape[0]
    assert d == 1024 and t % (NA_WIN_R * GRID_W) == 0 and ctx.shape[1] % (GDN_TILE_CHUNKS * DN_CHUNK) == 0

    mod_rows = -(-(bsz + 1) // SUBLANE) * SUBLANE
    cc = jnp.zeros((mod_rows, d), F32).at[:bsz].set(c).at[bsz].set(c_ctx)
    mod = _ada_call(cc, w_ada, b_ada).reshape(depth, mod_rows, 1, 6 * d)
    lat_row = lambda bi: bi
    ctx_row = lambda bi: bsz

    w_in_p, w_ab = (a.astype(BF16) for a in _permute_w_in(w_in))
    rope = _rope_tables(t)
    na_bias = _na_bias_table(na_rpb)
    par = jnp.zeros((depth, SUBLANE, LANE), F32)
    par = par.at[:, 0, :2 * DN_HEADS].set(dn_a_log.reshape(depth, -1)).at[:, 1, :2 * DN_HEADS].set(dn_dt_bias.reshape(depth, -1))
    s_zero = jnp.zeros((bsz, 2, DN_HEADS, DN_HEAD_DIM, DN_HEAD_DIM), F32)
    tm_lat = 1024 if t % 1024 == 0 else 512
    tm_ctx = ctx.shape[1]

    xc = ctx
    for l in range(depth):
        need_ctx = l < depth - 1
        g1 = norm1_g[l][None]
        g2 = norm2_g[l][None]
        wpa, wpb, wpc, wo = (a[l].astype(BF16) for a in (w_pa, w_pb, w_pc, w_o))
        wup, wdn = w_up[l].astype(BF16), w_down[l].astype(BF16)
        ng = dn_norm_g[l][None]
        p, ab = _inproj_call(x, g1, mod[l], lat_row, w_in_p[l], w_ab[l], tm_lat)
        lc_all = bsz * tm_ctx
        pc, abc = _inproj_call(xc.reshape(1, lc_all, d), g1, mod[l], ctx_row, w_in_p[l], w_ab[l],
                               tm_lat if lc_all % tm_lat == 0 else tm_ctx)
        pc, abc = pc.reshape(bsz, tm_ctx, -1), abc.reshape(bsz, tm_ctx, -1)
        ocf, ocb, s_ctx = _gdn_call(_gdn_prep_call(pc, dn_conv_w[l], None), abc, par[l], s_zero, True)
        olf, olb = _gdn_call(_gdn_prep_call(p, dn_conv_w[l], rope), ab, par[l], s_ctx, False)
        y_c = _na_call(p, pc, na_bias[l])
        x = _merge_call(olf, olb, p, y_c, x, mod[l], lat_row, wpa, wpb, wpc, wo, ng, sc_conv_w[l], 512)
        fg = final_norm_g[None] if l == depth - 1 else None
        x = _ffn_call(x, g2, mod[l], lat_row, wup, ffn_conv_w[l], ffn_conv_b[l][None], wdn, tm_lat, fg)
        if need_ctx:
            yc_c = _ctx_attn_call(pc)
            xc = _merge_call(ocf, ocb, pc, yc_c, xc, mod[l], ctx_row, wpa, wpb, wpc, wo, ng, sc_conv_w[l], tm_ctx)
            xc = _ffn_call(xc, g2, mod[l], ctx_row, wup, ffn_conv_w[l], ffn_conv_b[l][None], wdn, tm_ctx)
    return x
```

```python
import functools

import numpy as np
import jax
import jax.numpy as jnp
from jax import lax
from jax.experimental import pallas as pl
from jax.experimental.pallas import tpu as pltpu

F32 = jnp.float32
BF16 = jnp.bfloat16

GRID_W = 64
DN_HEADS = 4
DN_HEAD_DIM = 128
DN_WIDTH = DN_HEADS * DN_HEAD_DIM
DN_CHUNK = 64
GDN_TILE_CHUNKS = 4
FFN_DOWN_GROUP = 4
SC_WIDTH = 512
NA_HEADS = 8
NA_HEAD_DIM = 64
NA_WIDTH = NA_HEADS * NA_HEAD_DIM
NA_WIN_R = 8
NA_WIN_C = 16
NA_ROWS_PER_ITER = 4
ROPE_BASE = 10000.0
EPS = 1e-6
NEG = -1e30

LANE = 128
SUBLANE = 8
HALO = SUBLANE
HALO_P = 2 * SUBLANE
VMEM_LIMIT = 48 * 1024 * 1024

P_COLS = 3 * DN_WIDTH + DN_WIDTH + 3 * SC_WIDTH + 3 * NA_WIDTH + 3 * 1024
P_TILE = 2048
Z_COL0 = 3 * DN_WIDTH
GATE_COL0 = P_COLS - 3 * 1024
NA_BLK0 = (3 * DN_WIDTH + DN_WIDTH + 3 * SC_WIDTH) // LANE


def _dot(a, b):
    return jnp.dot(a.astype(BF16), b.astype(BF16), preferred_element_type=F32)


def _dot_nt(a, b):
    return lax.dot_general(a.astype(BF16), b.astype(BF16), (((1,), (1,)), ((), ())), preferred_element_type=F32)


def _dot_tn(a, b):
    return lax.dot_general(a.astype(BF16), b.astype(BF16), (((0,), (0,)), ((), ())), preferred_element_type=F32)


def _sigmoid(x):
    return 0.5 * jnp.tanh(0.5 * x) + 0.5


def _silu(x):
    return x * _sigmoid(x)


def _softplus(x):
    return jnp.maximum(x, 0.0) + jnp.log1p(jnp.exp(-jnp.abs(x)))


def _rms(x, eps=EPS):
    return x * lax.rsqrt(jnp.mean(x * x, axis=-1, keepdims=True) + eps)


def _params(*sem):
    return pltpu.CompilerParams(dimension_semantics=sem, vmem_limit_bytes=VMEM_LIMIT)


def _dwconv3_rows(full, w_ref, n, halo=HALO):
    rows = full.shape[0]
    dn = pltpu.roll(full, 1, axis=0)
    up = pltpu.roll(full, rows - 1, axis=0)
    out = dn * w_ref[0:1, :] + full * w_ref[1:2, :] + up * w_ref[2:3, :]
    return out[halo:halo + n]


def _ada_kernel(c_ref, w_ref, b_ref, o_ref):
    o_ref[...] = _dot(_silu(c_ref[...]), w_ref[...]) + b_ref[...]


def _ada_call(cc, w_ada, b_ada):
    depth, d, n6 = w_ada.shape
    r = cc.shape[0]
    tn = 1536
    return pl.pallas_call(
        _ada_kernel,
        grid=(depth, n6 // tn),
        in_specs=[
            pl.BlockSpec((r, d), lambda l, j: (0, 0)),
            pl.BlockSpec((None, d, tn), lambda l, j: (l, 0, j)),
            pl.BlockSpec((None, 1, tn), lambda l, j: (l, 0, j)),
        ],
        out_specs=pl.BlockSpec((None, r, tn), lambda l, j: (l, 0, j)),
        out_shape=jax.ShapeDtypeStruct((depth, r, n6), F32),
        compiler_params=_params("parallel", "parallel"),
        name="ada_mod",
    )(cc, w_ada, b_ada.reshape(depth, 1, n6))


def _inproj_kernel(x_ref, g_ref, sh_ref, sc_ref, w_ref, wab_ref, o_ref, ab_ref, h_ref):
    j = pl.program_id(2)

    @pl.when(j == 0)
    def _():
        h = (_rms(x_ref[...]) * g_ref[...]) * (1.0 + sc_ref[...]) + sh_ref[...]
        h_ref[...] = h.astype(BF16)
        ab_ref[...] = jnp.dot(h_ref[...], wab_ref[...], preferred_element_type=F32)

    res = jnp.dot(h_ref[...], w_ref[...], preferred_element_type=F32)
    sg = _sigmoid(res)
    col = j * res.shape[1] + lax.broadcasted_iota(jnp.int32, (1, res.shape[1]), 1)
    is_z = (col >= Z_COL0) & (col < Z_COL0 + DN_WIDTH)
    out = jnp.where(col >= GATE_COL0, sg, jnp.where(is_z, res * sg, res))
    o_ref[...] = out.astype(o_ref.dtype)


def _inproj_call(x, g, mod, mrow, w, wab, tm):
    b, t, d = x.shape
    n = w.shape[1]
    tn = P_TILE
    return pl.pallas_call(
        _inproj_kernel,
        grid=(b, t // tm, n // tn),
        in_specs=[
            pl.BlockSpec((None, tm, d), lambda bi, i, j: (bi, i, 0)),
            pl.BlockSpec((1, d), lambda bi, i, j: (0, 0)),
            pl.BlockSpec((None, 1, d), lambda bi, i, j: (mrow(bi), 0, 0)),
            pl.BlockSpec((None, 1, d), lambda bi, i, j: (mrow(bi), 0, 1)),
            pl.BlockSpec((d, tn), lambda bi, i, j: (0, j)),
            pl.BlockSpec((d, LANE), lambda bi, i, j: (0, 0)),
        ],
        out_specs=[pl.BlockSpec((None, tm, tn), lambda bi, i, j: (bi, i, j)),
                   pl.BlockSpec((None, tm, LANE), lambda bi, i, j: (bi, i, 0))],
        out_shape=[jax.ShapeDtypeStruct((b, t, n), BF16), jax.ShapeDtypeStruct((b, t, LANE), F32)],
        scratch_shapes=[pltpu.VMEM((tm, d), BF16)],
        compiler_params=_params("parallel", "parallel", "arbitrary"),
        name="inproj",
    )(x, g, mod, mod, w, wab)


def _swap_pairs(x, even):
    n = x.shape[-1]
    return jnp.where(even, pltpu.roll(x, n - 1, axis=1), pltpu.roll(x, 1, axis=1))


def _gdn_prep_kernel(*refs, nt, use_rope):
    x_ref, xp_ref, xn_ref, cw_ref = refs[:4]
    cos_ref, sin_ref = refs[4:6] if use_rope else (None, None)
    o_ref = refs[-1]
    i = pl.program_id(1)
    tr = x_ref.shape[0]
    hd = DN_HEAD_DIM
    has_prev = (i > 0).astype(F32)
    has_next = (i < nt - 1).astype(F32)
    full = jnp.concatenate([xp_ref[...].astype(F32) * has_prev, x_ref[...].astype(F32),
                            xn_ref[...].astype(F32) * has_next], axis=0)
    act = _silu(_dwconv3_rows(full, cw_ref, tr, HALO_P))
    even = (lax.broadcasted_iota(jnp.int32, (tr, hd), 1) % 2) == 0
    for off, scale in ((0, hd ** -0.5), (DN_WIDTH, 1.0)):
        for h in range(DN_HEADS):
            sl = slice(off + h * hd, off + (h + 1) * hd)
            xh = act[:, sl]
            xh = xh * lax.rsqrt(jnp.sum(xh * xh, axis=-1, keepdims=True) + EPS)
            if use_rope:
                xh = xh * cos_ref[...] + _swap_pairs(xh, even) * sin_ref[...]
            o_ref[:, sl] = (xh * scale).astype(o_ref.dtype)
    o_ref[:, 2 * DN_WIDTH:] = act[:, 2 * DN_WIDTH:].astype(o_ref.dtype)


def _gdn_prep_call(p, conv_w, rope):
    b, t, _ = p.shape
    tr = GDN_TILE_CHUNKS * DN_CHUNK
    nt = t // tr
    hb = tr // HALO_P
    nhb = t // HALO_P
    qkv_w = 3 * DN_WIDTH
    in_specs = [
        pl.BlockSpec((None, tr, qkv_w), lambda bi, i: (bi, i, 0)),
        pl.BlockSpec((None, HALO_P, qkv_w), lambda bi, i: (bi, jnp.maximum(i * hb - 1, 0), 0)),
        pl.BlockSpec((None, HALO_P, qkv_w), lambda bi, i: (bi, jnp.minimum(i * hb + hb, nhb - 1), 0)),
        pl.BlockSpec(conv_w.shape, lambda bi, i: (0, 0)),
    ]
    args = [p, p, p, conv_w]
    if rope is not None:
        in_specs += [pl.BlockSpec((tr, DN_HEAD_DIM), lambda bi, i: (i, 0))] * 2
        args += [rope[0], rope[1]]
    return pl.pallas_call(
        functools.partial(_gdn_prep_kernel, nt=nt, use_rope=rope is not None),
        grid=(b, nt),
        in_specs=in_specs,
        out_specs=pl.BlockSpec((None, tr, qkv_w), lambda bi, i: (bi, i, 0)),
        out_shape=jax.ShapeDtypeStruct((b, t, qkv_w), p.dtype),
        compiler_params=_params("parallel", "parallel"),
        name="gdn_prep",
    )(*args)


def _gdn_kernel(*refs, nt, emit_state):
    it = iter(refs)
    xs = [[next(it) for _ in range(2)] for _ in range(2)]
    par_ref = next(it)
    s0_ref = next(it)
    o_refs = [next(it), next(it)]
    sout_ref = next(it) if emit_state else None
    s_scr, uw_scr, qkd_scr = (next(it) for _ in range(3))

    c = DN_CHUNK
    nh = DN_HEADS
    hd = DN_HEAD_DIM
    nch = GDN_TILE_CHUNKS
    tr = nch * c
    cw = nh * c
    lc = c.bit_length() - 1
    i = pl.program_id(1)

    @pl.when(i == 0)
    def _():
        s_scr[...] = s0_ref[...]

    ri = lax.broadcasted_iota(jnp.int32, (c, cw), 0)
    li = lax.broadcasted_iota(jnp.int32, (c, cw), 1)
    jj = li & (c - 1)
    seg = li >> lc
    eye_cat = ri == jj
    bd_sq = jnp.where((lax.broadcasted_iota(jnp.int32, (cw, cw), 0) >> lc)
                      == (lax.broadcasted_iota(jnp.int32, (cw, cw), 1) >> lc), 1.0, 0.0).astype(BF16)
    bd_k = jnp.where((lax.broadcasted_iota(jnp.int32, (cw, nh * hd), 0) >> lc)
                     == (lax.broadcasted_iota(jnp.int32, (cw, nh * hd), 1) // hd), 1.0, 0.0).astype(BF16)
    rowi = lax.broadcasted_iota(jnp.int32, (tr, LANE), 0) & (c - 1)
    neg_a = -jnp.exp(par_ref[0:1, :])
    dt_b = par_ref[1:2, :]

    def col_bcast(x, j, width):
        return jnp.broadcast_to(x[:, j:j + 1], (x.shape[0], width))

    def colcat(x, base):
        out = col_bcast(x, base + nh - 1, cw)
        for h in range(nh - 2, -1, -1):
            out = jnp.where(seg == h, col_bcast(x, base + h, cw), out)
        return out

    def block_diag(m, mask):
        return jnp.concatenate([m.astype(BF16)] * nh, axis=0) * mask

    nar = []
    for d in range(2):
        ab = xs[d][1][...]
        gc = neg_a * _softplus(ab + dt_b)
        beta = pltpu.roll(_sigmoid(ab), LANE - 2 * nh, axis=1)
        sft = 1
        while sft < c:
            if d == 0:
                gc = gc + jnp.where(rowi >= sft, pltpu.roll(gc, sft, axis=0), 0.0)
            else:
                gc = gc + jnp.where(rowi < c - sft, pltpu.roll(gc, tr - sft, axis=0), 0.0)
            sft *= 2
        e1 = jnp.exp(gc)
        nar.append(dict(gc=gc, e1=e1, beta=beta, be=beta * e1))

    st = {}

    def setup(key):
        d, ch = key
        rs = slice(ch * c, (ch + 1) * c)
        base = d * nh
        incl = (ri >= jj) if d == 0 else (ri <= jj)
        gcc = colcat(nar[d]["gc"][rs], base)
        gcr = jnp.broadcast_to(jnp.sum(jnp.where(eye_cat, gcc, 0.0), axis=0, keepdims=True), (c, cw))
        decay = jnp.where(incl, jnp.exp(jnp.where(incl, gcc - gcr, 0.0)), 0.0)
        kf = xs[d][0][rs, DN_WIDTH:2 * DN_WIDTH]
        prod = _dot_nt(jnp.concatenate([xs[d][0][rs, :DN_WIDTH], kf], axis=0), block_diag(kf, bd_k))
        qkd_scr[d, ch] = prod[:c] * decay
        amat = jnp.where(incl & (~eye_cat), colcat(nar[d]["beta"][rs], base) * prod[c:] * decay, 0.0)
        lvl1 = ((ri >> 1) == (jj >> 1))
        st[key] = dict(amat=amat.astype(BF16), tinv=jnp.where(eye_cat, 1.0, 0.0) - jnp.where(lvl1, amat, 0.0))

    def invert_level(keys, s):
        ls = s.bit_length() - 1
        lvl = jnp.where(((ri >> (ls + 1)) == (jj >> (ls + 1))) & ((ri >> ls) != (jj >> ls)), 1.0, 0.0).astype(BF16)
        for key in keys:
            p = st[key]
            p["t_bf"] = p["tinv"].astype(BF16)
            p["x"] = _dot(p["t_bf"], block_diag(p["amat"] * lvl, bd_sq))
        for key in keys:
            p = st[key]
            p["tinv"] = p["tinv"] - _dot(p["x"], block_diag(p["t_bf"], bd_sq))

    def solve(key):
        d, ch = key
        rs = slice(ch * c, (ch + 1) * c)
        base = d * nh
        x_ref = xs[d][0]
        rhs = jnp.concatenate([
            jnp.concatenate([x_ref[rs, 2 * DN_WIDTH + h * hd:2 * DN_WIDTH + (h + 1) * hd].astype(F32)
                             * col_bcast(nar[d]["beta"][rs], base + h, hd),
                             x_ref[rs, DN_WIDTH + h * hd:DN_WIDTH + (h + 1) * hd].astype(F32)
                             * col_bcast(nar[d]["be"][rs], base + h, hd)], axis=1)
            for h in range(nh)], axis=0)
        uw_scr[d, ch] = _dot(block_diag(st.pop(key)["tinv"], bd_sq), rhs)

    states = [[s_scr[d, h] for h in range(nh)] for d in range(2)]

    def scan_step(step):
        cur = [(0, step), (1, nch - 1 - step)]
        loc = {}
        for d, ch in cur:
            rs = slice(ch * c, (ch + 1) * c)
            base = d * nh
            gc_c = nar[d]["gc"][rs]
            gtot = gc_c[c - 1:c] if d == 0 else gc_c[0:1]
            e2 = jnp.exp(gtot - gc_c)
            r_ = []
            for h in range(nh):
                hs = slice(h * hd, (h + 1) * hd)
                w_h = uw_scr[d, ch, h * c:(h + 1) * c, hd:]
                qd_h = xs[d][0][rs, hs].astype(F32) * col_bcast(nar[d]["e1"][rs], base + h, hd)
                r_.append(_dot(jnp.concatenate([w_h, qd_h], axis=0), states[d][h]))
            loc[d] = dict(r=r_, e2=e2, gtot=gtot)
        for d, ch in cur:
            loc[d]["vn"] = [uw_scr[d, ch, h * c:(h + 1) * c, :hd] - loc[d]["r"][h][:c] for h in range(nh)]
        for d, ch in cur:
            rs = slice(ch * c, (ch + 1) * c)
            base = d * nh
            vn = loc[d]["vn"]
            o2 = _dot(block_diag(qkd_scr[d, ch], bd_sq), jnp.concatenate(vn, axis=0))
            for h in range(nh):
                hs = slice(h * hd, (h + 1) * hd)
                kd_h = (xs[d][0][rs, DN_WIDTH + h * hd:DN_WIDTH + (h + 1) * hd].astype(F32)
                        * col_bcast(loc[d]["e2"], base + h, hd))
                o_refs[d][rs, hs] = (loc[d]["r"][h][c:] + o2[h * c:(h + 1) * c]).astype(o_refs[d].dtype)
                eg = jnp.exp(jnp.broadcast_to(loc[d]["gtot"][0:1, base + h:base + h + 1], (hd, hd)))
                states[d][h] = states[d][h] * eg + _dot_tn(kd_h, vn[h])

    probs = [(d, ch) for ch in range(nch) for d in range(2)]
    for key in probs:
        setup(key)
    for e in range(1, lc):
        invert_level(probs, 1 << e)
    for key in probs:
        solve(key)
    for step in range(nch):
        scan_step(step)
    for d in range(2):
        for h in range(nh):
            s_scr[d, h] = states[d][h]

    if emit_state:
        @pl.when(i == nt - 1)
        def _():
            sout_ref[...] = s_scr[...]


def _gdn_call(qkv, ab, par, s0, emit_state):
    b, t, qkv_w = qkv.shape
    c = DN_CHUNK
    tr = GDN_TILE_CHUNKS * c
    nt = t // tr

    def tile_of(d, i):
        return i if d == 0 else nt - 1 - i

    in_specs = []
    args = []
    for d in range(2):
        in_specs += [
            pl.BlockSpec((None, tr, qkv_w), lambda bi, i, d=d: (bi, tile_of(d, i), 0)),
            pl.BlockSpec((None, tr, LANE), lambda bi, i, d=d: (bi, tile_of(d, i), 0)),
        ]
        args += [qkv, ab]
    in_specs += [pl.BlockSpec(par.shape, lambda bi, i: (0, 0))]
    args += [par]
    sshape = (2, DN_HEADS, DN_HEAD_DIM, DN_HEAD_DIM)
    in_specs += [pl.BlockSpec((None,) + sshape, lambda bi, i: (bi, 0, 0, 0, 0))]
    args += [s0]
    out_specs = [
        pl.BlockSpec((None, tr, DN_WIDTH), lambda bi, i: (bi, i, 0)),
        pl.BlockSpec((None, tr, DN_WIDTH), lambda bi, i: (bi, nt - 1 - i, 0)),
    ]
    out_shape = [jax.ShapeDtypeStruct((b, t, DN_WIDTH), qkv.dtype)] * 2
    if emit_state:
        out_specs += [pl.BlockSpec((None,) + sshape, lambda bi, i: (bi, 0, 0, 0, 0))]
        out_shape += [jax.ShapeDtypeStruct((b,) + sshape, F32)]
    cw = DN_HEADS * c
    scratch = [
        pltpu.VMEM(sshape, F32),
        pltpu.VMEM((2, GDN_TILE_CHUNKS, cw, 2 * DN_HEAD_DIM), F32),
        pltpu.VMEM((2, GDN_TILE_CHUNKS, c, cw), F32),
    ]
    return pl.pallas_call(
        functools.partial(_gdn_kernel, nt=nt, emit_state=emit_state),
        grid=(b, nt),
        in_specs=in_specs,
        out_specs=out_specs,
        out_shape=out_shape,
        scratch_shapes=scratch,
        compiler_params=_params("parallel", "arbitrary"),
        name="gdn_ctx" if emit_state else "gdn_lat",
    )(*args)


def _na_kernel(q_ref, k_ref, v_ref, kc_ref, vc_ref, bias_ref, o_ref, *, rows):
    w = GRID_W
    nl = NA_WIN_R * w
    nr = NA_ROWS_PER_ITER
    scale = NA_HEAD_DIM ** -0.5
    kc = kc_ref[...].astype(BF16)
    vc = vc_ref[...].astype(BF16)

    def stack_heads(q2):
        sel = lax.broadcasted_iota(jnp.int32, q2.shape, 1) < NA_HEAD_DIM
        zero = jnp.zeros_like(q2)
        return jnp.concatenate([jnp.where(sel, q2, zero), jnp.where(sel, zero, q2)], axis=0)

    def unstack_heads(x):
        n = x.shape[0] // 2
        sel = lax.broadcasted_iota(jnp.int32, (n, LANE), 1) < NA_HEAD_DIM
        return jnp.where(sel, jnp.broadcast_to(x[:n], (n, LANE)), jnp.broadcast_to(x[n:], (n, LANE)))

    def row_block(it, carry):
        rr = [it * nr + u for u in range(nr)]
        rss = [jnp.clip(r - NA_WIN_R // 2, 0, rows - NA_WIN_R) for r in rr]
        rsl = [pl.ds(pl.multiple_of(r * w, w), w) for r in rr]
        wsl = [pl.ds(pl.multiple_of(rs * w, w), nl) for rs in rss]
        q_all = q_ref[pl.ds(pl.multiple_of(it * (nr * w), nr * w), nr * w), :] * scale
        s_c = _dot_nt(stack_heads(q_all), kc)
        s = [_dot_nt(stack_heads(q_all[u * w:(u + 1) * w]), k_ref[wsl[u], :]) + bias_ref[rss[u] - rr[u] + NA_WIN_R - 1]
             for u in range(nr)]
        m_c = jnp.max(s_c, axis=-1, keepdims=True)
        m_l = [jnp.max(x, axis=-1, keepdims=True) for x in s]
        p_c = jnp.exp(s_c - m_c)
        p = [jnp.exp(x - m) for x, m in zip(s, m_l)]
        o_c = unstack_heads(_dot(p_c, vc))
        o_l = [_dot(p[u], v_ref[wsl[u], :]) for u in range(nr)]
        l_c = unstack_heads(jnp.sum(p_c, axis=-1, keepdims=True))
        m_c = unstack_heads(m_c)
        for u in range(nr):
            usl = slice(u * w, (u + 1) * w)
            l_u = unstack_heads(jnp.sum(p[u], axis=-1, keepdims=True))
            m_u = unstack_heads(m_l[u])
            m = jnp.maximum(m_u, m_c[usl])
            a_l = jnp.exp(m_u - m)
            a_c = jnp.exp(m_c[usl] - m)
            out = (unstack_heads(o_l[u]) * a_l + o_c[usl] * a_c) / (l_u * a_l + l_c[usl] * a_c)
            o_ref[rsl[u], :] = out.astype(o_ref.dtype)
        return carry

    lax.fori_loop(0, rows // nr, row_block, 0)


def _na_call(p, pc, bias):
    b, t, _ = p.shape
    l = pc.shape[1]
    rows = t // GRID_W
    npair = NA_HEADS // 2
    return pl.pallas_call(
        functools.partial(_na_kernel, rows=rows),
        grid=(npair, b),
        in_specs=[
            pl.BlockSpec((None, t, LANE), lambda j, bi: (bi, 0, NA_BLK0 + j)),
            pl.BlockSpec((None, t, LANE), lambda j, bi: (bi, 0, NA_BLK0 + npair + j)),
            pl.BlockSpec((None, t, LANE), lambda j, bi: (bi, 0, NA_BLK0 + 2 * npair + j)),
            pl.BlockSpec((None, l, LANE), lambda j, bi: (bi, 0, NA_BLK0 + npair + j)),
            pl.BlockSpec((None, l, LANE), lambda j, bi: (bi, 0, NA_BLK0 + 2 * npair + j)),
            pl.BlockSpec((None,) + bias.shape[1:], lambda j, bi: (j, 0, 0, 0)),
        ],
        out_specs=pl.BlockSpec((None, t, LANE), lambda j, bi: (bi, 0, j)),
        out_shape=jax.ShapeDtypeStruct((b, t, NA_WIDTH), p.dtype),
        compiler_params=_params("parallel", "parallel"),
        name="na_lat",
    )(p, p, p, pc, pc, bias)


def _ctx_attn_kernel(q_ref, k_ref, v_ref, o_ref):
    scale = NA_HEAD_DIM ** -0.5
    q2 = q_ref[...] * scale
    k2 = k_ref[...]
    v2 = v_ref[...]
    head0 = lax.broadcasted_iota(jnp.int32, q2.shape, 1) < NA_HEAD_DIM
    out = None
    for hh in range(2):
        sel = head0 if hh == 0 else ~head0
        s = _dot_nt(jnp.where(sel, q2, jnp.zeros_like(q2)), k2)
        p = jnp.exp(s - jnp.max(s, axis=-1, keepdims=True))
        o = _dot(p, v2) / jnp.sum(p, axis=-1, keepdims=True)
        out = o if out is None else jnp.where(sel, o, out)
    o_ref[...] = out.astype(o_ref.dtype)


def _ctx_attn_call(pc):
    b, l, _ = pc.shape
    npair = NA_HEADS // 2
    return pl.pallas_call(
        _ctx_attn_kernel,
        grid=(b, npair),
        in_specs=[
            pl.BlockSpec((None, l, LANE), lambda bi, j: (bi, 0, NA_BLK0 + j)),
            pl.BlockSpec((None, l, LANE), lambda bi, j: (bi, 0, NA_BLK0 + npair + j)),
            pl.BlockSpec((None, l, LANE), lambda bi, j: (bi, 0, NA_BLK0 + 2 * npair + j)),
        ],
        out_specs=pl.BlockSpec((None, l, LANE), lambda bi, j: (bi, 0, j)),
        out_shape=jax.ShapeDtypeStruct((b, l, NA_WIDTH), pc.dtype),
        compiler_params=_params("parallel", "parallel"),
        name="ctx_attn",
    )(pc, pc, pc)


def _merge_kernel(of_ref, ob_ref, z_ref, bg_ref, cg_ref, hh_ref, cgp_ref, cgn_ref, hhp_ref, hhn_ref, yc_ref,
                  ga_ref, gb_ref, gcg_ref, x_ref, gt_ref, wpa_ref, wpb_ref, wpc_ref, wo_ref, ng_ref, scw_ref,
                  o_ref, *, nt):
    i = pl.program_id(1)
    tm = x_ref.shape[0]
    o = of_ref[...].astype(F32) + ob_ref[...].astype(F32)
    sz = z_ref[...].astype(F32)
    ya = []
    for h in range(DN_HEADS):
        sl = slice(h * DN_HEAD_DIM, (h + 1) * DN_HEAD_DIM)
        ya.append(_rms(o[:, sl]) * ng_ref[...] * sz[:, sl])
    y_a = jnp.concatenate(ya, axis=1)
    has_prev = (i > 0).astype(F32)
    has_next = (i < nt - 1).astype(F32)
    f32 = lambda r: r[...].astype(F32)
    full = jnp.concatenate([f32(cgp_ref) * f32(hhp_ref) * has_prev, f32(cg_ref) * f32(hh_ref),
                            f32(cgn_ref) * f32(hhn_ref) * has_next], axis=0)
    y_b = f32(bg_ref) * _dwconv3_rows(full, scw_ref, tm, HALO_P)
    y = (f32(ga_ref) * _dot(y_a, wpa_ref[...]) + f32(gb_ref) * _dot(y_b, wpb_ref[...])
         + f32(gcg_ref) * _dot(yc_ref[...], wpc_ref[...]))
    o_ref[...] = x_ref[...] + gt_ref[...] * _dot(y, wo_ref[...])


def _merge_call(of, ob, p, yc, x, mod, mrow, wpa, wpb, wpc, wo, ng, scw, tm):
    b, t, d = x.shape
    nt = t // tm
    hb = tm // HALO_P
    nhb = t // HALO_P
    w5 = SC_WIDTH

    def tile(width, col):
        return pl.BlockSpec((None, tm, width), lambda bi, i: (bi, i, col))

    def prev(col):
        return pl.BlockSpec((None, HALO_P, w5), lambda bi, i: (bi, jnp.maximum(i * hb - 1, 0), col))

    def nxt(col):
        return pl.BlockSpec((None, HALO_P, w5), lambda bi, i: (bi, jnp.minimum(i * hb + hb, nhb - 1), col))

    def full(a):
        return pl.BlockSpec(a.shape, lambda bi, i: (0,) * a.ndim)

    in_specs = [
        tile(DN_WIDTH, 0), tile(DN_WIDTH, 0),
        tile(w5, 3), tile(w5, 4), tile(w5, 5), tile(w5, 6),
        prev(5), nxt(5), prev(6), nxt(6),
        tile(NA_WIDTH, 0),
        tile(d, 5), tile(d, 6), tile(d, 7),
        tile(d, 0),
        pl.BlockSpec((None, 1, d), lambda bi, i: (mrow(bi), 0, 2)),
        full(wpa), full(wpb), full(wpc), full(wo), full(ng), full(scw),
    ]
    return pl.pallas_call(
        functools.partial(_merge_kernel, nt=nt),
        grid=(b, nt),
        in_specs=in_specs,
        out_specs=pl.BlockSpec((None, tm, d), lambda bi, i: (bi, i, 0)),
        out_shape=jax.ShapeDtypeStruct((b, t, d), F32),
        compiler_params=_params("parallel", "parallel"),
        name="merge",
    )(of, ob, p, p, p, p, p, p, p, p, yc, p, p, p, x, mod, wpa, wpb, wpc, wo, ng, scw)


def _ffn_kernel(*refs, nt, nk, fc, final):
    it = iter(refs)
    x_ref, xp_ref, xn_ref, g_ref, sh_ref, sc_ref, gt_ref = (next(it) for _ in range(7))
    wup_ref, cw_ref, cb_ref, wd_ref = (next(it) for _ in range(4))
    fg_ref = next(it) if final else None
    o_ref, h_scr, act_scr = next(it), next(it), next(it)
    i = pl.program_id(1)
    tm = x_ref.shape[0]
    dff = nk * fc

    xfull = jnp.concatenate([xp_ref[...], x_ref[...], xn_ref[...]], axis=0)
    h = (_rms(xfull) * g_ref[...]) * (1.0 + sc_ref[...]) + sh_ref[...]
    h_scr[...] = h.astype(BF16)

    row = lax.broadcasted_iota(jnp.int32, (tm + 2 * HALO, 1), 0)
    keep = ((row >= HALO) | (i > 0)) & ((row < tm + HALO) | (i < nt - 1))

    def up_proj(k):
        return [jnp.where(keep, jnp.dot(h_scr[...], wup_ref[:, c0:c0 + fc], preferred_element_type=F32), 0.0)
                for c0 in (k * fc, dff + k * fc)]

    ups = up_proj(0)
    g0 = 0
    for k in range(nk):
        nxt = up_proj(k + 1) if k + 1 < nk else None
        a = _dwconv3_rows(ups[0], cw_ref[:, k * fc:(k + 1) * fc], tm) + cb_ref[:, k * fc:(k + 1) * fc]
        bb = (_dwconv3_rows(ups[1], cw_ref[:, dff + k * fc:dff + (k + 1) * fc], tm)
              + cb_ref[:, dff + k * fc:dff + (k + 1) * fc])
        act_scr[:, k * fc:(k + 1) * fc] = (_silu(a) * bb).astype(BF16)
        if (k + 1) % FFN_DOWN_GROUP == 0 or k == nk - 1:
            g1 = (k + 1) * fc
            contrib = jnp.dot(act_scr[:, g0:g1], wd_ref[g0:g1, :], preferred_element_type=F32)
            if g0 == 0:
                o_ref[...] = contrib
            else:
                o_ref[...] += contrib
            g0 = g1
        ups = nxt

    out = x_ref[...] + gt_ref[...] * o_ref[...]
    if final:
        out = _rms(out) * fg_ref[...]
    o_ref[...] = out


def _ffn_call(x, g, mod, mrow, w_up, cw, cb, w_down, tm, final_g=None):
    b, t, d = x.shape
    dff = w_down.shape[0]
    fc = 256
    nk = dff // fc
    nt = t // tm
    hb = tm // HALO
    nhb = t // HALO
    final = final_g is not None

    def resident(a):
        return pl.BlockSpec(a.shape, lambda bi, i: (0,) * a.ndim, pipeline_mode=pl.Buffered(1))

    in_specs = [
        pl.BlockSpec((None, tm, d), lambda bi, i: (bi, i, 0)),
        pl.BlockSpec((None, HALO, d), lambda bi, i: (bi, jnp.maximum(i * hb - 1, 0), 0)),
        pl.BlockSpec((None, HALO, d), lambda bi, i: (bi, jnp.minimum(i * hb + hb, nhb - 1), 0)),
        pl.BlockSpec((1, d), lambda bi, i: (0, 0)),
        pl.BlockSpec((None, 1, d), lambda bi, i: (mrow(bi), 0, 3)),
        pl.BlockSpec((None, 1, d), lambda bi, i: (mrow(bi), 0, 4)),
        pl.BlockSpec((None, 1, d), lambda bi, i: (mrow(bi), 0, 5)),
        resident(w_up), resident(cw), resident(cb), resident(w_down),
    ]
    args = [x, x, x, g, mod, mod, mod, w_up, cw, cb, w_down]
    if final:
        in_specs.append(pl.BlockSpec((1, d), lambda bi, i: (0, 0)))
        args.append(final_g)
    return pl.pallas_call(
        functools.partial(_ffn_kernel, nt=nt, nk=nk, fc=fc, final=final),
        grid=(b, nt),
        in_specs=in_specs,
        out_specs=pl.BlockSpec((None, tm, d), lambda bi, i: (bi, i, 0)),
        out_shape=jax.ShapeDtypeStruct((b, t, d), F32),
        scratch_shapes=[pltpu.VMEM((tm + 2 * HALO, d), BF16), pltpu.VMEM((tm, dff), BF16)],
        compiler_params=_params("parallel", "parallel"),
        name="ffn_final" if final else "ffn",
    )(*args)


def _rope_tables(t):
    tok = jnp.arange(t, dtype=jnp.int32)
    rows = (tok // GRID_W).astype(F32)
    cols = (tok % GRID_W).astype(F32)
    nf = DN_HEAD_DIM // 4
    inv = ROPE_BASE ** (-jnp.arange(nf, dtype=F32) / nf)
    ang = jnp.concatenate([rows[:, None] * inv, cols[:, None] * inv], axis=-1)
    cos = jnp.repeat(jnp.cos(ang), 2, axis=-1)
    sin = jnp.repeat(jnp.sin(ang), 2, axis=-1)
    sign = jnp.where(jnp.arange(DN_HEAD_DIM) % 2 == 0, -1.0, 1.0).astype(F32)
    return cos, sin * sign


def _na_bias_table(rpb):
    w = GRID_W
    depth = rpb.shape[0]
    qc = np.arange(w)[:, None]
    kc = np.arange(w)[None, :]
    cstart = np.clip(qc - NA_WIN_C // 2, 0, w - NA_WIN_C)
    valid = (kc >= cstart) & (kc < cstart + NA_WIN_C)
    onehot = ((kc - qc + NA_WIN_C - 1)[None] == np.arange(2 * NA_WIN_C - 1)[:, None, None]) & valid[None]
    tbl = jnp.einsum("lhdj,jck->lhdck", rpb, jnp.asarray(onehot, F32), precision=lax.Precision.HIGHEST)
    tbl = tbl + jnp.asarray(np.where(valid, 0.0, NEG), F32)
    tbl = jnp.stack([tbl[:, :, d0:d0 + NA_WIN_R] for d0 in range(NA_WIN_R)], axis=2)
    tbl = tbl.transpose(0, 1, 2, 4, 3, 5).reshape(depth, NA_HEADS // 2, 2, NA_WIN_R, w, NA_WIN_R * w)
    return tbl.transpose(0, 1, 3, 2, 4, 5).reshape(depth, NA_HEADS // 2, NA_WIN_R, 2 * w, NA_WIN_R * w)


def _permute_w_in(w_in):
    s = np.cumsum([0, 3 * DN_WIDTH, DN_WIDTH, 2 * DN_HEADS, 2 * DN_HEADS, 3 * SC_WIDTH, 3 * NA_WIDTH, 3 * 1024])
    qkv, z, a, b, sc, na, gates = (w_in[..., s[j]:s[j + 1]] for j in range(7))
    pad = jnp.zeros(w_in.shape[:-1] + (LANE - 4 * DN_HEADS,), w_in.dtype)
    return jnp.concatenate([qkv, z, sc, na, gates], axis=-1), jnp.concatenate([a, b, pad], axis=-1)


def kernel(x, c, ctx, c_ctx, norm1_g, norm2_g, w_ada, b_ada, w_in, dn_conv_w, dn_a_log, dn_dt_bias, dn_norm_g, sc_conv_w, na_rpb, w_pa, w_pb, w_pc, w_o, w_up, ffn_conv_w, ffn_conv_b, w_down, final_norm_g):
    bsz, t, d = x.shape
    depth = w_in.shape[0]
    assert d == 1024 and t % (NA_WIN_R * GRID_W) == 0 and ctx.shape[1] % (GDN_TILE_CHUNKS * DN_CHUNK) == 0

    mod_rows = -(-(bsz + 1) // SUBLANE) * SUBLANE
    cc = jnp.zeros((mod_rows, d), F32).at[:bsz].set(c).at[bsz].set(c_ctx)
    mod = _ada_call(cc, w_ada, b_ada).reshape(depth, mod_rows, 1, 6 * d)
    lat_row = lambda bi: bi
    ctx_row = lambda bi: bsz

    w_in_p, w_ab = (a.astype(BF16) for a in _permute_w_in(w_in))
    rope = _rope_tables(t)
    na_bias = _na_bias_table(na_rpb)
    par = jnp.zeros((depth, SUBLANE, LANE), F32)
    par = par.at[:, 0, :2 * DN_HEADS].set(dn_a_log.reshape(depth, -1)).at[:, 1, :2 * DN_HEADS].set(dn_dt_bias.reshape(depth, -1))
    s_zero = jnp.zeros((bsz, 2, DN_HEADS, DN_HEAD_DIM, DN_HEAD_DIM), F32)
    tm_lat = 1024 if t % 1024 == 0 else 512
    tm_ctx = ctx.shape[1]

    xc = ctx
    for l in range(depth):
        need_ctx = l < depth - 1
        g1 = norm1_g[l][None]
        g2 = norm2_g[l][None]
        wpa, wpb, wpc, wo = (a[l].astype(BF16) for a in (w_pa, w_pb, w_pc, w_o))
        wup, wdn = w_up[l].astype(BF16), w_down[l].astype(BF16)
        ng = dn_norm_g[l][None]
        p, ab = _inproj_call(x, g1, mod[l], lat_row, w_in_p[l], w_ab[l], tm_lat)
        lc_all = bsz * tm_ctx
        pc, abc = _inproj_call(xc.reshape(1, lc_all, d), g1, mod[l], ctx_row, w_in_p[l], w_ab[l],
                               tm_lat if lc_all % tm_lat == 0 else tm_ctx)
        pc, abc = pc.reshape(bsz, tm_ctx, -1), abc.reshape(bsz, tm_ctx, -1)
        ocf, ocb, s_ctx = _gdn_call(_gdn_prep_call(pc, dn_conv_w[l], None), abc, par[l], s_zero, True)
        olf, olb = _gdn_call(_gdn_prep_call(p, dn_conv_w[l], rope), ab, par[l], s_ctx, False)
        y_c = _na_call(p, pc, na_bias[l])
        x = _merge_call(olf, olb, p, y_c, x, mod[l], lat_row, wpa, wpb, wpc, wo, ng, sc_conv_w[l], 512)
        fg = final_norm_g[None] if l == depth - 1 else None
        x = _ffn_call(x, g2, mod[l], lat_row, wup, ffn_conv_w[l], ffn_conv_b[l][None], wdn, tm_lat, fg)
        if need_ctx:
            yc_c = _ctx_attn_call(pc)
            xc = _merge_call(ocf, ocb, pc, yc_c, xc, mod[l], ctx_row, wpa, wpb, wpc, wo, ng, sc_conv_w[l], tm_ctx)
            xc = _ffn_call(xc, g2, mod[l], ctx_row, wup, ffn_conv_w[l], ffn_conv_b[l][None], wdn, tm_ctx)
    return x
```

```python
import functools

import numpy as np
import jax
import jax.numpy as jnp
from jax import lax
from jax.experimental import pallas as pl
from jax.experimental.pallas import tpu as pltpu

F32 = jnp.float32
BF16 = jnp.bfloat16

GRID_W = 64
DN_HEADS = 4
DN_HEAD_DIM = 128
DN_WIDTH = DN_HEADS * DN_HEAD_DIM
DN_CHUNK = 64
GDN_TILE_CHUNKS = 4
FFN_DOWN_GROUP = 4
SC_WIDTH = 512
NA_HEADS = 8
NA_HEAD_DIM = 64
NA_WIDTH = NA_HEADS * NA_HEAD_DIM
NA_WIN_R = 8
NA_WIN_C = 16
NA_ROWS_PER_ITER = 8
ROPE_BASE = 10000.0
EPS = 1e-6
NEG = -1e30

LANE = 128
SUBLANE = 8
HALO = SUBLANE
HALO_P = 2 * SUBLANE
VMEM_LIMIT = 48 * 1024 * 1024

P_COLS = 3 * DN_WIDTH + DN_WIDTH + 3 * SC_WIDTH + 3 * NA_WIDTH + 3 * 1024
P_TILE = 2048
Z_COL0 = 3 * DN_WIDTH
GATE_COL0 = P_COLS - 3 * 1024
NA_BLK0 = (3 * DN_WIDTH + DN_WIDTH + 3 * SC_WIDTH) // LANE


def _dot(a, b):
    return jnp.dot(a.astype(BF16), b.astype(BF16), preferred_element_type=F32)


def _dot_nt(a, b):
    return lax.dot_general(a.astype(BF16), b.astype(BF16), (((1,), (1,)), ((), ())), preferred_element_type=F32)


def _dot_tn(a, b):
    return lax.dot_general(a.astype(BF16), b.astype(BF16), (((0,), (0,)), ((), ())), preferred_element_type=F32)


def _sigmoid(x):
    return 0.5 * jnp.tanh(0.5 * x) + 0.5


def _silu(x):
    return x * _sigmoid(x)


def _softplus(x):
    return jnp.maximum(x, 0.0) + jnp.log1p(jnp.exp(-jnp.abs(x)))


def _rms(x, eps=EPS):
    return x * lax.rsqrt(jnp.mean(x * x, axis=-1, keepdims=True) + eps)


def _params(*sem):
    return pltpu.CompilerParams(dimension_semantics=sem, vmem_limit_bytes=VMEM_LIMIT)


def _dwconv3_rows(full, w_ref, n, halo=HALO):
    rows = full.shape[0]
    dn = pltpu.roll(full, 1, axis=0)
    up = pltpu.roll(full, rows - 1, axis=0)
    out = dn * w_ref[0:1, :] + full * w_ref[1:2, :] + up * w_ref[2:3, :]
    return out[halo:halo + n]


def _ada_kernel(c_ref, w_ref, b_ref, o_ref):
    o_ref[...] = _dot(_silu(c_ref[...]), w_ref[...]) + b_ref[...]


def _ada_call(cc, w_ada, b_ada):
    depth, d, n6 = w_ada.shape
    r = cc.shape[0]
    tn = 1536
    return pl.pallas_call(
        _ada_kernel,
        grid=(depth, n6 // tn),
        in_specs=[
            pl.BlockSpec((r, d), lambda l, j: (0, 0)),
            pl.BlockSpec((None, d, tn), lambda l, j: (l, 0, j)),
            pl.BlockSpec((None, 1, tn), lambda l, j: (l, 0, j)),
        ],
        out_specs=pl.BlockSpec((None, r, tn), lambda l, j: (l, 0, j)),
        out_shape=jax.ShapeDtypeStruct((depth, r, n6), F32),
        compiler_params=_params("parallel", "parallel"),
        name="ada_mod",
    )(cc, w_ada, b_ada.reshape(depth, 1, n6))


def _inproj_kernel(x_ref, g_ref, sh_ref, sc_ref, w_ref, wab_ref, o_ref, ab_ref, h_ref):
    j = pl.program_id(2)

    @pl.when(j == 0)
    def _():
        h = (_rms(x_ref[...]) * g_ref[...]) * (1.0 + sc_ref[...]) + sh_ref[...]
        h_ref[...] = h.astype(BF16)
        ab_ref[...] = jnp.dot(h_ref[...], wab_ref[...], preferred_element_type=F32)

    res = jnp.dot(h_ref[...], w_ref[...], preferred_element_type=F32)
    sg = _sigmoid(res)
    col = j * res.shape[1] + lax.broadcasted_iota(jnp.int32, (1, res.shape[1]), 1)
    is_z = (col >= Z_COL0) & (col < Z_COL0 + DN_WIDTH)
    out = jnp.where(col >= GATE_COL0, sg, jnp.where(is_z, res * sg, res))
    o_ref[...] = out.astype(o_ref.dtype)


def _inproj_call(x, g, mod, mrow, w, wab, tm):
    b, t, d = x.shape
    n = w.shape[1]
    tn = P_TILE
    return pl.pallas_call(
        _inproj_kernel,
        grid=(b, t // tm, n // tn),
        in_specs=[
            pl.BlockSpec((None, tm, d), lambda bi, i, j: (bi, i, 0)),
            pl.BlockSpec((1, d), lambda bi, i, j: (0, 0)),
            pl.BlockSpec((None, 1, d), lambda bi, i, j: (mrow(bi), 0, 0)),
            pl.BlockSpec((None, 1, d), lambda bi, i, j: (mrow(bi), 0, 1)),
            pl.BlockSpec((d, tn), lambda bi, i, j: (0, j)),
            pl.BlockSpec((d, LANE), lambda bi, i, j: (0, 0)),
        ],
        out_specs=[pl.BlockSpec((None, tm, tn), lambda bi, i, j: (bi, i, j)),
                   pl.BlockSpec((None, tm, LANE), lambda bi, i, j: (bi, i, 0))],
        out_shape=[jax.ShapeDtypeStruct((b, t, n), BF16), jax.ShapeDtypeStruct((b, t, LANE), F32)],
        scratch_shapes=[pltpu.VMEM((tm, d), BF16)],
        compiler_params=_params("parallel", "parallel", "arbitrary"),
        name="inproj",
    )(x, g, mod, mod, w, wab)


def _swap_pairs(x, even):
    n = x.shape[-1]
    return jnp.where(even, pltpu.roll(x, n - 1, axis=1), pltpu.roll(x, 1, axis=1))


def _gdn_prep_kernel(*refs, nt, use_rope):
    x_ref, xp_ref, xn_ref, cw_ref = refs[:4]
    cos_ref, sin_ref = refs[4:6] if use_rope else (None, None)
    o_ref = refs[-1]
    i = pl.program_id(1)
    tr = x_ref.shape[0]
    hd = DN_HEAD_DIM
    has_prev = (i > 0).astype(F32)
    has_next = (i < nt - 1).astype(F32)
    full = jnp.concatenate([xp_ref[...].astype(F32) * has_prev, x_ref[...].astype(F32),
                            xn_ref[...].astype(F32) * has_next], axis=0)
    act = _silu(_dwconv3_rows(full, cw_ref, tr, HALO_P))
    even = (lax.broadcasted_iota(jnp.int32, (tr, hd), 1) % 2) == 0
    for off, scale in ((0, hd ** -0.5), (DN_WIDTH, 1.0)):
        for h in range(DN_HEADS):
            sl = slice(off + h * hd, off + (h + 1) * hd)
            xh = act[:, sl]
            xh = xh * lax.rsqrt(jnp.sum(xh * xh, axis=-1, keepdims=True) + EPS)
            if use_rope:
                xh = xh * cos_ref[...] + _swap_pairs(xh, even) * sin_ref[...]
            o_ref[:, sl] = (xh * scale).astype(o_ref.dtype)
    o_ref[:, 2 * DN_WIDTH:] = act[:, 2 * DN_WIDTH:].astype(o_ref.dtype)


def _gdn_prep_call(p, conv_w, rope):
    b, t, _ = p.shape
    tr = GDN_TILE_CHUNKS * DN_CHUNK
    nt = t // tr
    hb = tr // HALO_P
    nhb = t // HALO_P
    qkv_w = 3 * DN_WIDTH
    in_specs = [
        pl.BlockSpec((None, tr, qkv_w), lambda bi, i: (bi, i, 0)),
        pl.BlockSpec((None, HALO_P, qkv_w), lambda bi, i: (bi, jnp.maximum(i * hb - 1, 0), 0)),
        pl.BlockSpec((None, HALO_P, qkv_w), lambda bi, i: (bi, jnp.minimum(i * hb + hb, nhb - 1), 0)),
        pl.BlockSpec(conv_w.shape, lambda bi, i: (0, 0)),
    ]
    args = [p, p, p, conv_w]
    if rope is not None:
        in_specs += [pl.BlockSpec((tr, DN_HEAD_DIM), lambda bi, i: (i, 0))] * 2
        args += [rope[0], rope[1]]
    return pl.pallas_call(
        functools.partial(_gdn_prep_kernel, nt=nt, use_rope=rope is not None),
        grid=(b, nt),
        in_specs=in_specs,
        out_specs=pl.BlockSpec((None, tr, qkv_w), lambda bi, i: (bi, i, 0)),
        out_shape=jax.ShapeDtypeStruct((b, t, qkv_w), p.dtype),
        compiler_params=_params("parallel", "parallel"),
        name="gdn_prep",
    )(*args)


def _gdn_kernel(*refs, nt, emit_state):
    it = iter(refs)
    xs = [[next(it) for _ in range(2)] for _ in range(2)]
    par_ref = next(it)
    s0_ref = next(it)
    o_refs = [next(it), next(it)]
    sout_ref = next(it) if emit_state else None
    s_scr, uw_scr, qkd_scr = (next(it) for _ in range(3))

    c = DN_CHUNK
    nh = DN_HEADS
    hd = DN_HEAD_DIM
    nch = GDN_TILE_CHUNKS
    tr = nch * c
    cw = nh * c
    lc = c.bit_length() - 1
    i = pl.program_id(1)

    @pl.when(i == 0)
    def _():
        s_scr[...] = s0_ref[...]

    ri = lax.broadcasted_iota(jnp.int32, (c, cw), 0)
    li = lax.broadcasted_iota(jnp.int32, (c, cw), 1)
    jj = li & (c - 1)
    seg = li >> lc
    eye_cat = ri == jj
    bd_sq = jnp.where((lax.broadcasted_iota(jnp.int32, (cw, cw), 0) >> lc)
                      == (lax.broadcasted_iota(jnp.int32, (cw, cw), 1) >> lc), 1.0, 0.0).astype(BF16)
    bd_k = jnp.where((lax.broadcasted_iota(jnp.int32, (cw, nh * hd), 0) >> lc)
                     == (lax.broadcasted_iota(jnp.int32, (cw, nh * hd), 1) // hd), 1.0, 0.0).astype(BF16)
    rowi = lax.broadcasted_iota(jnp.int32, (tr, LANE), 0) & (c - 1)
    neg_a = -jnp.exp(par_ref[0:1, :])
    dt_b = par_ref[1:2, :]

    def col_bcast(x, j, width):
        return jnp.broadcast_to(x[:, j:j + 1], (x.shape[0], width))

    def colcat(x, base):
        out = col_bcast(x, base + nh - 1, cw)
        for h in range(nh - 2, -1, -1):
            out = jnp.where(seg == h, col_bcast(x, base + h, cw), out)
        return out

    def block_diag(m, mask):
        return jnp.concatenate([m.astype(BF16)] * nh, axis=0) * mask

    nar = []
    for d in range(2):
        ab = xs[d][1][...]
        gc = neg_a * _softplus(ab + dt_b)
        beta = pltpu.roll(_sigmoid(ab), LANE - 2 * nh, axis=1)
        sft = 1
        while sft < c:
            if d == 0:
                gc = gc + jnp.where(rowi >= sft, pltpu.roll(gc, sft, axis=0), 0.0)
            else:
                gc = gc + jnp.where(rowi < c - sft, pltpu.roll(gc, tr - sft, axis=0), 0.0)
            sft *= 2
        e1 = jnp.exp(gc)
        nar.append(dict(gc=gc, e1=e1, beta=beta, be=beta * e1))

    st = {}

    def setup(key):
        d, ch = key
        rs = slice(ch * c, (ch + 1) * c)
        base = d * nh
        incl = (ri >= jj) if d == 0 else (ri <= jj)
        gcc = colcat(nar[d]["gc"][rs], base)
        gcr = jnp.broadcast_to(jnp.sum(jnp.where(eye_cat, gcc, 0.0), axis=0, keepdims=True), (c, cw))
        decay = jnp.where(incl, jnp.exp(jnp.where(incl, gcc - gcr, 0.0)), 0.0)
        kf = xs[d][0][rs, DN_WIDTH:2 * DN_WIDTH]
        prod = _dot_nt(jnp.concatenate([xs[d][0][rs, :DN_WIDTH], kf], axis=0), block_diag(kf, bd_k))
        qkd_scr[d, ch] = prod[:c] * decay
        amat = jnp.where(incl & (~eye_cat), colcat(nar[d]["beta"][rs], base) * prod[c:] * decay, 0.0)
        lvl1 = ((ri >> 1) == (jj >> 1))
        st[key] = dict(amat=amat.astype(BF16), tinv=jnp.where(eye_cat, 1.0, 0.0) - jnp.where(lvl1, amat, 0.0))

    def invert_level(keys, s):
        ls = s.bit_length() - 1
        lvl = jnp.where(((ri >> (ls + 1)) == (jj >> (ls + 1))) & ((ri >> ls) != (jj >> ls)), 1.0, 0.0).astype(BF16)
        for key in keys:
            p = st[key]
            p["t_bf"] = p["tinv"].astype(BF16)
            p["x"] = _dot(p["t_bf"], block_diag(p["amat"] * lvl, bd_sq))
        for key in keys:
            p = st[key]
            p["tinv"] = p["tinv"] - _dot(p["x"], block_diag(p["t_bf"], bd_sq))

    def solve(key):
        d, ch = key
        rs = slice(ch * c, (ch + 1) * c)
        base = d * nh
        x_ref = xs[d][0]
        rhs = jnp.concatenate([
            jnp.concatenate([x_ref[rs, 2 * DN_WIDTH + h * hd:2 * DN_WIDTH + (h + 1) * hd].astype(F32)
                             * col_bcast(nar[d]["beta"][rs], base + h, hd),
                             x_ref[rs, DN_WIDTH + h * hd:DN_WIDTH + (h + 1) * hd].astype(F32)
                             * col_bcast(nar[d]["be"][rs], base + h, hd)], axis=1)
            for h in range(nh)], axis=0)
        uw_scr[d, ch] = _dot(block_diag(st.pop(key)["tinv"], bd_sq), rhs)

    states = [[s_scr[d, h] for h in range(nh)] for d in range(2)]

    def scan_step(step):
        cur = [(0, step), (1, nch - 1 - step)]
        loc = {}
        for d, ch in cur:
            rs = slice(ch * c, (ch + 1) * c)
            base = d * nh
            gc_c = nar[d]["gc"][rs]
            gtot = gc_c[c - 1:c] if d == 0 else gc_c[0:1]
            e2 = jnp.exp(gtot - gc_c)
            r_ = []
            for h in range(nh):
                hs = slice(h * hd, (h + 1) * hd)
                w_h = uw_scr[d, ch, h * c:(h + 1) * c, hd:]
                qd_h = xs[d][0][rs, hs].astype(F32) * col_bcast(nar[d]["e1"][rs], base + h, hd)
                r_.append(_dot(jnp.concatenate([w_h, qd_h], axis=0), states[d][h]))
            loc[d] = dict(r=r_, e2=e2, gtot=gtot)
        for d, ch in cur:
            loc[d]["vn"] = [uw_scr[d, ch, h * c:(h + 1) * c, :hd] - loc[d]["r"][h][:c] for h in range(nh)]
        for d, ch in cur:
            rs = slice(ch * c, (ch + 1) * c)
            base = d * nh
            vn = loc[d]["vn"]
            o2 = _dot(block_diag(qkd_scr[d, ch], bd_sq), jnp.concatenate(vn, axis=0))
            for h in range(nh):
                hs = slice(h * hd, (h + 1) * hd)
                kd_h = (xs[d][0][rs, DN_WIDTH + h * hd:DN_WIDTH + (h + 1) * hd].astype(F32)
                        * col_bcast(loc[d]["e2"], base + h, hd))
                o_refs[d][rs, hs] = (loc[d]["r"][h][c:] + o2[h * c:(h + 1) * c]).astype(o_refs[d].dtype)
                eg = jnp.exp(jnp.broadcast_to(loc[d]["gtot"][0:1, base + h:base + h + 1], (hd, hd)))
                states[d][h] = states[d][h] * eg + _dot_tn(kd_h, vn[h])

    probs = [(d, ch) for ch in range(nch) for d in range(2)]
    for key in probs:
        setup(key)
    for e in range(1, lc):
        invert_level(probs, 1 << e)
    for key in probs:
        solve(key)
    for step in range(nch):
        scan_step(step)
    for d in range(2):
        for h in range(nh):
            s_scr[d, h] = states[d][h]

    if emit_state:
        @pl.when(i == nt - 1)
        def _():
            sout_ref[...] = s_scr[...]


def _gdn_call(qkv, ab, par, s0, emit_state):
    b, t, qkv_w = qkv.shape
    c = DN_CHUNK
    tr = GDN_TILE_CHUNKS * c
    nt = t // tr

    def tile_of(d, i):
        return i if d == 0 else nt - 1 - i

    in_specs = []
    args = []
    for d in range(2):
        in_specs += [
            pl.BlockSpec((None, tr, qkv_w), lambda bi, i, d=d: (bi, tile_of(d, i), 0)),
            pl.BlockSpec((None, tr, LANE), lambda bi, i, d=d: (bi, tile_of(d, i), 0)),
        ]
        args += [qkv, ab]
    in_specs += [pl.BlockSpec(par.shape, lambda bi, i: (0, 0))]
    args += [par]
    sshape = (2, DN_HEADS, DN_HEAD_DIM, DN_HEAD_DIM)
    in_specs += [pl.BlockSpec((None,) + sshape, lambda bi, i: (bi, 0, 0, 0, 0))]
    args += [s0]
    out_specs = [
        pl.BlockSpec((None, tr, DN_WIDTH), lambda bi, i: (bi, i, 0)),
        pl.BlockSpec((None, tr, DN_WIDTH), lambda bi, i: (bi, nt - 1 - i, 0)),
    ]
    out_shape = [jax.ShapeDtypeStruct((b, t, DN_WIDTH), qkv.dtype)] * 2
    if emit_state:
        out_specs += [pl.BlockSpec((None,) + sshape, lambda bi, i: (bi, 0, 0, 0, 0))]
        out_shape += [jax.ShapeDtypeStruct((b,) + sshape, F32)]
    cw = DN_HEADS * c
    scratch = [
        pltpu.VMEM(sshape, F32),
        pltpu.VMEM((2, GDN_TILE_CHUNKS, cw, 2 * DN_HEAD_DIM), F32),
        pltpu.VMEM((2, GDN_TILE_CHUNKS, c, cw), F32),
    ]
    return pl.pallas_call(
        functools.partial(_gdn_kernel, nt=nt, emit_state=emit_state),
        grid=(b, nt),
        in_specs=in_specs,
        out_specs=out_specs,
        out_shape=out_shape,
        scratch_shapes=scratch,
        compiler_params=_params("parallel", "arbitrary"),
        name="gdn_ctx" if emit_state else "gdn_lat",
    )(*args)


def _na_kernel(q_ref, k_ref, v_ref, kc_ref, vc_ref, bias_ref, o_ref, *, rows):
    w = GRID_W
    nl = NA_WIN_R * w
    nr = NA_ROWS_PER_ITER
    scale = NA_HEAD_DIM ** -0.5
    kc = kc_ref[...].astype(BF16)
    vc = vc_ref[...].astype(BF16)

    def stack_heads(q2):
        sel = lax.broadcasted_iota(jnp.int32, q2.shape, 1) < NA_HEAD_DIM
        zero = jnp.zeros_like(q2)
        return jnp.concatenate([jnp.where(sel, q2, zero), jnp.where(sel, zero, q2)], axis=0)

    def unstack_heads(x):
        n = x.shape[0] // 2
        sel = lax.broadcasted_iota(jnp.int32, (n, LANE), 1) < NA_HEAD_DIM
        return jnp.where(sel, jnp.broadcast_to(x[:n], (n, LANE)), jnp.broadcast_to(x[n:], (n, LANE)))

    def row_block(it, carry):
        rr = [it * nr + u for u in range(nr)]
        rss = [jnp.clip(r - NA_WIN_R // 2, 0, rows - NA_WIN_R) for r in rr]
        rsl = [pl.ds(pl.multiple_of(r * w, w), w) for r in rr]
        wsl = [pl.ds(pl.multiple_of(rs * w, w), nl) for rs in rss]
        q_all = q_ref[pl.ds(pl.multiple_of(it * (nr * w), nr * w), nr * w), :] * scale
        s_c = _dot_nt(stack_heads(q_all), kc)
        s = [_dot_nt(stack_heads(q_all[u * w:(u + 1) * w]), k_ref[wsl[u], :]) + bias_ref[rss[u] - rr[u] + NA_WIN_R - 1]
             for u in range(nr)]
        m_c = jnp.max(s_c, axis=-1, keepdims=True)
        m_l = [jnp.max(x, axis=-1, keepdims=True) for x in s]
        p_c = jnp.exp(s_c - m_c)
        p = [jnp.exp(x - m) for x, m in zip(s, m_l)]
        o_c = unstack_heads(_dot(p_c, vc))
        o_l = [_dot(p[u], v_ref[wsl[u], :]) for u in range(nr)]
        l_c = unstack_heads(jnp.sum(p_c, axis=-1, keepdims=True))
        m_c = unstack_heads(m_c)
        for u in range(nr):
            usl = slice(u * w, (u + 1) * w)
            l_u = unstack_heads(jnp.sum(p[u], axis=-1, keepdims=True))
            m_u = unstack_heads(m_l[u])
            m = jnp.maximum(m_u, m_c[usl])
            a_l = jnp.exp(m_u - m)
            a_c = jnp.exp(m_c[usl] - m)
            out = (unstack_heads(o_l[u]) * a_l + o_c[usl] * a_c) / (l_u * a_l + l_c[usl] * a_c)
            o_ref[rsl[u], :] = out.astype(o_ref.dtype)
        return carry

    lax.fori_loop(0, rows // nr, row_block, 0)


def _na_call(p, pc, bias):
    b, t, _ = p.shape
    l = pc.shape[1]
    rows = t // GRID_W
    npair = NA_HEADS // 2
    return pl.pallas_call(
        functools.partial(_na_kernel, rows=rows),
        grid=(npair, b),
        in_specs=[
            pl.BlockSpec((None, t, LANE), lambda j, bi: (bi, 0, NA_BLK0 + j)),
            pl.BlockSpec((None, t, LANE), lambda j, bi: (bi, 0, NA_BLK0 + npair + j)),
            pl.BlockSpec((None, t, LANE), lambda j, bi: (bi, 0, NA_BLK0 + 2 * npair + j)),
            pl.BlockSpec((None, l, LANE), lambda j, bi: (bi, 0, NA_BLK0 + npair + j)),
            pl.BlockSpec((None, l, LANE), lambda j, bi: (bi, 0, NA_BLK0 + 2 * npair + j)),
            pl.BlockSpec((None,) + bias.shape[1:], lambda j, bi: (j, 0, 0, 0)),
        ],
        out_specs=pl.BlockSpec((None, t, LANE), lambda j, bi: (bi, 0, j)),
        out_shape=jax.ShapeDtypeStruct((b, t, NA_WIDTH), p.dtype),
        compiler_params=_params("parallel", "parallel"),
        name="na_lat",
    )(p, p, p, pc, pc, bias)


def _ctx_attn_kernel(q_ref, k_ref, v_ref, o_ref):
    scale = NA_HEAD_DIM ** -0.5
    q2 = q_ref[...] * scale
    k2 = k_ref[...]
    v2 = v_ref[...]
    head0 = lax.broadcasted_iota(jnp.int32, q2.shape, 1) < NA_HEAD_DIM
    out = None
    for hh in range(2):
        sel = head0 if hh == 0 else ~head0
        s = _dot_nt(jnp.where(sel, q2, jnp.zeros_like(q2)), k2)
        p = jnp.exp(s - jnp.max(s, axis=-1, keepdims=True))
        o = _dot(p, v2) / jnp.sum(p, axis=-1, keepdims=True)
        out = o if out is None else jnp.where(sel, o, out)
    o_ref[...] = out.astype(o_ref.dtype)


def _ctx_attn_call(pc):
    b, l, _ = pc.shape
    npair = NA_HEADS // 2
    return pl.pallas_call(
        _ctx_attn_kernel,
        grid=(b, npair),
        in_specs=[
            pl.BlockSpec((None, l, LANE), lambda bi, j: (bi, 0, NA_BLK0 + j)),
            pl.BlockSpec((None, l, LANE), lambda bi, j: (bi, 0, NA_BLK0 + npair + j)),
            pl.BlockSpec((None, l, LANE), lambda bi, j: (bi, 0, NA_BLK0 + 2 * npair + j)),
        ],
        out_specs=pl.BlockSpec((None, l, LANE), lambda bi, j: (bi, 0, j)),
        out_shape=jax.ShapeDtypeStruct((b, l, NA_WIDTH), pc.dtype),
        compiler_params=_params("parallel", "parallel"),
        name="ctx_attn",
    )(pc, pc, pc)


def _merge_kernel(of_ref, ob_ref, z_ref, bg_ref, cg_ref, hh_ref, cgp_ref, cgn_ref, hhp_ref, hhn_ref, yc_ref,
                  ga_ref, gb_ref, gcg_ref, x_ref, gt_ref, wpa_ref, wpb_ref, wpc_ref, wo_ref, ng_ref, scw_ref,
                  o_ref, *, nt):
    i = pl.program_id(1)
    tm = x_ref.shape[0]
    o = of_ref[...].astype(F32) + ob_ref[...].astype(F32)
    sz = z_ref[...].astype(F32)
    ya = []
    for h in range(DN_HEADS):
        sl = slice(h * DN_HEAD_DIM, (h + 1) * DN_HEAD_DIM)
        ya.append(_rms(o[:, sl]) * ng_ref[...] * sz[:, sl])
    y_a = jnp.concatenate(ya, axis=1)
    has_prev = (i > 0).astype(F32)
    has_next = (i < nt - 1).astype(F32)
    f32 = lambda r: r[...].astype(F32)
    full = jnp.concatenate([f32(cgp_ref) * f32(hhp_ref) * has_prev, f32(cg_ref) * f32(hh_ref),
                            f32(cgn_ref) * f32(hhn_ref) * has_next], axis=0)
    y_b = f32(bg_ref) * _dwconv3_rows(full, scw_ref, tm, HALO_P)
    y = (f32(ga_ref) * _dot(y_a, wpa_ref[...]) + f32(gb_ref) * _dot(y_b, wpb_ref[...])
         + f32(gcg_ref) * _dot(yc_ref[...], wpc_ref[...]))
    o_ref[...] = x_ref[...] + gt_ref[...] * _dot(y, wo_ref[...])


def _merge_call(of, ob, p, yc, x, mod, mrow, wpa, wpb, wpc, wo, ng, scw, tm):
    b, t, d = x.shape
    nt = t // tm
    hb = tm // HALO_P
    nhb = t // HALO_P
    w5 = SC_WIDTH

    def tile(width, col):
        return pl.BlockSpec((None, tm, width), lambda bi, i: (bi, i, col))

    def prev(col):
        return pl.BlockSpec((None, HALO_P, w5), lambda bi, i: (bi, jnp.maximum(i * hb - 1, 0), col))

    def nxt(col):
        return pl.BlockSpec((None, HALO_P, w5), lambda bi, i: (bi, jnp.minimum(i * hb + hb, nhb - 1), col))

    def full(a):
        return pl.BlockSpec(a.shape, lambda bi, i: (0,) * a.ndim)

    in_specs = [
        tile(DN_WIDTH, 0), tile(DN_WIDTH, 0),
        tile(w5, 3), tile(w5, 4), tile(w5, 5), tile(w5, 6),
        prev(5), nxt(5), prev(6), nxt(6),
        tile(NA_WIDTH, 0),
        tile(d, 5), tile(d, 6), tile(d, 7),
        tile(d, 0),
        pl.BlockSpec((None, 1, d), lambda bi, i: (mrow(bi), 0, 2)),
        full(wpa), full(wpb), full(wpc), full(wo), full(ng), full(scw),
    ]
    return pl.pallas_call(
        functools.partial(_merge_kernel, nt=nt),
        grid=(b, nt),
        in_specs=in_specs,
        out_specs=pl.BlockSpec((None, tm, d), lambda bi, i: (bi, i, 0)),
        out_shape=jax.ShapeDtypeStruct((b, t, d), F32),
        compiler_params=_params("parallel", "parallel"),
        name="merge",
    )(of, ob, p, p, p, p, p, p, p, p, yc, p, p, p, x, mod, wpa, wpb, wpc, wo, ng, scw)


def _ffn_kernel(*refs, nt, nk, fc, final):
    it = iter(refs)
    x_ref, xp_ref, xn_ref, g_ref, sh_ref, sc_ref, gt_ref = (next(it) for _ in range(7))
    wup_ref, cw_ref, cb_ref, wd_ref = (next(it) for _ in range(4))
    fg_ref = next(it) if final else None
    o_ref, h_scr, act_scr = next(it), next(it), next(it)
    i = pl.program_id(1)
    tm = x_ref.shape[0]
    dff = nk * fc

    xfull = jnp.concatenate([xp_ref[...], x_ref[...], xn_ref[...]], axis=0)
    h = (_rms(xfull) * g_ref[...]) * (1.0 + sc_ref[...]) + sh_ref[...]
    h_scr[...] = h.astype(BF16)

    row = lax.broadcasted_iota(jnp.int32, (tm + 2 * HALO, 1), 0)
    keep = ((row >= HALO) | (i > 0)) & ((row < tm + HALO) | (i < nt - 1))

    def up_proj(k):
        return [jnp.where(keep, jnp.dot(h_scr[...], wup_ref[:, c0:c0 + fc], preferred_element_type=F32), 0.0)
                for c0 in (k * fc, dff + k * fc)]

    ups = up_proj(0)
    g0 = 0
    for k in range(nk):
        nxt = up_proj(k + 1) if k + 1 < nk else None
        a = _dwconv3_rows(ups[0], cw_ref[:, k * fc:(k + 1) * fc], tm) + cb_ref[:, k * fc:(k + 1) * fc]
        bb = (_dwconv3_rows(ups[1], cw_ref[:, dff + k * fc:dff + (k + 1) * fc], tm)
              + cb_ref[:, dff + k * fc:dff + (k + 1) * fc])
        act_scr[:, k * fc:(k + 1) * fc] = (_silu(a) * bb).astype(BF16)
        if (k + 1) % FFN_DOWN_GROUP == 0 or k == nk - 1:
            g1 = (k + 1) * fc
            contrib = jnp.dot(act_scr[:, g0:g1], wd_ref[g0:g1, :], preferred_element_type=F32)
            if g0 == 0:
                o_ref[...] = contrib
            else:
                o_ref[...] += contrib
            g0 = g1
        ups = nxt

    out = x_ref[...] + gt_ref[...] * o_ref[...]
    if final:
        out = _rms(out) * fg_ref[...]
    o_ref[...] = out


def _ffn_call(x, g, mod, mrow, w_up, cw, cb, w_down, tm, final_g=None):
    b, t, d = x.shape
    dff = w_down.shape[0]
    fc = 256
    nk = dff // fc
    nt = t // tm
    hb = tm // HALO
    nhb = t // HALO
    final = final_g is not None

    def resident(a):
        return pl.BlockSpec(a.shape, lambda bi, i: (0,) * a.ndim, pipeline_mode=pl.Buffered(1))

    in_specs = [
        pl.BlockSpec((None, tm, d), lambda bi, i: (bi, i, 0)),
        pl.BlockSpec((None, HALO, d), lambda bi, i: (bi, jnp.maximum(i * hb - 1, 0), 0)),
        pl.BlockSpec((None, HALO, d), lambda bi, i: (bi, jnp.minimum(i * hb + hb, nhb - 1), 0)),
        pl.BlockSpec((1, d), lambda bi, i: (0, 0)),
        pl.BlockSpec((None, 1, d), lambda bi, i: (mrow(bi), 0, 3)),
        pl.BlockSpec((None, 1, d), lambda bi, i: (mrow(bi), 0, 4)),
        pl.BlockSpec((None, 1, d), lambda bi, i: (mrow(bi), 0, 5)),
        resident(w_up), resident(cw), resident(cb), resident(w_down),
    ]
    args = [x, x, x, g, mod, mod, mod, w_up, cw, cb, w_down]
    if final:
        in_specs.append(pl.BlockSpec((1, d), lambda bi, i: (0, 0)))
        args.append(final_g)
    return pl.pallas_call(
        functools.partial(_ffn_kernel, nt=nt, nk=nk, fc=fc, final=final),
        grid=(b, nt),
        in_specs=in_specs,
        out_specs=pl.BlockSpec((None, tm, d), lambda bi, i: (bi, i, 0)),
        out_shape=jax.ShapeDtypeStruct((b, t, d), F32),
        scratch_shapes=[pltpu.VMEM((tm + 2 * HALO, d), BF16), pltpu.VMEM((tm, dff), BF16)],
        compiler_params=_params("parallel", "parallel"),
        name="ffn_final" if final else "ffn",
    )(*args)


def _rope_tables(t):
    tok = jnp.arange(t, dtype=jnp.int32)
    rows = (tok // GRID_W).astype(F32)
    cols = (tok % GRID_W).astype(F32)
    nf = DN_HEAD_DIM // 4
    inv = ROPE_BASE ** (-jnp.arange(nf, dtype=F32) / nf)
    ang = jnp.concatenate([rows[:, None] * inv, cols[:, None] * inv], axis=-1)
    cos = jnp.repeat(jnp.cos(ang), 2, axis=-1)
    sin = jnp.repeat(jnp.sin(ang), 2, axis=-1)
    sign = jnp.where(jnp.arange(DN_HEAD_DIM) % 2 == 0, -1.0, 1.0).astype(F32)
    return cos, sin * sign


def _na_bias_table(rpb):
    w = GRID_W
    depth = rpb.shape[0]
    qc = np.arange(w)[:, None]
    kc = np.arange(w)[None, :]
    cstart = np.clip(qc - NA_WIN_C // 2, 0, w - NA_WIN_C)
    valid = (kc >= cstart) & (kc < cstart + NA_WIN_C)
    onehot = ((kc - qc + NA_WIN_C - 1)[None] == np.arange(2 * NA_WIN_C - 1)[:, None, None]) & valid[None]
    tbl = jnp.einsum("lhdj,jck->lhdck", rpb, jnp.asarray(onehot, F32), precision=lax.Precision.HIGHEST)
    tbl = tbl + jnp.asarray(np.where(valid, 0.0, NEG), F32)
    tbl = jnp.stack([tbl[:, :, d0:d0 + NA_WIN_R] for d0 in range(NA_WIN_R)], axis=2)
    tbl = tbl.transpose(0, 1, 2, 4, 3, 5).reshape(depth, NA_HEADS // 2, 2, NA_WIN_R, w, NA_WIN_R * w)
    return tbl.transpose(0, 1, 3, 2, 4, 5).reshape(depth, NA_HEADS // 2, NA_WIN_R, 2 * w, NA_WIN_R * w)


def _permute_w_in(w_in):
    s = np.cumsum([0, 3 * DN_WIDTH, DN_WIDTH, 2 * DN_HEADS, 2 * DN_HEADS, 3 * SC_WIDTH, 3 * NA_WIDTH, 3 * 1024])
    qkv, z, a, b, sc, na, gates = (w_in[..., s[j]:s[j + 1]] for j in range(7))
    pad = jnp.zeros(w_in.shape[:-1] + (LANE - 4 * DN_HEADS,), w_in.dtype)
    return jnp.concatenate([qkv, z, sc, na, gates], axis=-1), jnp.concatenate([a, b, pad], axis=-1)


def kernel(x, c, ctx, c_ctx, norm1_g, norm2_g, w_ada, b_ada, w_in, dn_conv_w, dn_a_log, dn_dt_bias, dn_norm_g, sc_conv_w, na_rpb, w_pa, w_pb, w_pc, w_o, w_up, ffn_conv_w, ffn_conv_b, w_down, final_norm_g):
    bsz, t, d = x.shape
    depth = w_in.shape[0]
    assert d == 1024 and t % (NA_WIN_R * GRID_W) == 0 and ctx.shape[1] % (GDN_TILE_CHUNKS * DN_CHUNK) == 0

    mod_rows = -(-(bsz + 1) // SUBLANE) * SUBLANE
    cc = jnp.zeros((mod_rows, d), F32).at[:bsz].set(c).at[bsz].set(c_ctx)
    mod = _ada_call(cc, w_ada, b_ada).reshape(depth, mod_rows, 1, 6 * d)
    lat_row = lambda bi: bi
    ctx_row = lambda bi: bsz

    w_in_p, w_ab = _permute_w_in(w_in.astype(BF16))
    rope = _rope_tables(t)
    na_bias = _na_bias_table(na_rpb)
    par = jnp.zeros((depth, SUBLANE, LANE), F32)
    par = par.at[:, 0, :2 * DN_HEADS].set(dn_a_log.reshape(depth, -1)).at[:, 1, :2 * DN_HEADS].set(dn_dt_bias.reshape(depth, -1))
    s_zero = jnp.zeros((bsz, 2, DN_HEADS, DN_HEAD_DIM, DN_HEAD_DIM), F32)
    tm_lat = 1024 if t % 1024 == 0 else 512
    tm_ctx = ctx.shape[1]

    xc = ctx
    for l in range(depth):
        need_ctx = l < depth - 1
        g1 = norm1_g[l][None]
        g2 = norm2_g[l][None]
        wpa, wpb, wpc, wo = (a[l].astype(BF16) for a in (w_pa, w_pb, w_pc, w_o))
        wup, wdn = w_up[l].astype(BF16), w_down[l].astype(BF16)
        ng = dn_norm_g[l][None]
        p, ab = _inproj_call(x, g1, mod[l], lat_row, w_in_p[l], w_ab[l], tm_lat)
        lc_all = bsz * tm_ctx
        pc, abc = _inproj_call(xc.reshape(1, lc_all, d), g1, mod[l], ctx_row, w_in_p[l], w_ab[l],
                               tm_lat if lc_all % tm_lat == 0 else tm_ctx)
        pc, abc = pc.reshape(bsz, tm_ctx, -1), abc.reshape(bsz, tm_ctx, -1)
        ocf, ocb, s_ctx = _gdn_call(_gdn_prep_call(pc, dn_conv_w[l], None), abc, par[l], s_zero, True)
        olf, olb = _gdn_call(_gdn_prep_call(p, dn_conv_w[l], rope), ab, par[l], s_ctx, False)
        y_c = _na_call(p, pc, na_bias[l])
        x = _merge_call(olf, olb, p, y_c, x, mod[l], lat_row, wpa, wpb, wpc, wo, ng, sc_conv_w[l], 512)
        fg = final_norm_g[None] if l == depth - 1 else None
        x = _ffn_call(x, g2, mod[l], lat_row, wup, ffn_conv_w[l], ffn_conv_b[l][None], wdn, tm_lat, fg)
        if need_ctx:
            yc_c = _ctx_attn_call(pc)
            xc = _merge_call(ocf, ocb, pc, yc_c, xc, mod[l], ctx_row, wpa, wpb, wpc, wo, ng, sc_conv_w[l], tm_ctx)
            xc = _ffn_call(xc, g2, mod[l], ctx_row, wup, ffn_conv_w[l], ffn_conv_b[l][None], wdn, tm_ctx)
    return x
```

```python
import functools

import numpy as np
import jax
import jax.numpy as jnp
from jax import lax
from jax.experimental import pallas as pl
from jax.experimental.pallas import tpu as pltpu

F32 = jnp.float32
BF16 = jnp.bfloat16

GRID_W = 64
DN_HEADS = 4
DN_HEAD_DIM = 128
DN_WIDTH = DN_HEADS * DN_HEAD_DIM
DN_CHUNK = 64
GDN_TILE_CHUNKS = 4
FFN_DOWN_GROUP = 4
SC_WIDTH = 512
NA_HEADS = 8
NA_HEAD_DIM = 64
NA_WIDTH = NA_HEADS * NA_HEAD_DIM
NA_WIN_R = 8
NA_WIN_C = 16
NA_ROWS_PER_ITER = 8
ROPE_BASE = 10000.0
EPS = 1e-6
NEG = -1e30

LANE = 128
SUBLANE = 8
HALO = SUBLANE
HALO_P = 2 * SUBLANE
VMEM_LIMIT = 48 * 1024 * 1024

P_COLS = 3 * DN_WIDTH + DN_WIDTH + 3 * SC_WIDTH + 3 * NA_WIDTH + 3 * 1024
P_TILE = 2048
P_SLOT = 512
Z_SLOT, BG_SLOT, CG_SLOT, HH_SLOT = 3, 4, 5, 8
NAQ_SLOT, NAK_SLOT, NAV_SLOT = 9, 12, 13
GATE_SLOTS = (6, 10, 14)


def _dot(a, b):
    return jnp.dot(a.astype(BF16), b.astype(BF16), preferred_element_type=F32)


def _dot_nt(a, b):
    return lax.dot_general(a.astype(BF16), b.astype(BF16), (((1,), (1,)), ((), ())), preferred_element_type=F32)


def _dot_tn(a, b):
    return lax.dot_general(a.astype(BF16), b.astype(BF16), (((0,), (0,)), ((), ())), preferred_element_type=F32)


def _sigmoid(x):
    return 0.5 * jnp.tanh(0.5 * x) + 0.5


def _silu(x):
    return x * _sigmoid(x)


def _softplus(x):
    return jnp.maximum(x, 0.0) + jnp.log1p(jnp.exp(-jnp.abs(x)))


def _rms(x, eps=EPS):
    return x * lax.rsqrt(jnp.mean(x * x, axis=-1, keepdims=True) + eps)


def _params(*sem):
    return pltpu.CompilerParams(dimension_semantics=sem, vmem_limit_bytes=VMEM_LIMIT)


def _dwconv3_rows(full, w_ref, n, halo=HALO):
    rows = full.shape[0]
    dn = pltpu.roll(full, 1, axis=0)
    up = pltpu.roll(full, rows - 1, axis=0)
    out = dn * w_ref[0:1, :] + full * w_ref[1:2, :] + up * w_ref[2:3, :]
    return out[halo:halo + n]


def _ada_kernel(c_ref, w_ref, b_ref, o_ref):
    o_ref[...] = _dot(_silu(c_ref[...]), w_ref[...]) + b_ref[...]


def _ada_call(cc, w_ada, b_ada):
    depth, d, n6 = w_ada.shape
    r = cc.shape[0]
    tn = 1536
    return pl.pallas_call(
        _ada_kernel,
        grid=(depth, n6 // tn),
        in_specs=[
            pl.BlockSpec((r, d), lambda l, j: (0, 0)),
            pl.BlockSpec((None, d, tn), lambda l, j: (l, 0, j)),
            pl.BlockSpec((None, 1, tn), lambda l, j: (l, 0, j)),
        ],
        out_specs=pl.BlockSpec((None, r, tn), lambda l, j: (l, 0, j)),
        out_shape=jax.ShapeDtypeStruct((depth, r, n6), F32),
        compiler_params=_params("parallel", "parallel"),
        name="ada_mod",
    )(cc, w_ada, b_ada.reshape(depth, 1, n6))


def _inproj_kernel(x_ref, g_ref, sh_ref, sc_ref, w_ref, wab_ref, o_ref, ab_ref, h_ref):
    j = pl.program_id(2)

    @pl.when(j == 0)
    def _():
        h = (_rms(x_ref[...]) * g_ref[...]) * (1.0 + sc_ref[...]) + sh_ref[...]
        h_ref[...] = h.astype(BF16)
        ab_ref[...] = jnp.dot(h_ref[...], wab_ref[...], preferred_element_type=F32)

    res = jnp.dot(h_ref[...], w_ref[...], preferred_element_type=F32)
    half = res.shape[1] // 2
    o_ref[:, :half] = res[:, :half].astype(o_ref.dtype)
    act = res[:, half:]
    sg = _sigmoid(act)
    is_z = lax.broadcasted_iota(jnp.int32, (1, half), 1) >= half // 2
    o_ref[:, half:] = jnp.where(j > 0, sg, jnp.where(is_z, act * sg, act)).astype(o_ref.dtype)


def _inproj_call(x, g, mod, mrow, w, wab, tm):
    b, t, d = x.shape
    n = w.shape[1]
    tn = P_TILE
    return pl.pallas_call(
        _inproj_kernel,
        grid=(b, t // tm, n // tn),
        in_specs=[
            pl.BlockSpec((None, tm, d), lambda bi, i, j: (bi, i, 0)),
            pl.BlockSpec((1, d), lambda bi, i, j: (0, 0)),
            pl.BlockSpec((None, 1, d), lambda bi, i, j: (mrow(bi), 0, 0)),
            pl.BlockSpec((None, 1, d), lambda bi, i, j: (mrow(bi), 0, 1)),
            pl.BlockSpec((d, tn), lambda bi, i, j: (0, j)),
            pl.BlockSpec((d, LANE), lambda bi, i, j: (0, 0)),
        ],
        out_specs=[pl.BlockSpec((None, tm, tn), lambda bi, i, j: (bi, i, j)),
                   pl.BlockSpec((None, tm, LANE), lambda bi, i, j: (bi, i, 0))],
        out_shape=[jax.ShapeDtypeStruct((b, t, n), BF16), jax.ShapeDtypeStruct((b, t, LANE), F32)],
        scratch_shapes=[pltpu.VMEM((tm, d), BF16)],
        compiler_params=_params("parallel", "parallel", "arbitrary"),
        name="inproj",
    )(x, g, mod, mod, w, wab)


def _swap_pairs(x, even):
    n = x.shape[-1]
    return jnp.where(even, pltpu.roll(x, n - 1, axis=1), pltpu.roll(x, 1, axis=1))


def _gdn_prep_kernel(*refs, nt, use_rope):
    x_ref, xp_ref, xn_ref, cw_ref = refs[:4]
    cos_ref, sin_ref = refs[4:6] if use_rope else (None, None)
    o_ref = refs[-1]
    i = pl.program_id(1)
    tr = x_ref.shape[0]
    hd = DN_HEAD_DIM
    has_prev = (i > 0).astype(F32)
    has_next = (i < nt - 1).astype(F32)
    full = jnp.concatenate([xp_ref[...].astype(F32) * has_prev, x_ref[...].astype(F32),
                            xn_ref[...].astype(F32) * has_next], axis=0)
    act = _silu(_dwconv3_rows(full, cw_ref, tr, HALO_P))
    even = (lax.broadcasted_iota(jnp.int32, (tr, hd), 1) % 2) == 0
    for off, scale in ((0, hd ** -0.5), (DN_WIDTH, 1.0)):
        for h in range(DN_HEADS):
            sl = slice(off + h * hd, off + (h + 1) * hd)
            xh = act[:, sl]
            xh = xh * lax.rsqrt(jnp.sum(xh * xh, axis=-1, keepdims=True) + EPS)
            if use_rope:
                xh = xh * cos_ref[...] + _swap_pairs(xh, even) * sin_ref[...]
            o_ref[:, sl] = (xh * scale).astype(o_ref.dtype)
    o_ref[:, 2 * DN_WIDTH:] = act[:, 2 * DN_WIDTH:].astype(o_ref.dtype)


def _gdn_prep_call(p, conv_w, rope):
    b, t, _ = p.shape
    tr = GDN_TILE_CHUNKS * DN_CHUNK
    nt = t // tr
    hb = tr // HALO_P
    nhb = t // HALO_P
    qkv_w = 3 * DN_WIDTH
    in_specs = [
        pl.BlockSpec((None, tr, qkv_w), lambda bi, i: (bi, i, 0)),
        pl.BlockSpec((None, HALO_P, qkv_w), lambda bi, i: (bi, jnp.maximum(i * hb - 1, 0), 0)),
        pl.BlockSpec((None, HALO_P, qkv_w), lambda bi, i: (bi, jnp.minimum(i * hb + hb, nhb - 1), 0)),
        pl.BlockSpec(conv_w.shape, lambda bi, i: (0, 0)),
    ]
    args = [p, p, p, conv_w]
    if rope is not None:
        in_specs += [pl.BlockSpec((tr, DN_HEAD_DIM), lambda bi, i: (i, 0))] * 2
        args += [rope[0], rope[1]]
    return pl.pallas_call(
        functools.partial(_gdn_prep_kernel, nt=nt, use_rope=rope is not None),
        grid=(b, nt),
        in_specs=in_specs,
        out_specs=pl.BlockSpec((None, tr, qkv_w), lambda bi, i: (bi, i, 0)),
        out_shape=jax.ShapeDtypeStruct((b, t, qkv_w), p.dtype),
        compiler_params=_params("parallel", "parallel"),
        name="gdn_prep",
    )(*args)


def _gdn_kernel(*refs, nt, emit_state):
    it = iter(refs)
    xs = [[next(it) for _ in range(2)] for _ in range(2)]
    par_ref = next(it)
    s0_ref = next(it)
    o_refs = [next(it), next(it)]
    sout_ref = next(it) if emit_state else None
    s_scr, uw_scr, qkd_scr = (next(it) for _ in range(3))

    c = DN_CHUNK
    nh = DN_HEADS
    hd = DN_HEAD_DIM
    nch = GDN_TILE_CHUNKS
    tr = nch * c
    cw = nh * c
    lc = c.bit_length() - 1
    i = pl.program_id(1)

    @pl.when(i == 0)
    def _():
        s_scr[...] = s0_ref[...]

    ri = lax.broadcasted_iota(jnp.int32, (c, cw), 0)
    li = lax.broadcasted_iota(jnp.int32, (c, cw), 1)
    jj = li & (c - 1)
    seg = li >> lc
    eye_cat = ri == jj
    bd_sq = jnp.where((lax.broadcasted_iota(jnp.int32, (cw, cw), 0) >> lc)
                      == (lax.broadcasted_iota(jnp.int32, (cw, cw), 1) >> lc), 1.0, 0.0).astype(BF16)
    bd_k = jnp.where((lax.broadcasted_iota(jnp.int32, (cw, nh * hd), 0) >> lc)
                     == (lax.broadcasted_iota(jnp.int32, (cw, nh * hd), 1) // hd), 1.0, 0.0).astype(BF16)
    rowi = lax.broadcasted_iota(jnp.int32, (tr, LANE), 0) & (c - 1)
    neg_a = -jnp.exp(par_ref[0:1, :])
    dt_b = par_ref[1:2, :]

    def col_bcast(x, j, width):
        return jnp.broadcast_to(x[:, j:j + 1], (x.shape[0], width))

    def colcat(x, base):
        out = col_bcast(x, base + nh - 1, cw)
        for h in range(nh - 2, -1, -1):
            out = jnp.where(seg == h, col_bcast(x, base + h, cw), out)
        return out

    def block_diag(m, mask):
        return jnp.concatenate([m.astype(BF16)] * nh, axis=0) * mask

    nar = []
    for d in range(2):
        ab = xs[d][1][...]
        gc = neg_a * _softplus(ab + dt_b)
        beta = pltpu.roll(_sigmoid(ab), LANE - 2 * nh, axis=1)
        sft = 1
        while sft < c:
            if d == 0:
                gc = gc + jnp.where(rowi >= sft, pltpu.roll(gc, sft, axis=0), 0.0)
            else:
                gc = gc + jnp.where(rowi < c - sft, pltpu.roll(gc, tr - sft, axis=0), 0.0)
            sft *= 2
        e1 = jnp.exp(gc)
        nar.append(dict(gc=gc, e1=e1, beta=beta, be=beta * e1))

    st = {}

    def setup(key):
        d, ch = key
        rs = slice(ch * c, (ch + 1) * c)
        base = d * nh
        incl = (ri >= jj) if d == 0 else (ri <= jj)
        gcc = colcat(nar[d]["gc"][rs], base)
        gcr = jnp.broadcast_to(jnp.sum(jnp.where(eye_cat, gcc, 0.0), axis=0, keepdims=True), (c, cw))
        decay = jnp.where(incl, jnp.exp(jnp.where(incl, gcc - gcr, 0.0)), 0.0)
        kf = xs[d][0][rs, DN_WIDTH:2 * DN_WIDTH]
        prod = _dot_nt(jnp.concatenate([xs[d][0][rs, :DN_WIDTH], kf], axis=0), block_diag(kf, bd_k))
        qkd_scr[d, ch] = prod[:c] * decay
        amat = jnp.where(incl & (~eye_cat), colcat(nar[d]["beta"][rs], base) * prod[c:] * decay, 0.0)
        lvl1 = ((ri >> 1) == (jj >> 1))
        st[key] = dict(amat=amat.astype(BF16), tinv=jnp.where(eye_cat, 1.0, 0.0) - jnp.where(lvl1, amat, 0.0))

    def invert_level(keys, s):
        ls = s.bit_length() - 1
        lvl = jnp.where(((ri >> (ls + 1)) == (jj >> (ls + 1))) & ((ri >> ls) != (jj >> ls)), 1.0, 0.0).astype(BF16)
        for key in keys:
            p = st[key]
            p["t_bf"] = p["tinv"].astype(BF16)
            p["x"] = _dot(p["t_bf"], block_diag(p["amat"] * lvl, bd_sq))
        for key in keys:
            p = st[key]
            p["tinv"] = p["tinv"] - _dot(p["x"], block_diag(p["t_bf"], bd_sq))

    def solve(key):
        d, ch = key
        rs = slice(ch * c, (ch + 1) * c)
        base = d * nh
        x_ref = xs[d][0]
        rhs = jnp.concatenate([
            jnp.concatenate([x_ref[rs, 2 * DN_WIDTH + h * hd:2 * DN_WIDTH + (h + 1) * hd].astype(F32)
                             * col_bcast(nar[d]["beta"][rs], base + h, hd),
                             x_ref[rs, DN_WIDTH + h * hd:DN_WIDTH + (h + 1) * hd].astype(F32)
                             * col_bcast(nar[d]["be"][rs], base + h, hd)], axis=1)
            for h in range(nh)], axis=0)
        uw_scr[d, ch] = _dot(block_diag(st.pop(key)["tinv"], bd_sq), rhs)

    states = [[s_scr[d, h] for h in range(nh)] for d in range(2)]

    def scan_step(step):
        cur = [(0, step), (1, nch - 1 - step)]
        loc = {}
        for d, ch in cur:
            rs = slice(ch * c, (ch + 1) * c)
            base = d * nh
            gc_c = nar[d]["gc"][rs]
            gtot = gc_c[c - 1:c] if d == 0 else gc_c[0:1]
            e2 = jnp.exp(gtot - gc_c)
            r_ = []
            for h in range(nh):
                hs = slice(h * hd, (h + 1) * hd)
                w_h = uw_scr[d, ch, h * c:(h + 1) * c, hd:]
                qd_h = xs[d][0][rs, hs].astype(F32) * col_bcast(nar[d]["e1"][rs], base + h, hd)
                r_.append(_dot(jnp.concatenate([w_h, qd_h], axis=0), states[d][h]))
            loc[d] = dict(r=r_, e2=e2, gtot=gtot)
        for d, ch in cur:
            loc[d]["vn"] = [uw_scr[d, ch, h * c:(h + 1) * c, :hd] - loc[d]["r"][h][:c] for h in range(nh)]
        for d, ch in cur:
            rs = slice(ch * c, (ch + 1) * c)
            base = d * nh
            vn = loc[d]["vn"]
            o2 = _dot(block_diag(qkd_scr[d, ch], bd_sq), jnp.concatenate(vn, axis=0))
            for h in range(nh):
                hs = slice(h * hd, (h + 1) * hd)
                kd_h = (xs[d][0][rs, DN_WIDTH + h * hd:DN_WIDTH + (h + 1) * hd].astype(F32)
                        * col_bcast(loc[d]["e2"], base + h, hd))
                o_refs[d][rs, hs] = (loc[d]["r"][h][c:] + o2[h * c:(h + 1) * c]).astype(o_refs[d].dtype)
                eg = jnp.exp(jnp.broadcast_to(loc[d]["gtot"][0:1, base + h:base + h + 1], (hd, hd)))
                states[d][h] = states[d][h] * eg + _dot_tn(kd_h, vn[h])

    probs = [(d, ch) for ch in range(nch) for d in range(2)]
    for key in probs:
        setup(key)
    for e in range(1, lc):
        invert_level(probs, 1 << e)
    for key in probs:
        solve(key)
    for step in range(nch):
        scan_step(step)
    for d in range(2):
        for h in range(nh):
            s_scr[d, h] = states[d][h]

    if emit_state:
        @pl.when(i == nt - 1)
        def _():
            sout_ref[...] = s_scr[...]


def _gdn_call(qkv, ab, par, s0, emit_state):
    b, t, qkv_w = qkv.shape
    c = DN_CHUNK
    tr = GDN_TILE_CHUNKS * c
    nt = t // tr

    def tile_of(d, i):
        return i if d == 0 else nt - 1 - i

    in_specs = []
    args = []
    for d in range(2):
        in_specs += [
            pl.BlockSpec((None, tr, qkv_w), lambda bi, i, d=d: (bi, tile_of(d, i), 0)),
            pl.BlockSpec((None, tr, LANE), lambda bi, i, d=d: (bi, tile_of(d, i), 0)),
        ]
        args += [qkv, ab]
    in_specs += [pl.BlockSpec(par.shape, lambda bi, i: (0, 0))]
    args += [par]
    sshape = (2, DN_HEADS, DN_HEAD_DIM, DN_HEAD_DIM)
    in_specs += [pl.BlockSpec((None,) + sshape, lambda bi, i: (bi, 0, 0, 0, 0))]
    args += [s0]
    out_specs = [
        pl.BlockSpec((None, tr, DN_WIDTH), lambda bi, i: (bi, i, 0)),
        pl.BlockSpec((None, tr, DN_WIDTH), lambda bi, i: (bi, nt - 1 - i, 0)),
    ]
    out_shape = [jax.ShapeDtypeStruct((b, t, DN_WIDTH), qkv.dtype)] * 2
    if emit_state:
        out_specs += [pl.BlockSpec((None,) + sshape, lambda bi, i: (bi, 0, 0, 0, 0))]
        out_shape += [jax.ShapeDtypeStruct((b,) + sshape, F32)]
    cw = DN_HEADS * c
    scratch = [
        pltpu.VMEM(sshape, F32),
        pltpu.VMEM((2, GDN_TILE_CHUNKS, cw, 2 * DN_HEAD_DIM), F32),
        pltpu.VMEM((2, GDN_TILE_CHUNKS, c, cw), F32),
    ]
    return pl.pallas_call(
        functools.partial(_gdn_kernel, nt=nt, emit_state=emit_state),
        grid=(b, nt),
        in_specs=in_specs,
        out_specs=out_specs,
        out_shape=out_shape,
        scratch_shapes=scratch,
        compiler_params=_params("parallel", "arbitrary"),
        name="gdn_ctx" if emit_state else "gdn_lat",
    )(*args)


def _na_kernel(q_ref, k_ref, v_ref, kc_ref, vc_ref, bias_ref, o_ref, *, rows):
    w = GRID_W
    nl = NA_WIN_R * w
    nr = NA_ROWS_PER_ITER
    scale = NA_HEAD_DIM ** -0.5
    kc = kc_ref[...].astype(BF16)
    vc = vc_ref[...].astype(BF16)

    def stack_heads(q2):
        sel = lax.broadcasted_iota(jnp.int32, q2.shape, 1) < NA_HEAD_DIM
        zero = jnp.zeros_like(q2)
        return jnp.concatenate([jnp.where(sel, q2, zero), jnp.where(sel, zero, q2)], axis=0)

    def unstack_heads(x):
        n = x.shape[0] // 2
        sel = lax.broadcasted_iota(jnp.int32, (n, LANE), 1) < NA_HEAD_DIM
        return jnp.where(sel, jnp.broadcast_to(x[:n], (n, LANE)), jnp.broadcast_to(x[n:], (n, LANE)))

    def row_block(it, carry):
        rr = [it * nr + u for u in range(nr)]
        rss = [jnp.clip(r - NA_WIN_R // 2, 0, rows - NA_WIN_R) for r in rr]
        rsl = [pl.ds(pl.multiple_of(r * w, w), w) for r in rr]
        wsl = [pl.ds(pl.multiple_of(rs * w, w), nl) for rs in rss]
        q_all = q_ref[pl.ds(pl.multiple_of(it * (nr * w), nr * w), nr * w), :] * scale
        s_c = _dot_nt(stack_heads(q_all), kc)
        s = [_dot_nt(stack_heads(q_all[u * w:(u + 1) * w]), k_ref[wsl[u], :]) + bias_ref[rss[u] - rr[u] + NA_WIN_R - 1]
             for u in range(nr)]
        m_c = jnp.max(s_c, axis=-1, keepdims=True)
        m_l = [jnp.max(x, axis=-1, keepdims=True) for x in s]
        p_c = jnp.exp(s_c - m_c)
        p = [jnp.exp(x - m) for x, m in zip(s, m_l)]
        o_c = unstack_heads(_dot(p_c, vc))
        o_l = [_dot(p[u], v_ref[wsl[u], :]) for u in range(nr)]
        l_c = unstack_heads(jnp.sum(p_c, axis=-1, keepdims=True))
        m_c = unstack_heads(m_c)
        for u in range(nr):
            usl = slice(u * w, (u + 1) * w)
            l_u = unstack_heads(jnp.sum(p[u], axis=-1, keepdims=True))
            m_u = unstack_heads(m_l[u])
            m = jnp.maximum(m_u, m_c[usl])
            a_l = jnp.exp(m_u - m)
            a_c = jnp.exp(m_c[usl] - m)
            out = (unstack_heads(o_l[u]) * a_l + o_c[usl] * a_c) / (l_u * a_l + l_c[usl] * a_c)
            o_ref[rsl[u], :] = out.astype(o_ref.dtype)
        return carry

    lax.fori_loop(0, rows // nr, row_block, 0)


def _na_call(p, pc, bias):
    b, t, _ = p.shape
    l = pc.shape[1]
    rows = t // GRID_W
    npair = NA_HEADS // 2
    return pl.pallas_call(
        functools.partial(_na_kernel, rows=rows),
        grid=(npair, b),
        in_specs=[
            pl.BlockSpec((None, t, LANE), lambda j, bi: (bi, 0, NAQ_SLOT * (P_SLOT // LANE) + j)),
            pl.BlockSpec((None, t, LANE), lambda j, bi: (bi, 0, NAK_SLOT * (P_SLOT // LANE) + j)),
            pl.BlockSpec((None, t, LANE), lambda j, bi: (bi, 0, NAV_SLOT * (P_SLOT // LANE) + j)),
            pl.BlockSpec((None, l, LANE), lambda j, bi: (bi, 0, NAK_SLOT * (P_SLOT // LANE) + j)),
            pl.BlockSpec((None, l, LANE), lambda j, bi: (bi, 0, NAV_SLOT * (P_SLOT // LANE) + j)),
            pl.BlockSpec((None,) + bias.shape[1:], lambda j, bi: (j, 0, 0, 0)),
        ],
        out_specs=pl.BlockSpec((None, t, LANE), lambda j, bi: (bi, 0, j)),
        out_shape=jax.ShapeDtypeStruct((b, t, NA_WIDTH), p.dtype),
        compiler_params=_params("parallel", "parallel"),
        name="na_lat",
    )(p, p, p, pc, pc, bias)


def _ctx_attn_kernel(q_ref, k_ref, v_ref, o_ref):
    scale = NA_HEAD_DIM ** -0.5
    q2 = q_ref[...] * scale
    k2 = k_ref[...]
    v2 = v_ref[...]
    head0 = lax.broadcasted_iota(jnp.int32, q2.shape, 1) < NA_HEAD_DIM
    out = None
    for hh in range(2):
        sel = head0 if hh == 0 else ~head0
        s = _dot_nt(jnp.where(sel, q2, jnp.zeros_like(q2)), k2)
        p = jnp.exp(s - jnp.max(s, axis=-1, keepdims=True))
        o = _dot(p, v2) / jnp.sum(p, axis=-1, keepdims=True)
        out = o if out is None else jnp.where(sel, o, out)
    o_ref[...] = out.astype(o_ref.dtype)


def _ctx_attn_call(pc):
    b, l, _ = pc.shape
    npair = NA_HEADS // 2
    return pl.pallas_call(
        _ctx_attn_kernel,
        grid=(b, npair),
        in_specs=[
            pl.BlockSpec((None, l, LANE), lambda bi, j: (bi, 0, NAQ_SLOT * (P_SLOT // LANE) + j)),
            pl.BlockSpec((None, l, LANE), lambda bi, j: (bi, 0, NAK_SLOT * (P_SLOT // LANE) + j)),
            pl.BlockSpec((None, l, LANE), lambda bi, j: (bi, 0, NAV_SLOT * (P_SLOT // LANE) + j)),
        ],
        out_specs=pl.BlockSpec((None, l, LANE), lambda bi, j: (bi, 0, j)),
        out_shape=jax.ShapeDtypeStruct((b, l, NA_WIDTH), pc.dtype),
        compiler_params=_params("parallel", "parallel"),
        name="ctx_attn",
    )(pc, pc, pc)


def _merge_kernel(of_ref, ob_ref, z_ref, bg_ref, cg_ref, hh_ref, cgp_ref, cgn_ref, hhp_ref, hhn_ref, yc_ref,
                  ga_ref, gb_ref, gcg_ref, x_ref, gt_ref, wpa_ref, wpb_ref, wpc_ref, wo_ref, ng_ref, scw_ref,
                  o_ref, *, nt):
    i = pl.program_id(1)
    tm = x_ref.shape[0]
    o = of_ref[...].astype(F32) + ob_ref[...].astype(F32)
    sz = z_ref[...].astype(F32)
    ya = []
    for h in range(DN_HEADS):
        sl = slice(h * DN_HEAD_DIM, (h + 1) * DN_HEAD_DIM)
        ya.append(_rms(o[:, sl]) * ng_ref[...] * sz[:, sl])
    y_a = jnp.concatenate(ya, axis=1)
    has_prev = (i > 0).astype(F32)
    has_next = (i < nt - 1).astype(F32)
    f32 = lambda r: r[...].astype(F32)
    full = jnp.concatenate([f32(cgp_ref) * f32(hhp_ref) * has_prev, f32(cg_ref) * f32(hh_ref),
                            f32(cgn_ref) * f32(hhn_ref) * has_next], axis=0)
    y_b = f32(bg_ref) * _dwconv3_rows(full, scw_ref, tm, HALO_P)
    y = (f32(ga_ref) * _dot(y_a, wpa_ref[...]) + f32(gb_ref) * _dot(y_b, wpb_ref[...])
         + f32(gcg_ref) * _dot(yc_ref[...], wpc_ref[...]))
    o_ref[...] = x_ref[...] + gt_ref[...] * _dot(y, wo_ref[...])


def _merge_call(of, ob, p, yc, x, mod, mrow, wpa, wpb, wpc, wo, ng, scw, tm):
    b, t, d = x.shape
    nt = t // tm
    hb = tm // HALO_P
    nhb = t // HALO_P
    w5 = P_SLOT

    def tile(width, col):
        return pl.BlockSpec((None, tm, width), lambda bi, i: (bi, i, col))

    def prev(col):
        return pl.BlockSpec((None, HALO_P, w5), lambda bi, i: (bi, jnp.maximum(i * hb - 1, 0), col))

    def nxt(col):
        return pl.BlockSpec((None, HALO_P, w5), lambda bi, i: (bi, jnp.minimum(i * hb + hb, nhb - 1), col))

    def full(a):
        return pl.BlockSpec(a.shape, lambda bi, i: (0,) * a.ndim)

    in_specs = [
        tile(DN_WIDTH, 0), tile(DN_WIDTH, 0),
        tile(w5, Z_SLOT), tile(w5, BG_SLOT), tile(w5, CG_SLOT), tile(w5, HH_SLOT),
        prev(CG_SLOT), nxt(CG_SLOT), prev(HH_SLOT), nxt(HH_SLOT),
        tile(NA_WIDTH, 0),
        tile(d, GATE_SLOTS[0] // 2), tile(d, GATE_SLOTS[1] // 2), tile(d, GATE_SLOTS[2] // 2),
        tile(d, 0),
        pl.BlockSpec((None, 1, d), lambda bi, i: (mrow(bi), 0, 2)),
        full(wpa), full(wpb), full(wpc), full(wo), full(ng), full(scw),
    ]
    return pl.pallas_call(
        functools.partial(_merge_kernel, nt=nt),
        grid=(b, nt),
        in_specs=in_specs,
        out_specs=pl.BlockSpec((None, tm, d), lambda bi, i: (bi, i, 0)),
        out_shape=jax.ShapeDtypeStruct((b, t, d), F32),
        compiler_params=_params("parallel", "parallel"),
        name="merge",
    )(of, ob, p, p, p, p, p, p, p, p, yc, p, p, p, x, mod, wpa, wpb, wpc, wo, ng, scw)


def _ffn_kernel(*refs, nt, nk, fc, final):
    it = iter(refs)
    x_ref, xp_ref, xn_ref, g_ref, sh_ref, sc_ref, gt_ref = (next(it) for _ in range(7))
    wup_ref, cw_ref, cb_ref, wd_ref = (next(it) for _ in range(4))
    fg_ref = next(it) if final else None
    o_ref, h_scr, act_scr = next(it), next(it), next(it)
    i = pl.program_id(1)
    tm = x_ref.shape[0]
    dff = nk * fc

    xfull = jnp.concatenate([xp_ref[...], x_ref[...], xn_ref[...]], axis=0)
    h = (_rms(xfull) * g_ref[...]) * (1.0 + sc_ref[...]) + sh_ref[...]
    h_scr[...] = h.astype(BF16)

    row = lax.broadcasted_iota(jnp.int32, (tm + 2 * HALO, 1), 0)
    keep = ((row >= HALO) | (i > 0)) & ((row < tm + HALO) | (i < nt - 1))

    def up_proj(k):
        return [jnp.where(keep, jnp.dot(h_scr[...], wup_ref[:, c0:c0 + fc], preferred_element_type=F32), 0.0)
                for c0 in (k * fc, dff + k * fc)]

    ups = up_proj(0)
    g0 = 0
    for k in range(nk):
        nxt = up_proj(k + 1) if k + 1 < nk else None
        a = _dwconv3_rows(ups[0], cw_ref[:, k * fc:(k + 1) * fc], tm) + cb_ref[:, k * fc:(k + 1) * fc]
        bb = (_dwconv3_rows(ups[1], cw_ref[:, dff + k * fc:dff + (k + 1) * fc], tm)
              + cb_ref[:, dff + k * fc:dff + (k + 1) * fc])
        act_scr[:, k * fc:(k + 1) * fc] = (_silu(a) * bb).astype(BF16)
        if (k + 1) % FFN_DOWN_GROUP == 0 or k == nk - 1:
            g1 = (k + 1) * fc
            contrib = jnp.dot(act_scr[:, g0:g1], wd_ref[g0:g1, :], preferred_element_type=F32)
            if g0 == 0:
                o_ref[...] = contrib
            else:
                o_ref[...] += contrib
            g0 = g1
        ups = nxt

    out = x_ref[...] + gt_ref[...] * o_ref[...]
    if final:
        out = _rms(out) * fg_ref[...]
    o_ref[...] = out


def _ffn_call(x, g, mod, mrow, w_up, cw, cb, w_down, tm, final_g=None):
    b, t, d = x.shape
    dff = w_down.shape[0]
    fc = 256
    nk = dff // fc
    nt = t // tm
    hb = tm // HALO
    nhb = t // HALO
    final = final_g is not None

    def resident(a):
        return pl.BlockSpec(a.shape, lambda bi, i: (0,) * a.ndim, pipeline_mode=pl.Buffered(1))

    in_specs = [
        pl.BlockSpec((None, tm, d), lambda bi, i: (bi, i, 0)),
        pl.BlockSpec((None, HALO, d), lambda bi, i: (bi, jnp.maximum(i * hb - 1, 0), 0)),
        pl.BlockSpec((None, HALO, d), lambda bi, i: (bi, jnp.minimum(i * hb + hb, nhb - 1), 0)),
        pl.BlockSpec((1, d), lambda bi, i: (0, 0)),
        pl.BlockSpec((None, 1, d), lambda bi, i: (mrow(bi), 0, 3)),
        pl.BlockSpec((None, 1, d), lambda bi, i: (mrow(bi), 0, 4)),
        pl.BlockSpec((None, 1, d), lambda bi, i: (mrow(bi), 0, 5)),
        resident(w_up), resident(cw), resident(cb), resident(w_down),
    ]
    args = [x, x, x, g, mod, mod, mod, w_up, cw, cb, w_down]
    if final:
        in_specs.append(pl.BlockSpec((1, d), lambda bi, i: (0, 0)))
        args.append(final_g)
    return pl.pallas_call(
        functools.partial(_ffn_kernel, nt=nt, nk=nk, fc=fc, final=final),
        grid=(b, nt),
        in_specs=in_specs,
        out_specs=pl.BlockSpec((None, tm, d), lambda bi, i: (bi, i, 0)),
        out_shape=jax.ShapeDtypeStruct((b, t, d), F32),
        scratch_shapes=[pltpu.VMEM((tm + 2 * HALO, d), BF16), pltpu.VMEM((tm, dff), BF16)],
        compiler_params=_params("parallel", "parallel"),
        name="ffn_final" if final else "ffn",
    )(*args)


def _rope_tables(t):
    tok = jnp.arange(t, dtype=jnp.int32)
    rows = (tok // GRID_W).astype(F32)
    cols = (tok % GRID_W).astype(F32)
    nf = DN_HEAD_DIM // 4
    inv = ROPE_BASE ** (-jnp.arange(nf, dtype=F32) / nf)
    ang = jnp.concatenate([rows[:, None] * inv, cols[:, None] * inv], axis=-1)
    cos = jnp.repeat(jnp.cos(ang), 2, axis=-1)
    sin = jnp.repeat(jnp.sin(ang), 2, axis=-1)
    sign = jnp.where(jnp.arange(DN_HEAD_DIM) % 2 == 0, -1.0, 1.0).astype(F32)
    return cos, sin * sign


def _na_bias_table(rpb):
    w = GRID_W
    depth = rpb.shape[0]
    qc = np.arange(w)[:, None]
    kc = np.arange(w)[None, :]
    cstart = np.clip(qc - NA_WIN_C // 2, 0, w - NA_WIN_C)
    valid = (kc >= cstart) & (kc < cstart + NA_WIN_C)
    onehot = ((kc - qc + NA_WIN_C - 1)[None] == np.arange(2 * NA_WIN_C - 1)[:, None, None]) & valid[None]
    tbl = jnp.einsum("lhdj,jck->lhdck", rpb, jnp.asarray(onehot, F32), precision=lax.Precision.HIGHEST)
    tbl = tbl + jnp.asarray(np.where(valid, 0.0, NEG), F32)
    tbl = jnp.stack([tbl[:, :, d0:d0 + NA_WIN_R] for d0 in range(NA_WIN_R)], axis=2)
    tbl = tbl.transpose(0, 1, 2, 4, 3, 5).reshape(depth, NA_HEADS // 2, 2, NA_WIN_R, w, NA_WIN_R * w)
    return tbl.transpose(0, 1, 3, 2, 4, 5).reshape(depth, NA_HEADS // 2, NA_WIN_R, 2 * w, NA_WIN_R * w)


def _permute_w_in(w_in):
    s = np.cumsum([0, 3 * DN_WIDTH, DN_WIDTH, 2 * DN_HEADS, 2 * DN_HEADS, 3 * SC_WIDTH, 3 * NA_WIDTH, 3 * 1024])
    qkv, z, a, b, sc, na, gates = (w_in[..., s[j]:s[j + 1]] for j in range(7))
    bg, cg, hh = (sc[..., j * P_SLOT:(j + 1) * P_SLOT] for j in range(3))
    naq, nak, nav = (na[..., j * P_SLOT:(j + 1) * P_SLOT] for j in range(3))
    ga, gb, gc = (gates[..., j * 2 * P_SLOT:(j + 1) * 2 * P_SLOT] for j in range(3))
    pad = jnp.zeros(w_in.shape[:-1] + (LANE - 4 * DN_HEADS,), w_in.dtype)
    return (jnp.concatenate([qkv, z, bg, cg, ga, hh, naq, gb, nak, nav, gc], axis=-1),
            jnp.concatenate([a, b, pad], axis=-1))


def kernel(x, c, ctx, c_ctx, norm1_g, norm2_g, w_ada, b_ada, w_in, dn_conv_w, dn_a_log, dn_dt_bias, dn_norm_g, sc_conv_w, na_rpb, w_pa, w_pb, w_pc, w_o, w_up, ffn_conv_w, ffn_conv_b, w_down, final_norm_g):
    bsz, t, d = x.shape
    depth = w_in.shape[0]
    assert d == 1024 and t % (NA_WIN_R * GRID_W) == 0 and ctx.shape[1] % (GDN_TILE_CHUNKS * DN_CHUNK) == 0

    mod_rows = -(-(bsz + 1) // SUBLANE) * SUBLANE
    cc = jnp.zeros((mod_rows, d), F32).at[:bsz].set(c).at[bsz].set(c_ctx)
    mod = _ada_call(cc, w_ada, b_ada).reshape(depth, mod_rows, 1, 6 * d)
    lat_row = lambda bi: bi
    ctx_row = lambda bi: bsz

    w_in_p, w_ab = _permute_w_in(w_in.astype(BF16))
    rope = _rope_tables(t)
    na_bias = _na_bias_table(na_rpb)
    par = jnp.zeros((depth, SUBLANE, LANE), F32)
    par = par.at[:, 0, :2 * DN_HEADS].set(dn_a_log.reshape(depth, -1)).at[:, 1, :2 * DN_HEADS].set(dn_dt_bias.reshape(depth, -1))
    s_zero = jnp.zeros((bsz, 2, DN_HEADS, DN_HEAD_DIM, DN_HEAD_DIM), F32)
    tm_lat = 1024 if t % 1024 == 0 else 512
    tm_ctx = ctx.shape[1]

    xc = ctx
    for l in range(depth):
        need_ctx = l < depth - 1
        g1 = norm1_g[l][None]
        g2 = norm2_g[l][None]
        wpa, wpb, wpc, wo = (a[l].astype(BF16) for a in (w_pa, w_pb, w_pc, w_o))
        wup, wdn = w_up[l].astype(BF16), w_down[l].astype(BF16)
        ng = dn_norm_g[l][None]
        p, ab = _inproj_call(x, g1, mod[l], lat_row, w_in_p[l], w_ab[l], tm_lat)
        lc_all = bsz * tm_ctx
        pc, abc = _inproj_call(xc.reshape(1, lc_all, d), g1, mod[l], ctx_row, w_in_p[l], w_ab[l],
                               tm_lat if lc_all % tm_lat == 0 else tm_ctx)
        pc, abc = pc.reshape(bsz, tm_ctx, -1), abc.reshape(bsz, tm_ctx, -1)
        ocf, ocb, s_ctx = _gdn_call(_gdn_prep_call(pc, dn_conv_w[l], None), abc, par[l], s_zero, True)
        olf, olb = _gdn_call(_gdn_prep_call(p, dn_conv_w[l], rope), ab, par[l], s_ctx, False)
        y_c = _na_call(p, pc, na_bias[l])
        x = _merge_call(olf, olb, p, y_c, x, mod[l], lat_row, wpa, wpb, wpc, wo, ng, sc_conv_w[l], 512)
        fg = final_norm_g[None] if l == depth - 1 else None
        x = _ffn_call(x, g2, mod[l], lat_row, wup, ffn_conv_w[l], ffn_conv_b[l][None], wdn, tm_lat, fg)
        if need_ctx:
            yc_c = _ctx_attn_call(pc)
            xc = _merge_call(ocf, ocb, pc, yc_c, xc, mod[l], ctx_row, wpa, wpb, wpc, wo, ng, sc_conv_w[l], tm_ctx)
            xc = _ffn_call(xc, g2, mod[l], ctx_row, wup, ffn_conv_w[l], ffn_conv_b[l][None], wdn, tm_ctx)
    return x
```

```python
import functools

import numpy as np
import jax
import jax.numpy as jnp
from jax import lax
from jax.experimental import pallas as pl
from jax.experimental.pallas import tpu as pltpu

F32 = jnp.float32
BF16 = jnp.bfloat16

GRID_W = 64
DN_HEADS = 4
DN_HEAD_DIM = 128
DN_WIDTH = DN_HEADS * DN_HEAD_DIM
DN_CHUNK = 64
GDN_TILE_CHUNKS = 4
FFN_DOWN_GROUP = 4
SC_WIDTH = 512
NA_HEADS = 8
NA_HEAD_DIM = 64
NA_WIDTH = NA_HEADS * NA_HEAD_DIM
NA_WIN_R = 8
NA_WIN_C = 16
NA_ROWS_PER_ITER = 8
ROPE_BASE = 10000.0
EPS = 1e-6
NEG = -1e30
LOG2E = 1.4426950408889634

LANE = 128
SUBLANE = 8
HALO = SUBLANE
HALO_P = 2 * SUBLANE
VMEM_LIMIT = 48 * 1024 * 1024

P_COLS = 3 * DN_WIDTH + DN_WIDTH + 3 * SC_WIDTH + 3 * NA_WIDTH + 3 * 1024
P_TILE = 2048
P_SLOT = 512
Z_SLOT, BG_SLOT, CG_SLOT, HH_SLOT = 3, 4, 5, 8
NAQ_SLOT, NAK_SLOT, NAV_SLOT = 9, 12, 13
GATE_SLOTS = (6, 10, 14)


def _dot(a, b):
    return jnp.dot(a.astype(BF16), b.astype(BF16), preferred_element_type=F32)


def _dot_nt(a, b):
    return lax.dot_general(a.astype(BF16), b.astype(BF16), (((1,), (1,)), ((), ())), preferred_element_type=F32)


def _dot_tn(a, b):
    return lax.dot_general(a.astype(BF16), b.astype(BF16), (((0,), (0,)), ((), ())), preferred_element_type=F32)


def _sigmoid(x):
    return 0.5 * jnp.tanh(0.5 * x) + 0.5


def _silu(x):
    return x * _sigmoid(x)


def _softplus(x):
    return jnp.maximum(x, 0.0) + jnp.log1p(jnp.exp(-jnp.abs(x)))


def _rms(x, eps=EPS):
    return x * lax.rsqrt(jnp.mean(x * x, axis=-1, keepdims=True) + eps)


def _params(*sem):
    return pltpu.CompilerParams(dimension_semantics=sem, vmem_limit_bytes=VMEM_LIMIT)


def _dwconv3_rows(full, w_ref, n, halo=HALO):
    rows = full.shape[0]
    dn = pltpu.roll(full, 1, axis=0)
    up = pltpu.roll(full, rows - 1, axis=0)
    out = dn * w_ref[0:1, :] + full * w_ref[1:2, :] + up * w_ref[2:3, :]
    return out[halo:halo + n]


def _ada_kernel(c_ref, w_ref, b_ref, o_ref):
    o_ref[...] = _dot(_silu(c_ref[...]), w_ref[...]) + b_ref[...]


def _ada_call(cc, w_ada, b_ada):
    depth, d, n6 = w_ada.shape
    r = cc.shape[0]
    tn = 1536
    return pl.pallas_call(
        _ada_kernel,
        grid=(depth, n6 // tn),
        in_specs=[
            pl.BlockSpec((r, d), lambda l, j: (0, 0)),
            pl.BlockSpec((None, d, tn), lambda l, j: (l, 0, j)),
            pl.BlockSpec((None, 1, tn), lambda l, j: (l, 0, j)),
        ],
        out_specs=pl.BlockSpec((None, r, tn), lambda l, j: (l, 0, j)),
        out_shape=jax.ShapeDtypeStruct((depth, r, n6), F32),
        compiler_params=_params("parallel", "parallel"),
        name="ada_mod",
    )(cc, w_ada, b_ada.reshape(depth, 1, n6))


def _inproj_kernel(x_ref, g_ref, sh_ref, sc_ref, w_ref, wab_ref, o_ref, ab_ref, h_ref):
    j = pl.program_id(2)

    @pl.when(j == 0)
    def _():
        h = (_rms(x_ref[...]) * g_ref[...]) * (1.0 + sc_ref[...]) + sh_ref[...]
        h_ref[...] = h.astype(BF16)
        ab_ref[...] = jnp.dot(h_ref[...], wab_ref[...], preferred_element_type=F32)

    res = jnp.dot(h_ref[...], w_ref[...], preferred_element_type=F32)
    half = res.shape[1] // 2
    o_ref[:, :half] = res[:, :half].astype(o_ref.dtype)
    act = res[:, half:]
    sg = _sigmoid(act)
    is_z = lax.broadcasted_iota(jnp.int32, (1, half), 1) >= half // 2
    o_ref[:, half:] = jnp.where(j > 0, sg, jnp.where(is_z, act * sg, act)).astype(o_ref.dtype)


def _inproj_call(x, g, mod, mrow, w, wab, tm):
    b, t, d = x.shape
    n = w.shape[1]
    tn = P_TILE
    return pl.pallas_call(
        _inproj_kernel,
        grid=(b, t // tm, n // tn),
        in_specs=[
            pl.BlockSpec((None, tm, d), lambda bi, i, j: (bi, i, 0)),
            pl.BlockSpec((1, d), lambda bi, i, j: (0, 0)),
            pl.BlockSpec((None, 1, d), lambda bi, i, j: (mrow(bi), 0, 0)),
            pl.BlockSpec((None, 1, d), lambda bi, i, j: (mrow(bi), 0, 1)),
            pl.BlockSpec((d, tn), lambda bi, i, j: (0, j)),
            pl.BlockSpec((d, LANE), lambda bi, i, j: (0, 0)),
        ],
        out_specs=[pl.BlockSpec((None, tm, tn), lambda bi, i, j: (bi, i, j)),
                   pl.BlockSpec((None, tm, LANE), lambda bi, i, j: (bi, i, 0))],
        out_shape=[jax.ShapeDtypeStruct((b, t, n), BF16), jax.ShapeDtypeStruct((b, t, LANE), F32)],
        scratch_shapes=[pltpu.VMEM((tm, d), BF16)],
        compiler_params=_params("parallel", "parallel", "arbitrary"),
        name="inproj",
    )(x, g, mod, mod, w, wab)


def _swap_pairs(x, even):
    n = x.shape[-1]
    return jnp.where(even, pltpu.roll(x, n - 1, axis=1), pltpu.roll(x, 1, axis=1))


def _gdn_prep_kernel(*refs, nt, use_rope):
    x_ref, xp_ref, xn_ref, cw_ref = refs[:4]
    cos_ref, sin_ref = refs[4:6] if use_rope else (None, None)
    o_ref = refs[-1]
    i = pl.program_id(1)
    tr = x_ref.shape[0]
    hd = DN_HEAD_DIM
    has_prev = (i > 0).astype(F32)
    has_next = (i < nt - 1).astype(F32)
    full = jnp.concatenate([xp_ref[...].astype(F32) * has_prev, x_ref[...].astype(F32),
                            xn_ref[...].astype(F32) * has_next], axis=0)
    act = _silu(_dwconv3_rows(full, cw_ref, tr, HALO_P))
    even = (lax.broadcasted_iota(jnp.int32, (tr, hd), 1) % 2) == 0
    for off, scale in ((0, hd ** -0.5), (DN_WIDTH, 1.0)):
        for h in range(DN_HEADS):
            sl = slice(off + h * hd, off + (h + 1) * hd)
            xh = act[:, sl]
            xh = xh * lax.rsqrt(jnp.sum(xh * xh, axis=-1, keepdims=True) + EPS)
            if use_rope:
                xh = xh * cos_ref[...] + _swap_pairs(xh, even) * sin_ref[...]
            o_ref[:, sl] = (xh * scale).astype(o_ref.dtype)
    o_ref[:, 2 * DN_WIDTH:] = act[:, 2 * DN_WIDTH:].astype(o_ref.dtype)


def _gdn_prep_call(p, conv_w, rope):
    b, t, _ = p.shape
    tr = GDN_TILE_CHUNKS * DN_CHUNK
    nt = t // tr
    hb = tr // HALO_P
    nhb = t // HALO_P
    qkv_w = 3 * DN_WIDTH
    in_specs = [
        pl.BlockSpec((None, tr, qkv_w), lambda bi, i: (bi, i, 0)),
        pl.BlockSpec((None, HALO_P, qkv_w), lambda bi, i: (bi, jnp.maximum(i * hb - 1, 0), 0)),
        pl.BlockSpec((None, HALO_P, qkv_w), lambda bi, i: (bi, jnp.minimum(i * hb + hb, nhb - 1), 0)),
        pl.BlockSpec(conv_w.shape, lambda bi, i: (0, 0)),
    ]
    args = [p, p, p, conv_w]
    if rope is not None:
        in_specs += [pl.BlockSpec((tr, DN_HEAD_DIM), lambda bi, i: (i, 0))] * 2
        args += [rope[0], rope[1]]
    return pl.pallas_call(
        functools.partial(_gdn_prep_kernel, nt=nt, use_rope=rope is not None),
        grid=(b, nt),
        in_specs=in_specs,
        out_specs=pl.BlockSpec((None, tr, qkv_w), lambda bi, i: (bi, i, 0)),
        out_shape=jax.ShapeDtypeStruct((b, t, qkv_w), p.dtype),
        compiler_params=_params("parallel", "parallel"),
        name="gdn_prep",
    )(*args)


def _gdn_kernel(*refs, nt, emit_state):
    it = iter(refs)
    xs = [[next(it) for _ in range(2)] for _ in range(2)]
    par_ref = next(it)
    s0_ref = next(it)
    o_refs = [next(it), next(it)]
    sout_ref = next(it) if emit_state else None
    s_scr, uw_scr, qkd_scr = (next(it) for _ in range(3))

    c = DN_CHUNK
    nh = DN_HEADS
    hd = DN_HEAD_DIM
    nch = GDN_TILE_CHUNKS
    tr = nch * c
    cw = nh * c
    lc = c.bit_length() - 1
    i = pl.program_id(1)

    @pl.when(i == 0)
    def _():
        s_scr[...] = s0_ref[...]

    ri = lax.broadcasted_iota(jnp.int32, (c, cw), 0)
    li = lax.broadcasted_iota(jnp.int32, (c, cw), 1)
    jj = li & (c - 1)
    seg = li >> lc
    eye_cat = ri == jj
    bd_sq = jnp.where((lax.broadcasted_iota(jnp.int32, (cw, cw), 0) >> lc)
                      == (lax.broadcasted_iota(jnp.int32, (cw, cw), 1) >> lc), 1.0, 0.0).astype(BF16)
    bd_k = jnp.where((lax.broadcasted_iota(jnp.int32, (cw, nh * hd), 0) >> lc)
                     == (lax.broadcasted_iota(jnp.int32, (cw, nh * hd), 1) // hd), 1.0, 0.0).astype(BF16)
    rowi = lax.broadcasted_iota(jnp.int32, (tr, LANE), 0) & (c - 1)
    neg_a = -jnp.exp(par_ref[0:1, :])
    dt_b = par_ref[1:2, :]

    def col_bcast(x, j, width):
        return jnp.broadcast_to(x[:, j:j + 1], (x.shape[0], width))

    def colcat(x, base):
        out = col_bcast(x, base + nh - 1, cw)
        for h in range(nh - 2, -1, -1):
            out = jnp.where(seg == h, col_bcast(x, base + h, cw), out)
        return out

    def block_diag(m, mask):
        return jnp.concatenate([m.astype(BF16)] * nh, axis=0) * mask

    nar = []
    for d in range(2):
        ab = xs[d][1][...]
        gc = neg_a * _softplus(ab + dt_b)
        beta = pltpu.roll(_sigmoid(ab), LANE - 2 * nh, axis=1)
        sft = 1
        while sft < c:
            if d == 0:
                gc = gc + jnp.where(rowi >= sft, pltpu.roll(gc, sft, axis=0), 0.0)
            else:
                gc = gc + jnp.where(rowi < c - sft, pltpu.roll(gc, tr - sft, axis=0), 0.0)
            sft *= 2
        e1 = jnp.exp(gc)
        nar.append(dict(gc=gc, e1=e1, beta=beta, be=beta * e1))

    st = {}

    def setup(key):
        d, ch = key
        rs = slice(ch * c, (ch + 1) * c)
        base = d * nh
        incl = (ri >= jj) if d == 0 else (ri <= jj)
        gcc = colcat(nar[d]["gc"][rs], base)
        gcr = jnp.broadcast_to(jnp.sum(jnp.where(eye_cat, gcc, 0.0), axis=0, keepdims=True), (c, cw))
        decay = jnp.where(incl, jnp.exp(jnp.where(incl, gcc - gcr, 0.0)), 0.0)
        kf = xs[d][0][rs, DN_WIDTH:2 * DN_WIDTH]
        prod = _dot_nt(jnp.concatenate([xs[d][0][rs, :DN_WIDTH], kf], axis=0), block_diag(kf, bd_k))
        qkd_scr[d, ch] = prod[:c] * decay
        amat = jnp.where(incl & (~eye_cat), colcat(nar[d]["beta"][rs], base) * prod[c:] * decay, 0.0)
        lvl1 = ((ri >> 1) == (jj >> 1))
        st[key] = dict(amat=amat.astype(BF16), tinv=jnp.where(eye_cat, 1.0, 0.0) - jnp.where(lvl1, amat, 0.0))

    def invert_level(keys, s):
        ls = s.bit_length() - 1
        lvl = jnp.where(((ri >> (ls + 1)) == (jj >> (ls + 1))) & ((ri >> ls) != (jj >> ls)), 1.0, 0.0).astype(BF16)
        for key in keys:
            p = st[key]
            p["t_bf"] = p["tinv"].astype(BF16)
            p["x"] = _dot(p["t_bf"], block_diag(p["amat"] * lvl, bd_sq))
        for key in keys:
            p = st[key]
            p["tinv"] = p["tinv"] - _dot(p["x"], block_diag(p["t_bf"], bd_sq))

    def solve(key):
        d, ch = key
        rs = slice(ch * c, (ch + 1) * c)
        base = d * nh
        x_ref = xs[d][0]
        rhs = jnp.concatenate([
            jnp.concatenate([x_ref[rs, 2 * DN_WIDTH + h * hd:2 * DN_WIDTH + (h + 1) * hd].astype(F32)
                             * col_bcast(nar[d]["beta"][rs], base + h, hd),
                             x_ref[rs, DN_WIDTH + h * hd:DN_WIDTH + (h + 1) * hd].astype(F32)
                             * col_bcast(nar[d]["be"][rs], base + h, hd)], axis=1)
            for h in range(nh)], axis=0)
        uw_scr[d, ch] = _dot(block_diag(st.pop(key)["tinv"], bd_sq), rhs)

    states = [[s_scr[d, h] for h in range(nh)] for d in range(2)]

    def scan_step(step):
        cur = [(0, step), (1, nch - 1 - step)]
        loc = {}
        for d, ch in cur:
            rs = slice(ch * c, (ch + 1) * c)
            base = d * nh
            gc_c = nar[d]["gc"][rs]
            gtot = gc_c[c - 1:c] if d == 0 else gc_c[0:1]
            e2 = jnp.exp(gtot - gc_c)
            r_ = []
            for h in range(nh):
                hs = slice(h * hd, (h + 1) * hd)
                w_h = uw_scr[d, ch, h * c:(h + 1) * c, hd:]
                qd_h = xs[d][0][rs, hs].astype(F32) * col_bcast(nar[d]["e1"][rs], base + h, hd)
                r_.append(_dot(jnp.concatenate([w_h, qd_h], axis=0), states[d][h]))
            loc[d] = dict(r=r_, e2=e2, gtot=gtot)
        for d, ch in cur:
            loc[d]["vn"] = [uw_scr[d, ch, h * c:(h + 1) * c, :hd] - loc[d]["r"][h][:c] for h in range(nh)]
        for d, ch in cur:
            rs = slice(ch * c, (ch + 1) * c)
            base = d * nh
            vn = loc[d]["vn"]
            o2 = _dot(block_diag(qkd_scr[d, ch], bd_sq), jnp.concatenate(vn, axis=0))
            for h in range(nh):
                hs = slice(h * hd, (h + 1) * hd)
                kd_h = (xs[d][0][rs, DN_WIDTH + h * hd:DN_WIDTH + (h + 1) * hd].astype(F32)
                        * col_bcast(loc[d]["e2"], base + h, hd))
                o_refs[d][rs, hs] = (loc[d]["r"][h][c:] + o2[h * c:(h + 1) * c]).astype(o_refs[d].dtype)
                eg = jnp.exp(jnp.broadcast_to(loc[d]["gtot"][0:1, base + h:base + h + 1], (hd, hd)))
                states[d][h] = states[d][h] * eg + _dot_tn(kd_h, vn[h])

    probs = [(d, ch) for ch in range(nch) for d in range(2)]
    for key in probs:
        setup(key)
    for e in range(1, lc):
        invert_level(probs, 1 << e)
    for key in probs:
        solve(key)
    for step in range(nch):
        scan_step(step)
    for d in range(2):
        for h in range(nh):
            s_scr[d, h] = states[d][h]

    if emit_state:
        @pl.when(i == nt - 1)
        def _():
            sout_ref[...] = s_scr[...]


def _gdn_call(qkv, ab, par, s0, emit_state):
    b, t, qkv_w = qkv.shape
    c = DN_CHUNK
    tr = GDN_TILE_CHUNKS * c
    nt = t // tr

    def tile_of(d, i):
        return i if d == 0 else nt - 1 - i

    in_specs = []
    args = []
    for d in range(2):
        in_specs += [
            pl.BlockSpec((None, tr, qkv_w), lambda bi, i, d=d: (bi, tile_of(d, i), 0)),
            pl.BlockSpec((None, tr, LANE), lambda bi, i, d=d: (bi, tile_of(d, i), 0)),
        ]
        args += [qkv, ab]
    in_specs += [pl.BlockSpec(par.shape, lambda bi, i: (0, 0))]
    args += [par]
    sshape = (2, DN_HEADS, DN_HEAD_DIM, DN_HEAD_DIM)
    in_specs += [pl.BlockSpec((None,) + sshape, lambda bi, i: (bi, 0, 0, 0, 0))]
    args += [s0]
    out_specs = [
        pl.BlockSpec((None, tr, DN_WIDTH), lambda bi, i: (bi, i, 0)),
        pl.BlockSpec((None, tr, DN_WIDTH), lambda bi, i: (bi, nt - 1 - i, 0)),
    ]
    out_shape = [jax.ShapeDtypeStruct((b, t, DN_WIDTH), qkv.dtype)] * 2
    if emit_state:
        out_specs += [pl.BlockSpec((None,) + sshape, lambda bi, i: (bi, 0, 0, 0, 0))]
        out_shape += [jax.ShapeDtypeStruct((b,) + sshape, F32)]
    cw = DN_HEADS * c
    scratch = [
        pltpu.VMEM(sshape, F32),
        pltpu.VMEM((2, GDN_TILE_CHUNKS, cw, 2 * DN_HEAD_DIM), F32),
        pltpu.VMEM((2, GDN_TILE_CHUNKS, c, cw), F32),
    ]
    return pl.pallas_call(
        functools.partial(_gdn_kernel, nt=nt, emit_state=emit_state),
        grid=(b, nt),
        in_specs=in_specs,
        out_specs=out_specs,
        out_shape=out_shape,
        scratch_shapes=scratch,
        compiler_params=_params("parallel", "arbitrary"),
        name="gdn_ctx" if emit_state else "gdn_lat",
    )(*args)


def _na_kernel(q_ref, k_ref, v_ref, kc_ref, vc_ref, bias_ref, o_ref, *, rows):
    w = GRID_W
    nl = NA_WIN_R * w
    nr = NA_ROWS_PER_ITER
    scale = NA_HEAD_DIM ** -0.5 * LOG2E
    kc = kc_ref[...].astype(BF16)
    vc = vc_ref[...].astype(BF16)

    def stack_heads(q2):
        sel = lax.broadcasted_iota(jnp.int32, q2.shape, 1) < NA_HEAD_DIM
        q2 = q2.astype(BF16)
        zero = jnp.zeros_like(q2)
        return jnp.concatenate([jnp.where(sel, q2, zero), jnp.where(sel, zero, q2)], axis=0)

    def unstack_heads(x):
        n = x.shape[0] // 2
        sel = lax.broadcasted_iota(jnp.int32, (n, LANE), 1) < NA_HEAD_DIM
        return jnp.where(sel, jnp.broadcast_to(x[:n], (n, LANE)), jnp.broadcast_to(x[n:], (n, LANE)))

    def row_block(it, carry):
        rr = [it * nr + u for u in range(nr)]
        rss = [jnp.clip(r - NA_WIN_R // 2, 0, rows - NA_WIN_R) for r in rr]
        rsl = [pl.ds(pl.multiple_of(r * w, w), w) for r in rr]
        wsl = [pl.ds(pl.multiple_of(rs * w, w), nl) for rs in rss]
        q_all = q_ref[pl.ds(pl.multiple_of(it * (nr * w), nr * w), nr * w), :].astype(F32) * scale
        s_c = _dot_nt(stack_heads(q_all), kc)
        s = [_dot_nt(stack_heads(q_all[u * w:(u + 1) * w]), k_ref[wsl[u], :]) + bias_ref[rss[u] - rr[u] + NA_WIN_R - 1]
             for u in range(nr)]
        m_c = jnp.max(s_c, axis=-1, keepdims=True)
        m_l = [jnp.max(x, axis=-1, keepdims=True) for x in s]
        p_c = jnp.exp2(s_c - m_c)
        p = [jnp.exp2(x - m) for x, m in zip(s, m_l)]
        o_c = unstack_heads(_dot(p_c, vc))
        o_l = [_dot(p[u], v_ref[wsl[u], :]) for u in range(nr)]
        l_c = unstack_heads(jnp.sum(p_c, axis=-1, keepdims=True))
        m_c = unstack_heads(m_c)
        for u in range(nr):
            usl = slice(u * w, (u + 1) * w)
            l_u = unstack_heads(jnp.sum(p[u], axis=-1, keepdims=True))
            m_u = unstack_heads(m_l[u])
            m = jnp.maximum(m_u, m_c[usl])
            a_l = jnp.exp2(m_u - m)
            a_c = jnp.exp2(m_c[usl] - m)
            out = (unstack_heads(o_l[u]) * a_l + o_c[usl] * a_c) / (l_u * a_l + l_c[usl] * a_c)
            o_ref[rsl[u], :] = out.astype(o_ref.dtype)
        return carry

    lax.fori_loop(0, rows // nr, row_block, 0)


def _na_call(p, pc, bias):
    b, t, _ = p.shape
    l = pc.shape[1]
    rows = t // GRID_W
    npair = NA_HEADS // 2
    return pl.pallas_call(
        functools.partial(_na_kernel, rows=rows),
        grid=(npair, b),
        in_specs=[
            pl.BlockSpec((None, t, LANE), lambda j, bi: (bi, 0, NAQ_SLOT * (P_SLOT // LANE) + j)),
            pl.BlockSpec((None, t, LANE), lambda j, bi: (bi, 0, NAK_SLOT * (P_SLOT // LANE) + j)),
            pl.BlockSpec((None, t, LANE), lambda j, bi: (bi, 0, NAV_SLOT * (P_SLOT // LANE) + j)),
            pl.BlockSpec((None, l, LANE), lambda j, bi: (bi, 0, NAK_SLOT * (P_SLOT // LANE) + j)),
            pl.BlockSpec((None, l, LANE), lambda j, bi: (bi, 0, NAV_SLOT * (P_SLOT // LANE) + j)),
            pl.BlockSpec((None,) + bias.shape[1:], lambda j, bi: (j, 0, 0, 0)),
        ],
        out_specs=pl.BlockSpec((None, t, LANE), lambda j, bi: (bi, 0, j)),
        out_shape=jax.ShapeDtypeStruct((b, t, NA_WIDTH), p.dtype),
        compiler_params=_params("parallel", "parallel"),
        name="na_lat",
    )(p, p, p, pc, pc, bias)


def _ctx_attn_kernel(q_ref, k_ref, v_ref, o_ref):
    scale = NA_HEAD_DIM ** -0.5
    q2 = q_ref[...] * scale
    k2 = k_ref[...]
    v2 = v_ref[...]
    head0 = lax.broadcasted_iota(jnp.int32, q2.shape, 1) < NA_HEAD_DIM
    out = None
    for hh in range(2):
        sel = head0 if hh == 0 else ~head0
        s = _dot_nt(jnp.where(sel, q2, jnp.zeros_like(q2)), k2)
        p = jnp.exp(s - jnp.max(s, axis=-1, keepdims=True))
        o = _dot(p, v2) / jnp.sum(p, axis=-1, keepdims=True)
        out = o if out is None else jnp.where(sel, o, out)
    o_ref[...] = out.astype(o_ref.dtype)


def _ctx_attn_call(pc):
    b, l, _ = pc.shape
    npair = NA_HEADS // 2
    return pl.pallas_call(
        _ctx_attn_kernel,
        grid=(b, npair),
        in_specs=[
            pl.BlockSpec((None, l, LANE), lambda bi, j: (bi, 0, NAQ_SLOT * (P_SLOT // LANE) + j)),
            pl.BlockSpec((None, l, LANE), lambda bi, j: (bi, 0, NAK_SLOT * (P_SLOT // LANE) + j)),
            pl.BlockSpec((None, l, LANE), lambda bi, j: (bi, 0, NAV_SLOT * (P_SLOT // LANE) + j)),
        ],
        out_specs=pl.BlockSpec((None, l, LANE), lambda bi, j: (bi, 0, j)),
        out_shape=jax.ShapeDtypeStruct((b, l, NA_WIDTH), pc.dtype),
        compiler_params=_params("parallel", "parallel"),
        name="ctx_attn",
    )(pc, pc, pc)


def _merge_kernel(of_ref, ob_ref, z_ref, bg_ref, cg_ref, hh_ref, cgp_ref, cgn_ref, hhp_ref, hhn_ref, yc_ref,
                  ga_ref, gb_ref, gcg_ref, x_ref, gt_ref, wpa_ref, wpb_ref, wpc_ref, wo_ref, ng_ref, scw_ref,
                  o_ref, *, nt):
    i = pl.program_id(1)
    tm = x_ref.shape[0]
    o = of_ref[...].astype(F32) + ob_ref[...].astype(F32)
    sz = z_ref[...].astype(F32)
    ya = []
    for h in range(DN_HEADS):
        sl = slice(h * DN_HEAD_DIM, (h + 1) * DN_HEAD_DIM)
        ya.append(_rms(o[:, sl]) * ng_ref[...] * sz[:, sl])
    y_a = jnp.concatenate(ya, axis=1)
    has_prev = (i > 0).astype(F32)
    has_next = (i < nt - 1).astype(F32)
    f32 = lambda r: r[...].astype(F32)
    full = jnp.concatenate([f32(cgp_ref) * f32(hhp_ref) * has_prev, f32(cg_ref) * f32(hh_ref),
                            f32(cgn_ref) * f32(hhn_ref) * has_next], axis=0)
    y_b = f32(bg_ref) * _dwconv3_rows(full, scw_ref, tm, HALO_P)
    y = (f32(ga_ref) * _dot(y_a, wpa_ref[...]) + f32(gb_ref) * _dot(y_b, wpb_ref[...])
         + f32(gcg_ref) * _dot(yc_ref[...], wpc_ref[...]))
    o_ref[...] = x_ref[...] + gt_ref[...] * _dot(y, wo_ref[...])


def _merge_call(of, ob, p, yc, x, mod, mrow, wpa, wpb, wpc, wo, ng, scw, tm):
    b, t, d = x.shape
    nt = t // tm
    hb = tm // HALO_P
    nhb = t // HALO_P
    w5 = P_SLOT

    def tile(width, col):
        return pl.BlockSpec((None, tm, width), lambda bi, i: (bi, i, col))

    def prev(col):
        return pl.BlockSpec((None, HALO_P, w5), lambda bi, i: (bi, jnp.maximum(i * hb - 1, 0), col))

    def nxt(col):
        return pl.BlockSpec((None, HALO_P, w5), lambda bi, i: (bi, jnp.minimum(i * hb + hb, nhb - 1), col))

    def full(a):
        return pl.BlockSpec(a.shape, lambda bi, i: (0,) * a.ndim)

    in_specs = [
        tile(DN_WIDTH, 0), tile(DN_WIDTH, 0),
        tile(w5, Z_SLOT), tile(w5, BG_SLOT), tile(w5, CG_SLOT), tile(w5, HH_SLOT),
        prev(CG_SLOT), nxt(CG_SLOT), prev(HH_SLOT), nxt(HH_SLOT),
        tile(NA_WIDTH, 0),
        tile(d, GATE_SLOTS[0] // 2), tile(d, GATE_SLOTS[1] // 2), tile(d, GATE_SLOTS[2] // 2),
        tile(d, 0),
        pl.BlockSpec((None, 1, d), lambda bi, i: (mrow(bi), 0, 2)),
        full(wpa), full(wpb), full(wpc), full(wo), full(ng), full(scw),
    ]
    return pl.pallas_call(
        functools.partial(_merge_kernel, nt=nt),
        grid=(b, nt),
        in_specs=in_specs,
        out_specs=pl.BlockSpec((None, tm, d), lambda bi, i: (bi, i, 0)),
        out_shape=jax.ShapeDtypeStruct((b, t, d), F32),
        compiler_params=_params("parallel", "parallel"),
        name="merge",
    )(of, ob, p, p, p, p, p, p, p, p, yc, p, p, p, x, mod, wpa, wpb, wpc, wo, ng, scw)


def _ffn_kernel(*refs, nt, nk, fc, final):
    it = iter(refs)
    x_ref, xp_ref, xn_ref, g_ref, sh_ref, sc_ref, gt_ref = (next(it) for _ in range(7))
    wup_ref, cw_ref, cb_ref, wd_ref = (next(it) for _ in range(4))
    fg_ref = next(it) if final else None
    o_ref, h_scr, act_scr = next(it), next(it), next(it)
    i = pl.program_id(1)
    tm = x_ref.shape[0]
    dff = nk * fc

    xfull = jnp.concatenate([xp_ref[...], x_ref[...], xn_ref[...]], axis=0)
    h = (_rms(xfull) * g_ref[...]) * (1.0 + sc_ref[...]) + sh_ref[...]
    h_scr[...] = h.astype(BF16)

    row = lax.broadcasted_iota(jnp.int32, (tm + 2 * HALO, 1), 0)
    keep = ((row >= HALO) | (i > 0)) & ((row < tm + HALO) | (i < nt - 1))

    def up_proj(k):
        return [jnp.where(keep, jnp.dot(h_scr[...], wup_ref[:, c0:c0 + fc], preferred_element_type=F32), 0.0)
                for c0 in (k * fc, dff + k * fc)]

    ups = up_proj(0)
    g0 = 0
    for k in range(nk):
        nxt = up_proj(k + 1) if k + 1 < nk else None
        a = _dwconv3_rows(ups[0], cw_ref[:, k * fc:(k + 1) * fc], tm) + cb_ref[:, k * fc:(k + 1) * fc]
        bb = (_dwconv3_rows(ups[1], cw_ref[:, dff + k * fc:dff + (k + 1) * fc], tm)
              + cb_ref[:, dff + k * fc:dff + (k + 1) * fc])
        act_scr[:, k * fc:(k + 1) * fc] = (_silu(a) * bb).astype(BF16)
        if (k + 1) % FFN_DOWN_GROUP == 0 or k == nk - 1:
            g1 = (k + 1) * fc
            contrib = jnp.dot(act_scr[:, g0:g1], wd_ref[g0:g1, :], preferred_element_type=F32)
            if g0 == 0:
                o_ref[...] = contrib
            else:
                o_ref[...] += contrib
            g0 = g1
        ups = nxt

    out = x_ref[...] + gt_ref[...] * o_ref[...]
    if final:
        out = _rms(out) * fg_ref[...]
    o_ref[...] = out


def _ffn_call(x, g, mod, mrow, w_up, cw, cb, w_down, tm, final_g=None):
    b, t, d = x.shape
    dff = w_down.shape[0]
    fc = 256
    nk = dff // fc
    nt = t // tm
    hb = tm // HALO
    nhb = t // HALO
    final = final_g is not None

    def resident(a):
        return pl.BlockSpec(a.shape, lambda bi, i: (0,) * a.ndim, pipeline_mode=pl.Buffered(1))

    in_specs = [
        pl.BlockSpec((None, tm, d), lambda bi, i: (bi, i, 0)),
        pl.BlockSpec((None, HALO, d), lambda bi, i: (bi, jnp.maximum(i * hb - 1, 0), 0)),
        pl.BlockSpec((None, HALO, d), lambda bi, i: (bi, jnp.minimum(i * hb + hb, nhb - 1), 0)),
        pl.BlockSpec((1, d), lambda bi, i: (0, 0)),
        pl.BlockSpec((None, 1, d), lambda bi, i: (mrow(bi), 0, 3)),
        pl.BlockSpec((None, 1, d), lambda bi, i: (mrow(bi), 0, 4)),
        pl.BlockSpec((None, 1, d), lambda bi, i: (mrow(bi), 0, 5)),
        resident(w_up), resident(cw), resident(cb), resident(w_down),
    ]
    args = [x, x, x, g, mod, mod, mod, w_up, cw, cb, w_down]
    if final:
        in_specs.append(pl.BlockSpec((1, d), lambda bi, i: (0, 0)))
        args.append(final_g)
    return pl.pallas_call(
        functools.partial(_ffn_kernel, nt=nt, nk=nk, fc=fc, final=final),
        grid=(b, nt),
        in_specs=in_specs,
        out_specs=pl.BlockSpec((None, tm, d), lambda bi, i: (bi, i, 0)),
        out_shape=jax.ShapeDtypeStruct((b, t, d), F32),
        scratch_shapes=[pltpu.VMEM((tm + 2 * HALO, d), BF16), pltpu.VMEM((tm, dff), BF16)],
        compiler_params=_params("parallel", "parallel"),
        name="ffn_final" if final else "ffn",
    )(*args)


def _rope_tables(t):
    tok = jnp.arange(t, dtype=jnp.int32)
    rows = (tok // GRID_W).astype(F32)
    cols = (tok % GRID_W).astype(F32)
    nf = DN_HEAD_DIM // 4
    inv = ROPE_BASE ** (-jnp.arange(nf, dtype=F32) / nf)
    ang = jnp.concatenate([rows[:, None] * inv, cols[:, None] * inv], axis=-1)
    cos = jnp.repeat(jnp.cos(ang), 2, axis=-1)
    sin = jnp.repeat(jnp.sin(ang), 2, axis=-1)
    sign = jnp.where(jnp.arange(DN_HEAD_DIM) % 2 == 0, -1.0, 1.0).astype(F32)
    return cos, sin * sign


def _na_bias_table(rpb):
    w = GRID_W
    depth = rpb.shape[0]
    qc = np.arange(w)[:, None]
    kc = np.arange(w)[None, :]
    cstart = np.clip(qc - NA_WIN_C // 2, 0, w - NA_WIN_C)
    valid = (kc >= cstart) & (kc < cstart + NA_WIN_C)
    onehot = ((kc - qc + NA_WIN_C - 1)[None] == np.arange(2 * NA_WIN_C - 1)[:, None, None]) & valid[None]
    tbl = jnp.einsum("lhdj,jck->lhdck", rpb, jnp.asarray(onehot, F32), precision=lax.Precision.HIGHEST)
    tbl = tbl * LOG2E + jnp.asarray(np.where(valid, 0.0, NEG), F32)
    tbl = jnp.stack([tbl[:, :, d0:d0 + NA_WIN_R] for d0 in range(NA_WIN_R)], axis=2)
    tbl = tbl.transpose(0, 1, 2, 4, 3, 5).reshape(depth, NA_HEADS // 2, 2, NA_WIN_R, w, NA_WIN_R * w)
    return tbl.transpose(0, 1, 3, 2, 4, 5).reshape(depth, NA_HEADS // 2, NA_WIN_R, 2 * w, NA_WIN_R * w)


def _permute_w_in(w_in):
    s = np.cumsum([0, 3 * DN_WIDTH, DN_WIDTH, 2 * DN_HEADS, 2 * DN_HEADS, 3 * SC_WIDTH, 3 * NA_WIDTH, 3 * 1024])
    qkv, z, a, b, sc, na, gates = (w_in[..., s[j]:s[j + 1]] for j in range(7))
    bg, cg, hh = (sc[..., j * P_SLOT:(j + 1) * P_SLOT] for j in range(3))
    naq, nak, nav = (na[..., j * P_SLOT:(j + 1) * P_SLOT] for j in range(3))
    ga, gb, gc = (gates[..., j * 2 * P_SLOT:(j + 1) * 2 * P_SLOT] for j in range(3))
    pad = jnp.zeros(w_in.shape[:-1] + (LANE - 4 * DN_HEADS,), w_in.dtype)
    return (jnp.concatenate([qkv, z, bg, cg, ga, hh, naq, gb, nak, nav, gc], axis=-1),
            jnp.concatenate([a, b, pad], axis=-1))


def kernel(x, c, ctx, c_ctx, norm1_g, norm2_g, w_ada, b_ada, w_in, dn_conv_w, dn_a_log, dn_dt_bias, dn_norm_g, sc_conv_w, na_rpb, w_pa, w_pb, w_pc, w_o, w_up, ffn_conv_w, ffn_conv_b, w_down, final_norm_g):
    bsz, t, d = x.shape
    depth = w_in.shape[0]
    assert d == 1024 and t % (NA_WIN_R * GRID_W) == 0 and ctx.shape[1] % (GDN_TILE_CHUNKS * DN_CHUNK) == 0

    mod_rows = -(-(bsz + 1) // SUBLANE) * SUBLANE
    cc = jnp.zeros((mod_rows, d), F32).at[:bsz].set(c).at[bsz].set(c_ctx)
    mod = _ada_call(cc, w_ada, b_ada).reshape(depth, mod_rows, 1, 6 * d)
    lat_row = lambda bi: bi
    ctx_row = lambda bi: bsz

    w_in_p, w_ab = _permute_w_in(w_in.astype(BF16))
    rope = _rope_tables(t)
    na_bias = _na_bias_table(na_rpb)
    par = jnp.zeros((depth, SUBLANE, LANE), F32)
    par = par.at[:, 0, :2 * DN_HEADS].set(dn_a_log.reshape(depth, -1)).at[:, 1, :2 * DN_HEADS].set(dn_dt_bias.reshape(depth, -1))
    s_zero = jnp.zeros((bsz, 2, DN_HEADS, DN_HEAD_DIM, DN_HEAD_DIM), F32)
    tm_lat = 1024 if t % 1024 == 0 else 512
    tm_ctx = ctx.shape[1]

    xc = ctx
    for l in range(depth):
        need_ctx = l < depth - 1
        g1 = norm1_g[l][None]
        g2 = norm2_g[l][None]
        wpa, wpb, wpc, wo = (a[l].astype(BF16) for a in (w_pa, w_pb, w_pc, w_o))
        wup, wdn = w_up[l].astype(BF16), w_down[l].astype(BF16)
        ng = dn_norm_g[l][None]
        p, ab = _inproj_call(x, g1, mod[l], lat_row, w_in_p[l], w_ab[l], tm_lat)
        lc_all = bsz * tm_ctx
        pc, abc = _inproj_call(xc.reshape(1, lc_all, d), g1, mod[l], ctx_row, w_in_p[l], w_ab[l],
                               tm_lat if lc_all % tm_lat == 0 else tm_ctx)
        pc, abc = pc.reshape(bsz, tm_ctx, -1), abc.reshape(bsz, tm_ctx, -1)
        ocf, ocb, s_ctx = _gdn_call(_gdn_prep_call(pc, dn_conv_w[l], None), abc, par[l], s_zero, True)
        olf, olb = _gdn_call(_gdn_prep_call(p, dn_conv_w[l], rope), ab, par[l], s_ctx, False)
        y_c = _na_call(p, pc, na_bias[l])
        x = _merge_call(olf, olb, p, y_c, x, mod[l], lat_row, wpa, wpb, wpc, wo, ng, sc_conv_w[l], 512)
        fg = final_norm_g[None] if l == depth - 1 else None
        x = _ffn_call(x, g2, mod[l], lat_row, wup, ffn_conv_w[l], ffn_conv_b[l][None], wdn, tm_lat, fg)
        if need_ctx:
            yc_c = _ctx_attn_call(pc)
            xc = _merge_call(ocf, ocb, pc, yc_c, xc, mod[l], ctx_row, wpa, wpb, wpc, wo, ng, sc_conv_w[l], tm_ctx)
            xc = _ffn_call(xc, g2, mod[l], ctx_row, wup, ffn_conv_w[l], ffn_conv_b[l][None], wdn, tm_ctx)
    return x
```

```python
import functools

import numpy as np
import jax
import jax.numpy as jnp
from jax import lax
from jax.experimental import pallas as pl
from jax.experimental.pallas import tpu as pltpu

F32 = jnp.float32
BF16 = jnp.bfloat16

GRID_W = 64
DN_HEADS = 4
DN_HEAD_DIM = 128
DN_WIDTH = DN_HEADS * DN_HEAD_DIM
DN_CHUNK = 64
GDN_TILE_CHUNKS = 4
FFN_DOWN_GROUP = 4
SC_WIDTH = 512
NA_HEADS = 8
NA_HEAD_DIM = 64
NA_WIDTH = NA_HEADS * NA_HEAD_DIM
NA_WIN_R = 8
NA_WIN_C = 16
NA_ROWS_PER_ITER = 8
ROPE_BASE = 10000.0
EPS = 1e-6
NEG = -1e30
LOG2E = 1.4426950408889634

LANE = 128
SUBLANE = 8
HALO = SUBLANE
HALO_P = 2 * SUBLANE
VMEM_LIMIT = 48 * 1024 * 1024

P_COLS = 3 * DN_WIDTH + DN_WIDTH + 3 * SC_WIDTH + 3 * NA_WIDTH + 3 * 1024
P_TILE = 2048
P_SLOT = 512
Z_SLOT, BG_SLOT, CG_SLOT, HH_SLOT = 3, 4, 5, 8
NAQ_SLOT, NAK_SLOT, NAV_SLOT = 9, 12, 13
GATE_SLOTS = (6, 10, 14)


def _dot(a, b):
    return jnp.dot(a.astype(BF16), b.astype(BF16), preferred_element_type=F32)


def _dot_nt(a, b):
    return lax.dot_general(a.astype(BF16), b.astype(BF16), (((1,), (1,)), ((), ())), preferred_element_type=F32)


def _dot_tn(a, b):
    return lax.dot_general(a.astype(BF16), b.astype(BF16), (((0,), (0,)), ((), ())), preferred_element_type=F32)


def _sigmoid(x):
    return 0.5 * jnp.tanh(0.5 * x) + 0.5


def _silu(x):
    return x * _sigmoid(x)


def _softplus(x):
    return jnp.maximum(x, 0.0) + jnp.log1p(jnp.exp(-jnp.abs(x)))


def _rms(x, eps=EPS):
    return x * lax.rsqrt(jnp.mean(x * x, axis=-1, keepdims=True) + eps)


def _params(*sem):
    return pltpu.CompilerParams(dimension_semantics=sem, vmem_limit_bytes=VMEM_LIMIT)


def _dwconv3_rows(full, w_ref, n, halo=HALO):
    rows = full.shape[0]
    dn = pltpu.roll(full, 1, axis=0)
    up = pltpu.roll(full, rows - 1, axis=0)
    out = dn * w_ref[0:1, :] + full * w_ref[1:2, :] + up * w_ref[2:3, :]
    return out[halo:halo + n]


def _ada_kernel(c_ref, w_ref, b_ref, o_ref):
    o_ref[...] = _dot(_silu(c_ref[...]), w_ref[...]) + b_ref[...]


def _ada_call(cc, w_ada, b_ada):
    depth, d, n6 = w_ada.shape
    r = cc.shape[0]
    tn = 1536
    return pl.pallas_call(
        _ada_kernel,
        grid=(depth, n6 // tn),
        in_specs=[
            pl.BlockSpec((r, d), lambda l, j: (0, 0)),
            pl.BlockSpec((None, d, tn), lambda l, j: (l, 0, j)),
            pl.BlockSpec((None, 1, tn), lambda l, j: (l, 0, j)),
        ],
        out_specs=pl.BlockSpec((None, r, tn), lambda l, j: (l, 0, j)),
        out_shape=jax.ShapeDtypeStruct((depth, r, n6), F32),
        compiler_params=_params("parallel", "parallel"),
        name="ada_mod",
    )(cc, w_ada, b_ada.reshape(depth, 1, n6))


def _inproj_kernel(x_ref, g_ref, sh_ref, sc_ref, w_ref, wab_ref, o_ref, ab_ref, h_ref):
    j = pl.program_id(2)

    @pl.when(j == 0)
    def _():
        h = (_rms(x_ref[...]) * g_ref[...]) * (1.0 + sc_ref[...]) + sh_ref[...]
        h_ref[...] = h.astype(BF16)
        ab_ref[...] = jnp.dot(h_ref[...], wab_ref[...], preferred_element_type=F32)

    res = jnp.dot(h_ref[...], w_ref[j], preferred_element_type=F32)
    half = res.shape[1] // 2
    o_ref[:, :half] = res[:, :half].astype(o_ref.dtype)
    act = res[:, half:]
    sg = _sigmoid(act)
    is_z = lax.broadcasted_iota(jnp.int32, (1, half), 1) >= half // 2
    o_ref[:, half:] = jnp.where(j > 0, sg, jnp.where(is_z, act * sg, act)).astype(o_ref.dtype)


def _inproj_call(x, g, mod, mrow, w, wab, tm):
    b, t, d = x.shape
    n = w.shape[0] * w.shape[2]
    tn = P_TILE
    return pl.pallas_call(
        _inproj_kernel,
        grid=(b, t // tm, n // tn),
        in_specs=[
            pl.BlockSpec((None, tm, d), lambda bi, i, j: (bi, i, 0)),
            pl.BlockSpec((1, d), lambda bi, i, j: (0, 0)),
            pl.BlockSpec((None, 1, d), lambda bi, i, j: (mrow(bi), 0, 0)),
            pl.BlockSpec((None, 1, d), lambda bi, i, j: (mrow(bi), 0, 1)),
            pl.BlockSpec(w.shape, lambda bi, i, j: (0, 0, 0), pipeline_mode=pl.Buffered(1)),
            pl.BlockSpec((d, LANE), lambda bi, i, j: (0, 0)),
        ],
        out_specs=[pl.BlockSpec((None, tm, tn), lambda bi, i, j: (bi, i, j)),
                   pl.BlockSpec((None, tm, LANE), lambda bi, i, j: (bi, i, 0))],
        out_shape=[jax.ShapeDtypeStruct((b, t, n), BF16), jax.ShapeDtypeStruct((b, t, LANE), F32)],
        scratch_shapes=[pltpu.VMEM((tm, d), BF16)],
        compiler_params=_params("parallel", "parallel", "arbitrary"),
        name="inproj",
    )(x, g, mod, mod, w, wab)


def _swap_pairs(x, even):
    n = x.shape[-1]
    return jnp.where(even, pltpu.roll(x, n - 1, axis=1), pltpu.roll(x, 1, axis=1))


def _gdn_prep_kernel(*refs, nt, use_rope):
    x_ref, xp_ref, xn_ref, cw_ref = refs[:4]
    cos_ref, sin_ref = refs[4:6] if use_rope else (None, None)
    o_ref = refs[-1]
    i = pl.program_id(1)
    tr = x_ref.shape[0]
    hd = DN_HEAD_DIM
    has_prev = (i > 0).astype(F32)
    has_next = (i < nt - 1).astype(F32)
    full = jnp.concatenate([xp_ref[...].astype(F32) * has_prev, x_ref[...].astype(F32),
                            xn_ref[...].astype(F32) * has_next], axis=0)
    act = _silu(_dwconv3_rows(full, cw_ref, tr, HALO_P))
    even = (lax.broadcasted_iota(jnp.int32, (tr, hd), 1) % 2) == 0
    for off, scale in ((0, hd ** -0.5), (DN_WIDTH, 1.0)):
        for h in range(DN_HEADS):
            sl = slice(off + h * hd, off + (h + 1) * hd)
            xh = act[:, sl]
            xh = xh * lax.rsqrt(jnp.sum(xh * xh, axis=-1, keepdims=True) + EPS)
            if use_rope:
                xh = xh * cos_ref[...] + _swap_pairs(xh, even) * sin_ref[...]
            o_ref[:, sl] = (xh * scale).astype(o_ref.dtype)
    o_ref[:, 2 * DN_WIDTH:] = act[:, 2 * DN_WIDTH:].astype(o_ref.dtype)


def _gdn_prep_call(p, conv_w, rope):
    b, t, _ = p.shape
    tr = GDN_TILE_CHUNKS * DN_CHUNK
    nt = t // tr
    hb = tr // HALO_P
    nhb = t // HALO_P
    qkv_w = 3 * DN_WIDTH
    in_specs = [
        pl.BlockSpec((None, tr, qkv_w), lambda bi, i: (bi, i, 0)),
        pl.BlockSpec((None, HALO_P, qkv_w), lambda bi, i: (bi, jnp.maximum(i * hb - 1, 0), 0)),
        pl.BlockSpec((None, HALO_P, qkv_w), lambda bi, i: (bi, jnp.minimum(i * hb + hb, nhb - 1), 0)),
        pl.BlockSpec(conv_w.shape, lambda bi, i: (0, 0)),
    ]
    args = [p, p, p, conv_w]
    if rope is not None:
        in_specs += [pl.BlockSpec((tr, DN_HEAD_DIM), lambda bi, i: (i, 0))] * 2
        args += [rope[0], rope[1]]
    return pl.pallas_call(
        functools.partial(_gdn_prep_kernel, nt=nt, use_rope=rope is not None),
        grid=(b, nt),
        in_specs=in_specs,
        out_specs=pl.BlockSpec((None, tr, qkv_w), lambda bi, i: (bi, i, 0)),
        out_shape=jax.ShapeDtypeStruct((b, t, qkv_w), p.dtype),
        compiler_params=_params("parallel", "parallel"),
        name="gdn_prep",
    )(*args)


def _gdn_kernel(*refs, nt, emit_state):
    it = iter(refs)
    xs = [[next(it) for _ in range(2)] for _ in range(2)]
    par_ref = next(it)
    s0_ref = next(it)
    o_refs = [next(it), next(it)]
    sout_ref = next(it) if emit_state else None
    s_scr, uw_scr, qkd_scr = (next(it) for _ in range(3))

    c = DN_CHUNK
    nh = DN_HEADS
    hd = DN_HEAD_DIM
    nch = GDN_TILE_CHUNKS
    tr = nch * c
    cw = nh * c
    lc = c.bit_length() - 1
    i = pl.program_id(1)

    @pl.when(i == 0)
    def _():
        s_scr[...] = s0_ref[...]

    ri = lax.broadcasted_iota(jnp.int32, (c, cw), 0)
    li = lax.broadcasted_iota(jnp.int32, (c, cw), 1)
    jj = li & (c - 1)
    seg = li >> lc
    eye_cat = ri == jj
    bd_sq = jnp.where((lax.broadcasted_iota(jnp.int32, (cw, cw), 0) >> lc)
                      == (lax.broadcasted_iota(jnp.int32, (cw, cw), 1) >> lc), 1.0, 0.0).astype(BF16)
    bd_k = jnp.where((lax.broadcasted_iota(jnp.int32, (cw, nh * hd), 0) >> lc)
                     == (lax.broadcasted_iota(jnp.int32, (cw, nh * hd), 1) // hd), 1.0, 0.0).astype(BF16)
    rowi = lax.broadcasted_iota(jnp.int32, (tr, LANE), 0) & (c - 1)
    neg_a = -jnp.exp(par_ref[0:1, :])
    dt_b = par_ref[1:2, :]

    def col_bcast(x, j, width):
        return jnp.broadcast_to(x[:, j:j + 1], (x.shape[0], width))

    def colcat(x, base):
        out = col_bcast(x, base + nh - 1, cw)
        for h in range(nh - 2, -1, -1):
            out = jnp.where(seg == h, col_bcast(x, base + h, cw), out)
        return out

    def block_diag(m, mask):
        return jnp.concatenate([m.astype(BF16)] * nh, axis=0) * mask

    nar = []
    for d in range(2):
        ab = xs[d][1][...]
        gc = neg_a * _softplus(ab + dt_b)
        beta = pltpu.roll(_sigmoid(ab), LANE - 2 * nh, axis=1)
        sft = 1
        while sft < c:
            if d == 0:
                gc = gc + jnp.where(rowi >= sft, pltpu.roll(gc, sft, axis=0), 0.0)
            else:
                gc = gc + jnp.where(rowi < c - sft, pltpu.roll(gc, tr - sft, axis=0), 0.0)
            sft *= 2
        e1 = jnp.exp(gc)
        nar.append(dict(gc=gc, e1=e1, beta=beta, be=beta * e1))

    st = {}

    def setup(key):
        d, ch = key
        rs = slice(ch * c, (ch + 1) * c)
        base = d * nh
        incl = (ri >= jj) if d == 0 else (ri <= jj)
        gcc = colcat(nar[d]["gc"][rs], base)
        gcr = jnp.broadcast_to(jnp.sum(jnp.where(eye_cat, gcc, 0.0), axis=0, keepdims=True), (c, cw))
        decay = jnp.where(incl, jnp.exp(jnp.where(incl, gcc - gcr, 0.0)), 0.0)
        kf = xs[d][0][rs, DN_WIDTH:2 * DN_WIDTH]
        prod = _dot_nt(jnp.concatenate([xs[d][0][rs, :DN_WIDTH], kf], axis=0), block_diag(kf, bd_k))
        qkd_scr[d, ch] = prod[:c] * decay
        amat = jnp.where(incl & (~eye_cat), colcat(nar[d]["beta"][rs], base) * prod[c:] * decay, 0.0)
        lvl1 = ((ri >> 1) == (jj >> 1))
        st[key] = dict(amat=amat.astype(BF16), tinv=jnp.where(eye_cat, 1.0, 0.0) - jnp.where(lvl1, amat, 0.0))

    def invert_level(keys, s):
        ls = s.bit_length() - 1
        lvl = jnp.where(((ri >> (ls + 1)) == (jj >> (ls + 1))) & ((ri >> ls) != (jj >> ls)), 1.0, 0.0).astype(BF16)
        for key in keys:
            p = st[key]
            p["t_bf"] = p["tinv"].astype(BF16)
            p["x"] = _dot(p["t_bf"], block_diag(p["amat"] * lvl, bd_sq))
        for key in keys:
            p = st[key]
            p["tinv"] = p["tinv"] - _dot(p["x"], block_diag(p["t_bf"], bd_sq))

    def solve(key):
        d, ch = key
        rs = slice(ch * c, (ch + 1) * c)
        base = d * nh
        x_ref = xs[d][0]
        rhs = jnp.concatenate([
            jnp.concatenate([x_ref[rs, 2 * DN_WIDTH + h * hd:2 * DN_WIDTH + (h + 1) * hd].astype(F32)
                             * col_bcast(nar[d]["beta"][rs], base + h, hd),
                             x_ref[rs, DN_WIDTH + h * hd:DN_WIDTH + (h + 1) * hd].astype(F32)
                             * col_bcast(nar[d]["be"][rs], base + h, hd)], axis=1)
            for h in range(nh)], axis=0)
        uw_scr[d, ch] = _dot(block_diag(st.pop(key)["tinv"], bd_sq), rhs)

    states = [[s_scr[d, h] for h in range(nh)] for d in range(2)]

    def scan_step(step):
        cur = [(0, step), (1, nch - 1 - step)]
        loc = {}
        for d, ch in cur:
            rs = slice(ch * c, (ch + 1) * c)
            base = d * nh
            gc_c = nar[d]["gc"][rs]
            gtot = gc_c[c - 1:c] if d == 0 else gc_c[0:1]
            e2 = jnp.exp(gtot - gc_c)
            r_ = []
            for h in range(nh):
                hs = slice(h * hd, (h + 1) * hd)
                w_h = uw_scr[d, ch, h * c:(h + 1) * c, hd:]
                qd_h = xs[d][0][rs, hs].astype(F32) * col_bcast(nar[d]["e1"][rs], base + h, hd)
                r_.append(_dot(jnp.concatenate([w_h, qd_h], axis=0), states[d][h]))
            loc[d] = dict(r=r_, e2=e2, gtot=gtot)
        for d, ch in cur:
            loc[d]["vn"] = [uw_scr[d, ch, h * c:(h + 1) * c, :hd] - loc[d]["r"][h][:c] for h in range(nh)]
        for d, ch in cur:
            rs = slice(ch * c, (ch + 1) * c)
            base = d * nh
            vn = loc[d]["vn"]
            o2 = _dot(block_diag(qkd_scr[d, ch], bd_sq), jnp.concatenate(vn, axis=0))
            for h in range(nh):
                hs = slice(h * hd, (h + 1) * hd)
                kd_h = (xs[d][0][rs, DN_WIDTH + h * hd:DN_WIDTH + (h + 1) * hd].astype(F32)
                        * col_bcast(loc[d]["e2"], base + h, hd))
                o_refs[d][rs, hs] = (loc[d]["r"][h][c:] + o2[h * c:(h + 1) * c]).astype(o_refs[d].dtype)
                eg = jnp.exp(jnp.broadcast_to(loc[d]["gtot"][0:1, base + h:base + h + 1], (hd, hd)))
                states[d][h] = states[d][h] * eg + _dot_tn(kd_h, vn[h])

    probs = [(d, ch) for ch in range(nch) for d in range(2)]
    for key in probs:
        setup(key)
    for e in range(1, lc):
        invert_level(probs, 1 << e)
    for key in probs:
        solve(key)
    for step in range(nch):
        scan_step(step)
    for d in range(2):
        for h in range(nh):
            s_scr[d, h] = states[d][h]

    if emit_state:
        @pl.when(i == nt - 1)
        def _():
            sout_ref[...] = s_scr[...]


def _gdn_call(qkv, ab, par, s0, emit_state):
    b, t, qkv_w = qkv.shape
    c = DN_CHUNK
    tr = GDN_TILE_CHUNKS * c
    nt = t // tr

    def tile_of(d, i):
        return i if d == 0 else nt - 1 - i

    in_specs = []
    args = []
    for d in range(2):
        in_specs += [
            pl.BlockSpec((None, tr, qkv_w), lambda bi, i, d=d: (bi, tile_of(d, i), 0)),
            pl.BlockSpec((None, tr, LANE), lambda bi, i, d=d: (bi, tile_of(d, i), 0)),
        ]
        args += [qkv, ab]
    in_specs += [pl.BlockSpec(par.shape, lambda bi, i: (0, 0))]
    args += [par]
    sshape = (2, DN_HEADS, DN_HEAD_DIM, DN_HEAD_DIM)
    in_specs += [pl.BlockSpec((None,) + sshape, lambda bi, i: (bi, 0, 0, 0, 0))]
    args += [s0]
    out_specs = [
        pl.BlockSpec((None, tr, DN_WIDTH), lambda bi, i: (bi, i, 0)),
        pl.BlockSpec((None, tr, DN_WIDTH), lambda bi, i: (bi, nt - 1 - i, 0)),
    ]
    out_shape = [jax.ShapeDtypeStruct((b, t, DN_WIDTH), qkv.dtype)] * 2
    if emit_state:
        out_specs += [pl.BlockSpec((None,) + sshape, lambda bi, i: (bi, 0, 0, 0, 0))]
        out_shape += [jax.ShapeDtypeStruct((b,) + sshape, F32)]
    cw = DN_HEADS * c
    scratch = [
        pltpu.VMEM(sshape, F32),
        pltpu.VMEM((2, GDN_TILE_CHUNKS, cw, 2 * DN_HEAD_DIM), F32),
        pltpu.VMEM((2, GDN_TILE_CHUNKS, c, cw), F32),
    ]
    return pl.pallas_call(
        functools.partial(_gdn_kernel, nt=nt, emit_state=emit_state),
        grid=(b, nt),
        in_specs=in_specs,
        out_specs=out_specs,
        out_shape=out_shape,
        scratch_shapes=scratch,
        compiler_params=_params("parallel", "arbitrary"),
        name="gdn_ctx" if emit_state else "gdn_lat",
    )(*args)


def _na_kernel(q_ref, k_ref, v_ref, kc_ref, vc_ref, bias_ref, o_ref, *, rows):
    w = GRID_W
    nl = NA_WIN_R * w
    nr = NA_ROWS_PER_ITER
    scale = NA_HEAD_DIM ** -0.5 * LOG2E
    kc = kc_ref[...].astype(BF16)
    vc = vc_ref[...].astype(BF16)

    def stack_heads(q2):
        sel = lax.broadcasted_iota(jnp.int32, q2.shape, 1) < NA_HEAD_DIM
        q2 = q2.astype(BF16)
        zero = jnp.zeros_like(q2)
        return jnp.concatenate([jnp.where(sel, q2, zero), jnp.where(sel, zero, q2)], axis=0)

    def unstack_heads(x):
        n = x.shape[0] // 2
        sel = lax.broadcasted_iota(jnp.int32, (n, LANE), 1) < NA_HEAD_DIM
        return jnp.where(sel, jnp.broadcast_to(x[:n], (n, LANE)), jnp.broadcast_to(x[n:], (n, LANE)))

    def row_block(it, carry):
        rr = [it * nr + u for u in range(nr)]
        rss = [jnp.clip(r - NA_WIN_R // 2, 0, rows - NA_WIN_R) for r in rr]
        rsl = [pl.ds(pl.multiple_of(r * w, w), w) for r in rr]
        wsl = [pl.ds(pl.multiple_of(rs * w, w), nl) for rs in rss]
        q_all = q_ref[pl.ds(pl.multiple_of(it * (nr * w), nr * w), nr * w), :].astype(F32) * scale
        s_c = _dot_nt(stack_heads(q_all), kc)
        s = [_dot_nt(stack_heads(q_all[u * w:(u + 1) * w]), k_ref[wsl[u], :]) + bias_ref[rss[u] - rr[u] + NA_WIN_R - 1]
             for u in range(nr)]
        m_c = jnp.max(s_c, axis=-1, keepdims=True)
        m_l = [jnp.max(x, axis=-1, keepdims=True) for x in s]
        p_c = jnp.exp2(s_c - m_c)
        p = [jnp.exp2(x - m) for x, m in zip(s, m_l)]
        o_c = unstack_heads(_dot(p_c, vc))
        o_l = [_dot(p[u], v_ref[wsl[u], :]) for u in range(nr)]
        l_c = unstack_heads(jnp.sum(p_c, axis=-1, keepdims=True))
        m_c = unstack_heads(m_c)
        for u in range(nr):
            usl = slice(u * w, (u + 1) * w)
            l_u = unstack_heads(jnp.sum(p[u], axis=-1, keepdims=True))
            m_u = unstack_heads(m_l[u])
            m = jnp.maximum(m_u, m_c[usl])
            a_l = jnp.exp2(m_u - m)
            a_c = jnp.exp2(m_c[usl] - m)
            out = (unstack_heads(o_l[u]) * a_l + o_c[usl] * a_c) / (l_u * a_l + l_c[usl] * a_c)
            o_ref[rsl[u], :] = out.astype(o_ref.dtype)
        return carry

    lax.fori_loop(0, rows // nr, row_block, 0)


def _na_call(p, pc, bias):
    b, t, _ = p.shape
    l = pc.shape[1]
    rows = t // GRID_W
    npair = NA_HEADS // 2
    return pl.pallas_call(
        functools.partial(_na_kernel, rows=rows),
        grid=(npair, b),
        in_specs=[
            pl.BlockSpec((None, t, LANE), lambda j, bi: (bi, 0, NAQ_SLOT * (P_SLOT // LANE) + j)),
            pl.BlockSpec((None, t, LANE), lambda j, bi: (bi, 0, NAK_SLOT * (P_SLOT // LANE) + j)),
            pl.BlockSpec((None, t, LANE), lambda j, bi: (bi, 0, NAV_SLOT * (P_SLOT // LANE) + j)),
            pl.BlockSpec((None, l, LANE), lambda j, bi: (bi, 0, NAK_SLOT * (P_SLOT // LANE) + j)),
            pl.BlockSpec((None, l, LANE), lambda j, bi: (bi, 0, NAV_SLOT * (P_SLOT // LANE) + j)),
            pl.BlockSpec((None,) + bias.shape[1:], lambda j, bi: (j, 0, 0, 0)),
        ],
        out_specs=pl.BlockSpec((None, t, LANE), lambda j, bi: (bi, 0, j)),
        out_shape=jax.ShapeDtypeStruct((b, t, NA_WIDTH), p.dtype),
        compiler_params=_params("parallel", "parallel"),
        name="na_lat",
    )(p, p, p, pc, pc, bias)


def _ctx_attn_kernel(q_ref, k_ref, v_ref, o_ref):
    scale = NA_HEAD_DIM ** -0.5
    q2 = q_ref[...] * scale
    k2 = k_ref[...]
    v2 = v_ref[...]
    head0 = lax.broadcasted_iota(jnp.int32, q2.shape, 1) < NA_HEAD_DIM
    out = None
    for hh in range(2):
        sel = head0 if hh == 0 else ~head0
        s = _dot_nt(jnp.where(sel, q2, jnp.zeros_like(q2)), k2)
        p = jnp.exp(s - jnp.max(s, axis=-1, keepdims=True))
        o = _dot(p, v2) / jnp.sum(p, axis=-1, keepdims=True)
        out = o if out is None else jnp.where(sel, o, out)
    o_ref[...] = out.astype(o_ref.dtype)


def _ctx_attn_call(pc):
    b, l, _ = pc.shape
    npair = NA_HEADS // 2
    return pl.pallas_call(
        _ctx_attn_kernel,
        grid=(b, npair),
        in_specs=[
            pl.BlockSpec((None, l, LANE), lambda bi, j: (bi, 0, NAQ_SLOT * (P_SLOT // LANE) + j)),
            pl.BlockSpec((None, l, LANE), lambda bi, j: (bi, 0, NAK_SLOT * (P_SLOT // LANE) + j)),
            pl.BlockSpec((None, l, LANE), lambda bi, j: (bi, 0, NAV_SLOT * (P_SLOT // LANE) + j)),
        ],
        out_specs=pl.BlockSpec((None, l, LANE), lambda bi, j: (bi, 0, j)),
        out_shape=jax.ShapeDtypeStruct((b, l, NA_WIDTH), pc.dtype),
        compiler_params=_params("parallel", "parallel"),
        name="ctx_attn",
    )(pc, pc, pc)


def _merge_kernel(of_ref, ob_ref, z_ref, bg_ref, cg_ref, hh_ref, cgp_ref, cgn_ref, hhp_ref, hhn_ref, yc_ref,
                  ga_ref, gb_ref, gcg_ref, x_ref, gt_ref, wpa_ref, wpb_ref, wpc_ref, wo_ref, ng_ref, scw_ref,
                  o_ref, *, nt):
    i = pl.program_id(1)
    tm = x_ref.shape[0]
    o = of_ref[...].astype(F32) + ob_ref[...].astype(F32)
    sz = z_ref[...].astype(F32)
    ya = []
    for h in range(DN_HEADS):
        sl = slice(h * DN_HEAD_DIM, (h + 1) * DN_HEAD_DIM)
        ya.append(_rms(o[:, sl]) * ng_ref[...] * sz[:, sl])
    y_a = jnp.concatenate(ya, axis=1)
    has_prev = (i > 0).astype(F32)
    has_next = (i < nt - 1).astype(F32)
    f32 = lambda r: r[...].astype(F32)
    full = jnp.concatenate([f32(cgp_ref) * f32(hhp_ref) * has_prev, f32(cg_ref) * f32(hh_ref),
                            f32(cgn_ref) * f32(hhn_ref) * has_next], axis=0)
    y_b = f32(bg_ref) * _dwconv3_rows(full, scw_ref, tm, HALO_P)
    y = (f32(ga_ref) * _dot(y_a, wpa_ref[...]) + f32(gb_ref) * _dot(y_b, wpb_ref[...])
         + f32(gcg_ref) * _dot(yc_ref[...], wpc_ref[...]))
    o_ref[...] = x_ref[...] + gt_ref[...] * _dot(y, wo_ref[...])


def _merge_call(of, ob, p, yc, x, mod, mrow, wpa, wpb, wpc, wo, ng, scw, tm):
    b, t, d = x.shape
    nt = t // tm
    hb = tm // HALO_P
    nhb = t // HALO_P
    w5 = P_SLOT

    def tile(width, col):
        return pl.BlockSpec((None, tm, width), lambda bi, i: (bi, i, col))

    def prev(col):
        return pl.BlockSpec((None, HALO_P, w5), lambda bi, i: (bi, jnp.maximum(i * hb - 1, 0), col))

    def nxt(col):
        return pl.BlockSpec((None, HALO_P, w5), lambda bi, i: (bi, jnp.minimum(i * hb + hb, nhb - 1), col))

    def full(a):
        return pl.BlockSpec(a.shape, lambda bi, i: (0,) * a.ndim)

    in_specs = [
        tile(DN_WIDTH, 0), tile(DN_WIDTH, 0),
        tile(w5, Z_SLOT), tile(w5, BG_SLOT), tile(w5, CG_SLOT), tile(w5, HH_SLOT),
        prev(CG_SLOT), nxt(CG_SLOT), prev(HH_SLOT), nxt(HH_SLOT),
        tile(NA_WIDTH, 0),
        tile(d, GATE_SLOTS[0] // 2), tile(d, GATE_SLOTS[1] // 2), tile(d, GATE_SLOTS[2] // 2),
        tile(d, 0),
        pl.BlockSpec((None, 1, d), lambda bi, i: (mrow(bi), 0, 2)),
        full(wpa), full(wpb), full(wpc), full(wo), full(ng), full(scw),
    ]
    return pl.pallas_call(
        functools.partial(_merge_kernel, nt=nt),
        grid=(b, nt),
        in_specs=in_specs,
        out_specs=pl.BlockSpec((None, tm, d), lambda bi, i: (bi, i, 0)),
        out_shape=jax.ShapeDtypeStruct((b, t, d), F32),
        compiler_params=_params("parallel", "parallel"),
        name="merge",
    )(of, ob, p, p, p, p, p, p, p, p, yc, p, p, p, x, mod, wpa, wpb, wpc, wo, ng, scw)


def _ffn_kernel(*refs, nt, nk, fc, final):
    it = iter(refs)
    x_ref, xp_ref, xn_ref, g_ref, sh_ref, sc_ref, gt_ref = (next(it) for _ in range(7))
    wup_ref, cw_ref, cb_ref, wd_ref = (next(it) for _ in range(4))
    fg_ref = next(it) if final else None
    o_ref, h_scr, act_scr = next(it), next(it), next(it)
    i = pl.program_id(1)
    tm = x_ref.shape[0]
    dff = nk * fc

    xfull = jnp.concatenate([xp_ref[...], x_ref[...], xn_ref[...]], axis=0)
    h = (_rms(xfull) * g_ref[...]) * (1.0 + sc_ref[...]) + sh_ref[...]
    h_scr[...] = h.astype(BF16)

    row = lax.broadcasted_iota(jnp.int32, (tm + 2 * HALO, 1), 0)
    keep = ((row >= HALO) | (i > 0)) & ((row < tm + HALO) | (i < nt - 1))

    def up_proj(k):
        return [jnp.where(keep, jnp.dot(h_scr[...], wup_ref[:, c0:c0 + fc], preferred_element_type=F32), 0.0)
                for c0 in (k * fc, dff + k * fc)]

    ups = up_proj(0)
    g0 = 0
    for k in range(nk):
        nxt = up_proj(k + 1) if k + 1 < nk else None
        a = _dwconv3_rows(ups[0], cw_ref[:, k * fc:(k + 1) * fc], tm) + cb_ref[:, k * fc:(k + 1) * fc]
        bb = (_dwconv3_rows(ups[1], cw_ref[:, dff + k * fc:dff + (k + 1) * fc], tm)
              + cb_ref[:, dff + k * fc:dff + (k + 1) * fc])
        act_scr[:, k * fc:(k + 1) * fc] = (_silu(a) * bb).astype(BF16)
        if (k + 1) % FFN_DOWN_GROUP == 0 or k == nk - 1:
            g1 = (k + 1) * fc
            contrib = jnp.dot(act_scr[:, g0:g1], wd_ref[g0:g1, :], preferred_element_type=F32)
            if g0 == 0:
                o_ref[...] = contrib
            else:
                o_ref[...] += contrib
            g0 = g1
        ups = nxt

    out = x_ref[...] + gt_ref[...] * o_ref[...]
    if final:
        out = _rms(out) * fg_ref[...]
    o_ref[...] = out


def _ffn_call(x, g, mod, mrow, w_up, cw, cb, w_down, tm, final_g=None):
    b, t, d = x.shape
    dff = w_down.shape[0]
    fc = 256
    nk = dff // fc
    nt = t // tm
    hb = tm // HALO
    nhb = t // HALO
    final = final_g is not None

    def resident(a):
        return pl.BlockSpec(a.shape, lambda bi, i: (0,) * a.ndim, pipeline_mode=pl.Buffered(1))

    in_specs = [
        pl.BlockSpec((None, tm, d), lambda bi, i: (bi, i, 0)),
        pl.BlockSpec((None, HALO, d), lambda bi, i: (bi, jnp.maximum(i * hb - 1, 0), 0)),
        pl.BlockSpec((None, HALO, d), lambda bi, i: (bi, jnp.minimum(i * hb + hb, nhb - 1), 0)),
        pl.BlockSpec((1, d), lambda bi, i: (0, 0)),
        pl.BlockSpec((None, 1, d), lambda bi, i: (mrow(bi), 0, 3)),
        pl.BlockSpec((None, 1, d), lambda bi, i: (mrow(bi), 0, 4)),
        pl.BlockSpec((None, 1, d), lambda bi, i: (mrow(bi), 0, 5)),
        resident(w_up), resident(cw), resident(cb), resident(w_down),
    ]
    args = [x, x, x, g, mod, mod, mod, w_up, cw, cb, w_down]
    if final:
        in_specs.append(pl.BlockSpec((1, d), lambda bi, i: (0, 0)))
        args.append(final_g)
    return pl.pallas_call(
        functools.partial(_ffn_kernel, nt=nt, nk=nk, fc=fc, final=final),
        grid=(b, nt),
        in_specs=in_specs,
        out_specs=pl.BlockSpec((None, tm, d), lambda bi, i: (bi, i, 0)),
        out_shape=jax.ShapeDtypeStruct((b, t, d), F32),
        scratch_shapes=[pltpu.VMEM((tm + 2 * HALO, d), BF16), pltpu.VMEM((tm, dff), BF16)],
        compiler_params=_params("parallel", "parallel"),
        name="ffn_final" if final else "ffn",
    )(*args)


def _rope_tables(t):
    tok = jnp.arange(t, dtype=jnp.int32)
    rows = (tok // GRID_W).astype(F32)
    cols = (tok % GRID_W).astype(F32)
    nf = DN_HEAD_DIM // 4
    inv = ROPE_BASE ** (-jnp.arange(nf, dtype=F32) / nf)
    ang = jnp.concatenate([rows[:, None] * inv, cols[:, None] * inv], axis=-1)
    cos = jnp.repeat(jnp.cos(ang), 2, axis=-1)
    sin = jnp.repeat(jnp.sin(ang), 2, axis=-1)
    sign = jnp.where(jnp.arange(DN_HEAD_DIM) % 2 == 0, -1.0, 1.0).astype(F32)
    return cos, sin * sign


def _na_bias_table(rpb):
    w = GRID_W
    depth = rpb.shape[0]
    qc = np.arange(w)[:, None]
    kc = np.arange(w)[None, :]
    cstart = np.clip(qc - NA_WIN_C // 2, 0, w - NA_WIN_C)
    valid = (kc >= cstart) & (kc < cstart + NA_WIN_C)
    onehot = ((kc - qc + NA_WIN_C - 1)[None] == np.arange(2 * NA_WIN_C - 1)[:, None, None]) & valid[None]
    tbl = jnp.einsum("lhdj,jck->lhdck", rpb, jnp.asarray(onehot, F32), precision=lax.Precision.HIGHEST)
    tbl = tbl * LOG2E + jnp.asarray(np.where(valid, 0.0, NEG), F32)
    tbl = jnp.stack([tbl[:, :, d0:d0 + NA_WIN_R] for d0 in range(NA_WIN_R)], axis=2)
    tbl = tbl.transpose(0, 1, 2, 4, 3, 5).reshape(depth, NA_HEADS // 2, 2, NA_WIN_R, w, NA_WIN_R * w)
    return tbl.transpose(0, 1, 3, 2, 4, 5).reshape(depth, NA_HEADS // 2, NA_WIN_R, 2 * w, NA_WIN_R * w)


def _permute_w_in(w_in):
    s = np.cumsum([0, 3 * DN_WIDTH, DN_WIDTH, 2 * DN_HEADS, 2 * DN_HEADS, 3 * SC_WIDTH, 3 * NA_WIDTH, 3 * 1024])
    qkv, z, a, b, sc, na, gates = (w_in[..., s[j]:s[j + 1]] for j in range(7))
    bg, cg, hh = (sc[..., j * P_SLOT:(j + 1) * P_SLOT] for j in range(3))
    naq, nak, nav = (na[..., j * P_SLOT:(j + 1) * P_SLOT] for j in range(3))
    ga, gb, gc = (gates[..., j * 2 * P_SLOT:(j + 1) * 2 * P_SLOT] for j in range(3))
    pad = jnp.zeros(w_in.shape[:-1] + (LANE - 4 * DN_HEADS,), w_in.dtype)
    return (jnp.concatenate([qkv, z, bg, cg, ga, hh, naq, gb, nak, nav, gc], axis=-1),
            jnp.concatenate([a, b, pad], axis=-1))


def kernel(x, c, ctx, c_ctx, norm1_g, norm2_g, w_ada, b_ada, w_in, dn_conv_w, dn_a_log, dn_dt_bias, dn_norm_g, sc_conv_w, na_rpb, w_pa, w_pb, w_pc, w_o, w_up, ffn_conv_w, ffn_conv_b, w_down, final_norm_g):
    bsz, t, d = x.shape
    depth = w_in.shape[0]
    assert d == 1024 and t % (NA_WIN_R * GRID_W) == 0 and ctx.shape[1] % (GDN_TILE_CHUNKS * DN_CHUNK) == 0

    mod_rows = -(-(bsz + 1) // SUBLANE) * SUBLANE
    cc = jnp.zeros((mod_rows, d), F32).at[:bsz].set(c).at[bsz].set(c_ctx)
    mod = _ada_call(cc, w_ada, b_ada).reshape(depth, mod_rows, 1, 6 * d)
    lat_row = lambda bi: bi
    ctx_row = lambda bi: bsz

    w_in_p, w_ab = _permute_w_in(w_in.astype(BF16))
    w_in_p = w_in_p.reshape(depth, d, -1, P_TILE).transpose(0, 2, 1, 3)
    rope = _rope_tables(t)
    na_bias = _na_bias_table(na_rpb)
    par = jnp.zeros((depth, SUBLANE, LANE), F32)
    par = par.at[:, 0, :2 * DN_HEADS].set(dn_a_log.reshape(depth, -1)).at[:, 1, :2 * DN_HEADS].set(dn_dt_bias.reshape(depth, -1))
    s_zero = jnp.zeros((bsz, 2, DN_HEADS, DN_HEAD_DIM, DN_HEAD_DIM), F32)
    tm_lat = 1024 if t % 1024 == 0 else 512
    tm_ctx = ctx.shape[1]

    xc = ctx
    for l in range(depth):
        need_ctx = l < depth - 1
        g1 = norm1_g[l][None]
        g2 = norm2_g[l][None]
        wpa, wpb, wpc, wo = (a[l].astype(BF16) for a in (w_pa, w_pb, w_pc, w_o))
        wup, wdn = w_up[l].astype(BF16), w_down[l].astype(BF16)
        ng = dn_norm_g[l][None]
        p, ab = _inproj_call(x, g1, mod[l], lat_row, w_in_p[l], w_ab[l], tm_lat)
        lc_all = bsz * tm_ctx
        pc, abc = _inproj_call(xc.reshape(1, lc_all, d), g1, mod[l], ctx_row, w_in_p[l], w_ab[l],
                               tm_lat if lc_all % tm_lat == 0 else tm_ctx)
        pc, abc = pc.reshape(bsz, tm_ctx, -1), abc.reshape(bsz, tm_ctx, -1)
        ocf, ocb, s_ctx = _gdn_call(_gdn_prep_call(pc, dn_conv_w[l], None), abc, par[l], s_zero, True)
        olf, olb = _gdn_call(_gdn_prep_call(p, dn_conv_w[l], rope), ab, par[l], s_ctx, False)
        y_c = _na_call(p, pc, na_bias[l])
        x = _merge_call(olf, olb, p, y_c, x, mod[l], lat_row, wpa, wpb, wpc, wo, ng, sc_conv_w[l], 512)
        fg = final_norm_g[None] if l == depth - 1 else None
        x = _ffn_call(x, g2, mod[l], lat_row, wup, ffn_conv_w[l], ffn_conv_b[l][None], wdn, tm_lat, fg)
        if need_ctx:
            yc_c = _ctx_attn_call(pc)
            xc = _merge_call(ocf, ocb, pc, yc_c, xc, mod[l], ctx_row, wpa, wpb, wpc, wo, ng, sc_conv_w[l], tm_ctx)
            xc = _ffn_call(xc, g2, mod[l], ctx_row, wup, ffn_conv_w[l], ffn_conv_b[l][None], wdn, tm_ctx)
    return x
```
